```python
import jax, jax.numpy as jnp
from jax import lax
import numpy as np

D_MODEL = 2048
BATCH = 2
SEQ = 4096
DEPTH = 4

GRID_W = 64
CTX_LEN = 256
N_MIXERS = 2
N_GLA_LAYERS = (DEPTH + 1) // 2
N_FNET_LAYERS = DEPTH // 2
GLA_HEADS = 4
GLA_KEY_DIM = D_MODEL // 2
GLA_VALUE_DIM = D_MODEL
GLA_HEAD_K = GLA_KEY_DIM // GLA_HEADS
GLA_HEAD_V = GLA_VALUE_DIM // GLA_HEADS
GLA_GATE_RANK = 16
GLA_GATE_TAU = 16.0
GLA_CHUNK = 64
GLA_IN_DIM = 2 * GLA_KEY_DIM + 2 * GLA_VALUE_DIM + 2 * GLA_GATE_RANK
GLA_SPLITS = (GLA_KEY_DIM, 2 * GLA_KEY_DIM, 2 * GLA_KEY_DIM + GLA_VALUE_DIM,
              2 * GLA_KEY_DIM + 2 * GLA_VALUE_DIM, 2 * GLA_KEY_DIM + 2 * GLA_VALUE_DIM + GLA_GATE_RANK)
FNET_GROUPS = 4
FNET_GROUP_DIM = D_MODEL // FNET_GROUPS
D_FF = -(-8 * D_MODEL // (3 * 256)) * 256
N_ADA = 6
N_NORMS = 4
EPS = 1e-6

kernel_name = "hybrid_gla_fnet_prefix_dit"


def rms_norm(x, gain):
    x32 = x.astype(jnp.float32)
    y = x32 * lax.rsqrt(jnp.mean(x32 * x32, axis=-1, keepdims=True) + EPS)
    return (y * gain.astype(jnp.float32)).astype(x.dtype)


def modulate(x, gain, shift, scale):
    return rms_norm(x, gain) * (1 + scale) + shift


def to_col_major(x, rows):
    b, s, d = x.shape
    return x.reshape(b, rows, GRID_W, d).transpose(0, 2, 1, 3).reshape(b, s, d)


def to_row_major(x, rows):
    b, s, d = x.shape
    return x.reshape(b, GRID_W, rows, d).transpose(0, 2, 1, 3).reshape(b, s, d)


def gla_chunk_scan(q, k, v, g, s0):
    b, h, l, _ = q.shape
    dv = v.shape[-1]
    n = l // GLA_CHUNK
    q, k, v, g = (t.astype(jnp.float32).reshape(b, h, n, GLA_CHUNK, t.shape[-1]) for t in (q, k, v, g))
    cum = jnp.cumsum(g, axis=3)
    last = cum[:, :, :, -1:, :]
    q_dec = q * jnp.exp(cum)
    k_inv = k * jnp.exp(-cum)
    k_end = k * jnp.exp(last - cum)
    mask = jnp.tril(jnp.ones((GLA_CHUNK, GLA_CHUNK), dtype=bool))
    scores = jnp.where(mask, jnp.einsum('bhnid,bhnjd->bhnij', q_dec, k_inv), 0.0)
    o_intra = jnp.einsum('bhnij,bhnje->bhnie', scores, v)
    decay = jnp.exp(last[:, :, :, 0, :])

    def step(state, xs):
        qc, kc, vc, dc = xs
        o = jnp.einsum('bhcd,bhde->bhce', qc, state)
        state = dc[..., None] * state + jnp.einsum('bhcd,bhce->bhde', kc, vc)
        return state, o

    xs = tuple(jnp.moveaxis(t, 2, 0) for t in (q_dec, k_end, v, decay))
    s_final, o_inter = lax.scan(step, s0.astype(jnp.float32), xs)
    o = o_intra + jnp.moveaxis(o_inter, 0, 2)
    return o.reshape(b, h, l, dv), s_final


def gla_project(h, w_in, wg2_f, bg_f, wg2_b, bg_b):
    b, l, _ = h.shape
    q, k, v, r, lf, lb = jnp.split(h @ w_in, GLA_SPLITS, axis=-1)

    def heads(t, d):
        return t.reshape(b, l, GLA_HEADS, d).transpose(0, 2, 1, 3)

    def log_gate(low, w2, b2):
        return heads(jax.nn.log_sigmoid((low @ w2 + b2).astype(jnp.float32)) / GLA_GATE_TAU, GLA_HEAD_K)

    q = heads(q, GLA_HEAD_K) * GLA_HEAD_K ** -0.5
    k = heads(k, GLA_HEAD_K)
    v = heads(v, GLA_HEAD_V)
    return q, k, v, r, log_gate(lf, wg2_f, bg_f), log_gate(lb, wg2_b, bg_b)


def gla_output(o, r, head_gain, w_out):
    b, h, l, dv = o.shape
    o = o * lax.rsqrt(jnp.mean(o * o, axis=-1, keepdims=True) + EPS) * head_gain.astype(jnp.float32)
    o = o.transpose(0, 2, 1, 3).reshape(b, l, h * dv).astype(r.dtype)
    return (o * jax.nn.silu(r)) @ w_out


def gla_mix(h_lat, h_ctx, w_in, wg2_f, bg_f, wg2_b, bg_b, head_gain, w_out, need_ctx):
    qc, kc, vc, rc, gcf, gcb = gla_project(h_ctx, w_in, wg2_f, bg_f, wg2_b, bg_b)
    ql, kl, vl, rl, glf, glb = gla_project(h_lat, w_in, wg2_f, bg_f, wg2_b, bg_b)
    zero = jnp.zeros((h_lat.shape[0], GLA_HEADS, GLA_HEAD_K, GLA_HEAD_V), jnp.float32)

    def flip(t):
        return t[:, :, ::-1]

    oc_f, sc_f = gla_chunk_scan(qc, kc, vc, gcf, zero)
    oc_b, sc_b = gla_chunk_scan(flip(qc), flip(kc), flip(vc), flip(gcb), zero)
    ol_f, _ = gla_chunk_scan(ql, kl, vl, glf, sc_f)
    ol_b, _ = gla_chunk_scan(flip(ql), flip(kl), flip(vl), flip(glb), sc_b)
    y_lat = gla_output(ol_f + flip(ol_b), rl, head_gain, w_out)
    y_ctx = gla_output(oc_f + flip(oc_b), rc, head_gain, w_out) if need_ctx else None
    return y_lat, y_ctx


def fnet_mix(h, w_out):
    b, l, d = h.shape
    hg = h.astype(jnp.float32).reshape(b, l, FNET_GROUPS, FNET_GROUP_DIM)
    f = jnp.fft.fft2(hg, axes=(1, 3), norm="ortho").real
    return f.reshape(b, l, d).astype(h.dtype) @ w_out


def swiglu(h, w_gu, w_down):
    gate, up = jnp.split(h @ w_gu, 2, axis=-1)
    return (jax.nn.silu(gate) * up) @ w_down


def setup_inputs(seed: int = 0) -> dict:
    key = jax.random.key(seed)
    ks = jax.random.split(key, 20)
    d = D_MODEL

    def nrm(k, shape, fan_in):
        return jax.random.normal(k, shape, jnp.float32) * fan_in ** -0.5

    def small(k, shape, s):
        return s * jax.random.normal(k, shape, jnp.float32)

    return {
        "x": jax.random.normal(ks[0], (BATCH, SEQ, d), jnp.float32),
        "c": jax.random.normal(ks[1], (BATCH, d), jnp.float32),
        "ctx": jax.random.normal(ks[2], (BATCH, CTX_LEN, d), jnp.float32),
        "c_ctx": jax.random.normal(ks[3], (d,), jnp.float32),
        "ada_w": nrm(ks[4], (DEPTH, d, N_ADA * d), d),
        "ada_b": small(ks[5], (DEPTH, N_ADA * d), 0.02),
        "norm_gains": 1.0 + small(ks[6], (DEPTH, N_NORMS, d), 0.05),
        "gla_w_in": nrm(ks[7], (N_GLA_LAYERS, d, GLA_IN_DIM), d),
        "gla_wg2_f": nrm(ks[8], (N_GLA_LAYERS, GLA_GATE_RANK, GLA_KEY_DIM), GLA_GATE_RANK),
        "gla_bg_f": small(ks[9], (N_GLA_LAYERS, GLA_KEY_DIM), 0.1),
        "gla_wg2_b": nrm(ks[10], (N_GLA_LAYERS, GLA_GATE_RANK, GLA_KEY_DIM), GLA_GATE_RANK),
        "gla_bg_b": small(ks[11], (N_GLA_LAYERS, GLA_KEY_DIM), 0.1),
        "gla_head_gain": 1.0 + small(ks[12], (N_GLA_LAYERS, GLA_HEAD_V), 0.05),
        "gla_w_out": nrm(ks[13], (N_GLA_LAYERS, GLA_VALUE_DIM, d), GLA_VALUE_DIM),
        "fnet_w_out": nrm(ks[14], (N_FNET_LAYERS, d, d), d),
        "ffn_w_gu": nrm(ks[15], (DEPTH, d, 2 * D_FF), d),
        "ffn_w_down": nrm(ks[16], (DEPTH, D_FF, d), D_FF),
    }


def reference(x, c, ctx, c_ctx, ada_w, ada_b, norm_gains, gla_w_in, gla_wg2_f, gla_bg_f, gla_wg2_b, gla_bg_b,
              gla_head_gain, gla_w_out, fnet_w_out, ffn_w_gu, ffn_w_down):
    rows = x.shape[1] // GRID_W
    s_lat = jax.nn.silu(c)
    s_ctx = jax.nn.silu(c_ctx)[None]
    for i in range(DEPTH):
        need_ctx = i < DEPTH - 1
        j = i // N_MIXERS
        is_gla = i % N_MIXERS == 0
        sh_m, sc_m, gt_m, sh_f, sc_f, gt_f = jnp.split((s_lat @ ada_w[i] + ada_b[i])[:, None, :], N_ADA, axis=-1)
        csh_m, csc_m, cgt_m, csh_f, csc_f, cgt_f = jnp.split((s_ctx @ ada_w[i] + ada_b[i])[:, None, :], N_ADA, axis=-1)

        h_lat = modulate(x, norm_gains[i, 0], sh_m, sc_m)
        if is_gla:
            h_ctx = modulate(ctx, norm_gains[i, 0], csh_m, csc_m)
            col_major = j % 2 == 1
            if col_major:
                h_lat = to_col_major(h_lat, rows)
            y_lat, y_ctx = gla_mix(h_lat, h_ctx, gla_w_in[j], gla_wg2_f[j], gla_bg_f[j], gla_wg2_b[j], gla_bg_b[j],
                                   gla_head_gain[j], gla_w_out[j], need_ctx)
            if col_major:
                y_lat = to_row_major(y_lat, rows)
        else:
            y_lat = fnet_mix(h_lat, fnet_w_out[j])
            y_ctx = fnet_mix(modulate(ctx, norm_gains[i, 0], csh_m, csc_m), fnet_w_out[j]) if need_ctx else None
        x = x + gt_m * rms_norm(y_lat, norm_gains[i, 1])

        f_lat = swiglu(modulate(x, norm_gains[i, 2], sh_f, sc_f), ffn_w_gu[i], ffn_w_down[i])
        x = x + gt_f * rms_norm(f_lat, norm_gains[i, 3])

        if need_ctx:
            ctx = ctx + cgt_m * rms_norm(y_ctx, norm_gains[i, 1])
            f_ctx = swiglu(modulate(ctx, norm_gains[i, 2], csh_f, csc_f), ffn_w_gu[i], ffn_w_down[i])
            ctx = ctx + cgt_f * rms_norm(f_ctx, norm_gains[i, 3])
    return x
```

```python
from functools import partial

import jax
import jax.numpy as jnp
import numpy as np
from jax import lax
from jax.experimental import pallas as pl
from jax.experimental.pallas import tpu as pltpu

D_MODEL = 2048
SEQ = 4096
CTX_LEN = 256
GRID_W = 64
DEPTH = 4
GLA_HEADS = 4
GLA_HEAD_K = 256
GLA_HEAD_V = 512
GLA_KEY_DIM = GLA_HEADS * GLA_HEAD_K
GLA_VALUE_DIM = GLA_HEADS * GLA_HEAD_V
GLA_GATE_RANK = 16
GLA_GATE_TAU = 16.0
GLA_MAIN_DIM = 2 * GLA_KEY_DIM + 2 * GLA_VALUE_DIM
FNET_GROUPS = 4
FNET_GROUP_DIM = D_MODEL // FNET_GROUPS
D_FF = 5632
N_ADA = 6
EPS = 1e-6

LANES = 128
ADA_ROWS = 8
VMEM_LIMIT = 56 * 1024 * 1024

TOK_TILE = 512
COLS_PER_TILE = TOK_TILE // GRID_W
FF_TILE = 512
IN_TILE = 1536
ADA_TILE = 1536
GLA_CHUNK = 128
DFT_BM = 1024
DFT_BK = 512

BF16 = jnp.bfloat16
F32 = jnp.float32


def _cparams(sem):
    return pltpu.CompilerParams(dimension_semantics=sem, vmem_limit_bytes=VMEM_LIMIT)


def _dot(a, b):
    return jnp.dot(a, b, preferred_element_type=F32)


def _rms(x):
    return x * lax.rsqrt(jnp.mean(x * x, axis=-1, keepdims=True) + EPS)


def _modulate(x, gain, shift, scale):
    return (_rms(x) * gain) * (1.0 + scale) + shift


def _ada_kernel(c_ref, w_ref, b_ref, o_ref):
    c = c_ref[...]
    s = (c * jax.nn.sigmoid(c)).astype(BF16)
    o_ref[...] = _dot(s, w_ref[...].astype(BF16)) + b_ref[...]


def _ada_rows(cond, ada_w, ada_b):
    n_out = N_ADA * D_MODEL
    return pl.pallas_call(
        _ada_kernel,
        grid=(DEPTH, n_out // ADA_TILE),
        in_specs=[
            pl.BlockSpec((ADA_ROWS, D_MODEL), lambda i, n: (0, 0)),
            pl.BlockSpec((None, D_MODEL, ADA_TILE), lambda i, n: (i, 0, n)),
            pl.BlockSpec((None, 1, ADA_TILE), lambda i, n: (i, 0, n)),
        ],
        out_specs=pl.BlockSpec((None, ADA_ROWS, ADA_TILE), lambda i, n: (i, 0, n)),
        out_shape=jax.ShapeDtypeStruct((DEPTH, ADA_ROWS, n_out), F32),
        compiler_params=_cparams(("arbitrary", "arbitrary")),
        name="ada_rows",
    )(cond, ada_w, ada_b.reshape(DEPTH, 1, n_out))


def _tile_spec(colmajor):
    if colmajor:
        return pl.BlockSpec((None, GRID_W, COLS_PER_TILE * D_MODEL), lambda b, t, *_: (b, 0, t))
    return pl.BlockSpec((None, TOK_TILE, D_MODEL), lambda b, t, *_: (b, t, 0))


def _load_tile(x_ref, colmajor):
    if not colmajor:
        return x_ref[...]
    return jnp.concatenate(
        [x_ref[:, c * D_MODEL:(c + 1) * D_MODEL] for c in range(COLS_PER_TILE)], axis=0)


def _store_tile(o_ref, val, colmajor):
    if not colmajor:
        o_ref[...] = val
        return
    for c in range(COLS_PER_TILE):
        o_ref[:, c * D_MODEL:(c + 1) * D_MODEL] = val[c * GRID_W:(c + 1) * GRID_W, :]


def _mod_spec(j):
    return pl.BlockSpec((None, 1, D_MODEL), lambda b, t, *_: (b, 0, j))


def _row_spec():
    return pl.BlockSpec((1, D_MODEL), lambda b, t, *_: (0, 0))


def _as_stream(x, colmajor):
    if colmajor:
        bq, s, d = x.shape
        return x.reshape(bq, GRID_W, (s // GRID_W) * d)
    return x


def _gla_in_kernel(x_ref, sh_ref, sc_ref, g_ref, w_ref, wlow_ref, o_ref, low_ref, h_ref, *, colmajor):
    n = pl.program_id(2)

    @pl.when(n == 0)
    def _():
        h = _modulate(_load_tile(x_ref, colmajor), g_ref[...], sh_ref[...], sc_ref[...]).astype(BF16)
        h_ref[...] = h
        low_ref[...] = _dot(h, wlow_ref[...])

    o_ref[...] = _dot(h_ref[...], w_ref[...]).astype(BF16)


def _gla_in(x, mods, gain, w_main, w_low, colmajor):
    bq, s, _ = x.shape
    return pl.pallas_call(
        partial(_gla_in_kernel, colmajor=colmajor),
        grid=(bq, s // TOK_TILE, GLA_MAIN_DIM // IN_TILE),
        in_specs=[
            _tile_spec(colmajor), _mod_spec(0), _mod_spec(1), _row_spec(),
            pl.BlockSpec((D_MODEL, IN_TILE), lambda b, t, n: (0, n)),
            pl.BlockSpec((D_MODEL, LANES), lambda b, t, n: (0, 0)),
        ],
        out_specs=[
            pl.BlockSpec((None, TOK_TILE, IN_TILE), lambda b, t, n: (b, t, n)),
            pl.BlockSpec((None, TOK_TILE, LANES), lambda b, t, n: (b, t, 0)),
        ],
        out_shape=[
            jax.ShapeDtypeStruct((bq, s, GLA_MAIN_DIM), BF16),
            jax.ShapeDtypeStruct((bq, s, LANES), F32),
        ],
        scratch_shapes=[pltpu.VMEM((TOK_TILE, D_MODEL), BF16)],
        compiler_params=_cparams(("arbitrary", "arbitrary", "arbitrary")),
        name="gla_in",
    )(_as_stream(x, colmajor), mods, mods, gain, w_main, w_low)


def _gla_chunk(q, k, v, low, w2, bg, tri, mask, s_prev, reverse):
    c = GLA_CHUNK
    z = _dot(low, w2) + bg
    g = (jnp.minimum(z, 0.0) - jnp.log(1.0 + jnp.exp(-jnp.abs(z)))) * (1.0 / GLA_GATE_TAU)
    g_hi = g.astype(BF16)
    g_lo = (g - g_hi.astype(F32)).astype(BF16)
    cum = _dot(tri, g_hi) + _dot(tri, g_lo)
    if reverse:
        ref, tot = cum[c // 2:c // 2 + 1, :], cum[0:1, :]
    else:
        ref, tot = cum[c // 2 - 1:c // 2, :], cum[c - 1:c, :]
    qf = q.astype(F32) * (GLA_HEAD_K ** -0.5)
    kf = k.astype(F32)
    q_mid = (qf * jnp.exp(cum - ref)).astype(BF16)
    k_mid = (kf * jnp.exp(ref - cum)).astype(BF16)
    q_dec = (qf * jnp.exp(cum)).astype(BF16)
    k_end = (kf * jnp.exp(tot - cum)).astype(BF16)
    scores = lax.dot_general(q_mid, k_mid, (((1,), (1,)), ((), ())), preferred_element_type=F32)
    scores = jnp.where(mask, scores, 0.0).astype(BF16)
    o = _dot(scores, v) + _dot(q_dec, s_prev.astype(BF16))
    ds = lax.dot_general(k_end, v, (((0,), (0,)), ((), ())), preferred_element_type=F32)
    dec = jnp.exp(jnp.broadcast_to(tot, (LANES, GLA_HEAD_K)).T)
    dec = jnp.concatenate([dec] * (GLA_HEAD_V // LANES), axis=1)
    return o, s_prev * dec + ds


def _gla_kernel(qc_ref, kc_ref, vc_ref, rc_ref, lowc_ref, ql_ref, kl_ref, vl_ref, rl_ref, lowl_ref,
                w2_ref, bg_ref, hg_ref, oc_ref, ol_ref, s_ref, acc_ref):
    c = GLA_CHUNK
    row = lax.broadcasted_iota(jnp.int32, (c, c), 0)
    col = lax.broadcasted_iota(jnp.int32, (c, c), 1)
    masks = (col <= row, col >= row)
    tris = tuple(m.astype(BF16) for m in masks)
    hg = hg_ref[...]
    s_ref[...] = jnp.zeros_like(s_ref)

    def phase(q_ref, k_ref, v_ref, r_ref, low_ref, o_ref, n_chunks):
        half = n_chunks // 2

        def one(d, idx):
            rows = pl.ds(pl.multiple_of(idx * c, c), c)
            o, s_new = _gla_chunk(q_ref[rows, :], k_ref[rows, :], v_ref[rows, :],
                                  low_ref[rows, :].astype(BF16), w2_ref[d], bg_ref[d],
                                  tris[d], masks[d], s_ref[d], reverse=(d == 1))
            s_ref[d] = s_new
            return rows, o

        def first_visit(n, carry):
            for d, idx in ((0, n), (1, n_chunks - 1 - n)):
                rows, o = one(d, idx)
                acc_ref[rows, :] = o
            return carry

        def second_visit(n, carry):
            for d, idx in ((0, n), (1, n_chunks - 1 - n)):
                rows, o = one(d, idx)
                o = _rms(o + acc_ref[rows, :]) * hg
                r = r_ref[rows, :].astype(F32)
                o_ref[rows, :] = (o * (r * jax.nn.sigmoid(r))).astype(BF16)
            return carry

        lax.fori_loop(0, half, first_visit, 0)
        lax.fori_loop(half, n_chunks, second_visit, 0)

    phase(qc_ref, kc_ref, vc_ref, rc_ref, lowc_ref, oc_ref, CTX_LEN // c)
    phase(ql_ref, kl_ref, vl_ref, rl_ref, lowl_ref, ol_ref, SEQ // c)


def _gla_scan(qkvr_c, low_c, qkvr_l, low_l, w2, bg, head_gain):
    b = qkvr_l.shape[0]
    nk = GLA_KEY_DIM // GLA_HEAD_K

    def stream_specs(length):
        return [
            pl.BlockSpec((None, length, GLA_HEAD_K), lambda i, h: (i, 0, h)),
            pl.BlockSpec((None, length, GLA_HEAD_K), lambda i, h: (i, 0, nk + h)),
            pl.BlockSpec((None, length, GLA_HEAD_V), lambda i, h: (i, 0, nk + h)),
            pl.BlockSpec((None, length, GLA_HEAD_V), lambda i, h: (i, 0, 2 * nk + h)),
            pl.BlockSpec((None, length, LANES), lambda i, h: (i, 0, 0)),
        ]

    def out_spec(length):
        return pl.BlockSpec((None, length, GLA_HEAD_V), lambda i, h: (i, 0, h))

    return pl.pallas_call(
        _gla_kernel,
        grid=(b, GLA_HEADS),
        in_specs=stream_specs(CTX_LEN) + stream_specs(SEQ) + [
            pl.BlockSpec((2, LANES, GLA_HEAD_K), lambda i, h: (0, 0, h)),
            pl.BlockSpec((2, 1, GLA_HEAD_K), lambda i, h: (0, 0, h)),
            pl.BlockSpec((1, GLA_HEAD_V), lambda i, h: (0, 0)),
        ],
        out_specs=[out_spec(CTX_LEN), out_spec(SEQ)],
        out_shape=[
            jax.ShapeDtypeStruct((b, CTX_LEN, GLA_VALUE_DIM), BF16),
            jax.ShapeDtypeStruct((b, SEQ, GLA_VALUE_DIM), BF16),
        ],
        scratch_shapes=[
            pltpu.VMEM((2, GLA_HEAD_K, GLA_HEAD_V), F32),
            pltpu.VMEM((SEQ, GLA_HEAD_V), F32),
        ],
        compiler_params=_cparams(("arbitrary", "arbitrary")),
        name="gla_scan",
    )(qkvr_c, qkvr_c, qkvr_c, qkvr_c, low_c, qkvr_l, qkvr_l, qkvr_l, qkvr_l, low_l, w2, bg, head_gain)


def _fnet_ch_kernel(x_ref, sh_ref, sc_ref, g_ref, w_ref, y1_ref, y2_ref):
    h = _modulate(x_ref[...], g_ref[...], sh_ref[...], sc_ref[...]).astype(BF16)
    w = w_ref[...]
    gd = FNET_GROUP_DIM
    for g in range(FNET_GROUPS):
        r = _dot(h[:, g * gd:(g + 1) * gd], w).astype(BF16)
        y1_ref[:, g * gd:(g + 1) * gd] = r[:, :gd]
        y2_ref[:, g * gd:(g + 1) * gd] = r[:, gd:]


def _fnet_channels(x, mods, gain, w_ch):
    bq, s, _ = x.shape
    out = jax.ShapeDtypeStruct((bq, s, D_MODEL), BF16)
    return pl.pallas_call(
        _fnet_ch_kernel,
        grid=(bq, s // TOK_TILE),
        in_specs=[
            _tile_spec(False), _mod_spec(0), _mod_spec(1), _row_spec(),
            pl.BlockSpec((FNET_GROUP_DIM, 2 * FNET_GROUP_DIM), lambda b, t: (0, 0)),
        ],
        out_specs=[_tile_spec(False), _tile_spec(False)],
        out_shape=[out, out],
        compiler_params=_cparams(("arbitrary", "arbitrary")),
        name="fnet_channels",
    )(x, mods, mods, gain, w_ch)


def _fnet_tok_kernel(gc_ref, gs_ref, y1_ref, y2_ref, o_ref, acc_ref):
    k = pl.program_id(2)
    part = _dot(gc_ref[...], y1_ref[...]) + _dot(gs_ref[...], y2_ref[...])

    @pl.when(k == 0)
    def _():
        acc_ref[...] = part

    @pl.when(k > 0)
    def _():
        acc_ref[...] += part

    @pl.when(k == pl.num_programs(2) - 1)
    def _():
        o_ref[...] = acc_ref[...].astype(BF16)


def _fnet_tokens(gc, gs, y1, y2):
    b, length, _ = y1.shape
    bm, bk = min(DFT_BM, length), min(DFT_BK, length)
    y_spec = pl.BlockSpec((None, bk, D_MODEL), lambda i, m, k: (i, k, 0))
    g_spec = pl.BlockSpec((bm, bk), lambda i, m, k: (m, k))
    return pl.pallas_call(
        _fnet_tok_kernel,
        grid=(b, length // bm, length // bk),
        in_specs=[g_spec, g_spec, y_spec, y_spec],
        out_specs=pl.BlockSpec((None, bm, D_MODEL), lambda i, m, k: (i, m, 0)),
        out_shape=jax.ShapeDtypeStruct((b, length, D_MODEL), BF16),
        scratch_shapes=[pltpu.VMEM((bm, D_MODEL), F32)],
        compiler_params=_cparams(("arbitrary", "arbitrary", "arbitrary")),
        name="fnet_tokens",
    )(gc, gs, y1, y2)


def _dft_tables(n, split):
    lp = jnp.arange(n, dtype=jnp.int32)[:, None]
    hi = jnp.arange(n // split, dtype=jnp.int32)[None, :]
    lo = jnp.arange(split, dtype=jnp.int32)[None, :]
    ang_hi = ((lp * hi) % (n // split)).astype(F32) * (2.0 * np.pi * split / n)
    ang_lo = ((lp * lo) % n).astype(F32) * (2.0 * np.pi / n)
    ch, sh_, cl, sl = jnp.cos(ang_hi), jnp.sin(ang_hi), jnp.cos(ang_lo), jnp.sin(ang_lo)
    cos = ch[:, :, None] * cl[:, None, :] - sh_[:, :, None] * sl[:, None, :]
    sin = sh_[:, :, None] * cl[:, None, :] + ch[:, :, None] * sl[:, None, :]
    return cos.reshape(n, n), sin.reshape(n, n)


def _out_ffn_kernel(x_ref, a_ref, gtm_ref, shf_ref, scf_ref, gtf_ref, g1_ref, g2_ref, g3_ref,
                    wout_ref, wg_ref, wu_ref, wd_ref, o_ref, h_ref, acc_ref, *, colmajor):
    k = pl.program_id(2)

    @pl.when(k == 0)
    def _():
        y = _dot(a_ref[...], wout_ref[...])
        x1 = _load_tile(x_ref, colmajor) + gtm_ref[...] * (_rms(y) * g1_ref[...])
        _store_tile(o_ref, x1, colmajor)
        h_ref[...] = _modulate(x1, g2_ref[...], shf_ref[...], scf_ref[...]).astype(BF16)

    h = h_ref[...]
    gate = _dot(h, wg_ref[...])
    up = _dot(h, wu_ref[...])
    act = (gate * jax.nn.sigmoid(gate) * up).astype(BF16)
    part = _dot(act, wd_ref[...])

    @pl.when(k == 0)
    def _():
        acc_ref[...] = part

    @pl.when(k > 0)
    def _():
        acc_ref[...] += part

    @pl.when(k == pl.num_programs(2) - 1)
    def _():
        out = _load_tile(o_ref, colmajor) + gtf_ref[...] * (_rms(acc_ref[...]) * g3_ref[...])
        _store_tile(o_ref, out, colmajor)


def _out_ffn(x, a, mods, g1, g2, g3, w_out, w_gu, w_down, colmajor):
    bq, s, _ = x.shape
    nf = D_FF // FF_TILE
    res = pl.pallas_call(
        partial(_out_ffn_kernel, colmajor=colmajor),
        grid=(bq, s // TOK_TILE, nf),
        in_specs=[
            _tile_spec(colmajor), _tile_spec(False),
            _mod_spec(2), _mod_spec(3), _mod_spec(4), _mod_spec(5),
            _row_spec(), _row_spec(), _row_spec(),
            pl.BlockSpec((D_MODEL, D_MODEL), lambda b, t, k: (0, 0), pipeline_mode=pl.Buffered(1)),
            pl.BlockSpec((D_MODEL, FF_TILE), lambda b, t, k: (0, k)),
            pl.BlockSpec((D_MODEL, FF_TILE), lambda b, t, k: (0, nf + k)),
            pl.BlockSpec((FF_TILE, D_MODEL), lambda b, t, k: (k, 0)),
        ],
        out_specs=_tile_spec(colmajor),
        out_shape=jax.ShapeDtypeStruct(_as_stream(x, colmajor).shape, F32),
        scratch_shapes=[
            pltpu.VMEM((TOK_TILE, D_MODEL), BF16),
            pltpu.VMEM((TOK_TILE, D_MODEL), F32),
        ],
        compiler_params=_cparams(("arbitrary", "arbitrary", "arbitrary")),
        name="out_ffn",
    )(_as_stream(x, colmajor), a, mods, mods, mods, mods, g1, g2, g3, w_out, w_gu, w_gu, w_down)
    return res.reshape(x.shape)


def kernel(x, c, ctx, c_ctx, ada_w, ada_b, norm_gains, gla_w_in, gla_wg2_f, gla_bg_f, gla_wg2_b, gla_bg_b,
           gla_head_gain, gla_w_out, fnet_w_out, ffn_w_gu, ffn_w_down):
    batch = x.shape[0]
    cond = jnp.zeros((ADA_ROWS, D_MODEL), F32).at[:batch].set(c).at[batch].set(c_ctx)
    mods = _ada_rows(cond, ada_w, ada_b)
    ctx_s = ctx.reshape(1, batch * CTX_LEN, D_MODEL)

    n_ch = FNET_GROUP_DIM
    cos_c, sin_c = _dft_tables(n_ch, 16)
    w_ch = (jnp.concatenate([cos_c, sin_c], axis=1) * (n_ch ** -0.5)).astype(BF16)
    tok_tables = {}
    for length, split in ((SEQ, 64), (CTX_LEN, 16)):
        cos_t, sin_t = _dft_tables(length, split)
        tok_tables[length] = ((cos_t * (length ** -0.5)).astype(BF16), (sin_t * -(length ** -0.5)).astype(BF16))

    for i in range(DEPTH):
        need_ctx = i < DEPTH - 1
        j = i // 2
        mod_lat = mods[i, :batch].reshape(batch, 1, N_ADA * D_MODEL)
        mod_ctx = mods[i, batch:batch + 1].reshape(1, 1, N_ADA * D_MODEL)
        gains = [norm_gains[i, n].reshape(1, D_MODEL) for n in range(4)]
        colmajor = False
        if i % 2 == 0:
            colmajor = j % 2 == 1
            w_in = gla_w_in[j]
            w_main = w_in[:, :GLA_MAIN_DIM].astype(BF16)
            w_low = jnp.pad(w_in[:, GLA_MAIN_DIM:], ((0, 0), (0, LANES - 2 * GLA_GATE_RANK))).astype(BF16)
            w2 = jnp.zeros((2, LANES, GLA_KEY_DIM), F32)
            w2 = w2.at[0, :GLA_GATE_RANK].set(gla_wg2_f[j]).at[1, GLA_GATE_RANK:2 * GLA_GATE_RANK].set(gla_wg2_b[j])
            bg = jnp.stack([gla_bg_f[j], gla_bg_b[j]]).reshape(2, 1, GLA_KEY_DIM)
            qkvr_l, low_l = _gla_in(x, mod_lat, gains[0], w_main, w_low, colmajor)
            qkvr_c, low_c = _gla_in(ctx_s, mod_ctx, gains[0], w_main, w_low, False)
            a_ctx, a_lat = _gla_scan(
                qkvr_c.reshape(batch, CTX_LEN, GLA_MAIN_DIM), low_c.reshape(batch, CTX_LEN, LANES),
                qkvr_l, low_l, w2.astype(BF16), bg, gla_head_gain[j].reshape(1, GLA_HEAD_V))
            w_mix = gla_w_out[j].astype(BF16)
        else:
            y1, y2 = _fnet_channels(x, mod_lat, gains[0], w_ch)
            a_lat = _fnet_tokens(*tok_tables[SEQ], y1, y2)
            if need_ctx:
                y1, y2 = _fnet_channels(ctx_s, mod_ctx, gains[0], w_ch)
                a_ctx = _fnet_tokens(*tok_tables[CTX_LEN], y1.reshape(batch, CTX_LEN, D_MODEL),
                                     y2.reshape(batch, CTX_LEN, D_MODEL))
            w_mix = fnet_w_out[j].astype(BF16)
        w_gu = ffn_w_gu[i].astype(BF16)
        w_down = ffn_w_down[i].astype(BF16)
        x = _out_ffn(x, a_lat, mod_lat, gains[1], gains[2], gains[3], w_mix, w_gu, w_down, colmajor)
        if need_ctx:
            ctx_s = _out_ffn(ctx_s, a_ctx.reshape(1, batch * CTX_LEN, D_MODEL), mod_ctx,
                             gains[1], gains[2], gains[3], w_mix, w_gu, w_down, False)
    return x
```

```python
from functools import partial

import jax
import jax.numpy as jnp
import numpy as np
from jax import lax
from jax.experimental import pallas as pl
from jax.experimental.pallas import tpu as pltpu

D_MODEL = 2048
SEQ = 4096
CTX_LEN = 256
GRID_W = 64
DEPTH = 4
GLA_HEADS = 4
GLA_HEAD_K = 256
GLA_HEAD_V = 512
GLA_KEY_DIM = GLA_HEADS * GLA_HEAD_K
GLA_VALUE_DIM = GLA_HEADS * GLA_HEAD_V
GLA_GATE_RANK = 16
GLA_GATE_TAU = 16.0
GLA_MAIN_DIM = 2 * GLA_KEY_DIM + 2 * GLA_VALUE_DIM
FNET_GROUPS = 4
FNET_GROUP_DIM = D_MODEL // FNET_GROUPS
D_FF = 5632
N_ADA = 6
EPS = 1e-6

LANES = 128
ADA_ROWS = 8
VMEM_LIMIT = 56 * 1024 * 1024

TOK_TILE = 512
COLS_PER_TILE = TOK_TILE // GRID_W
FF_TILE = 512
IN_TILE = 1536
ADA_TILE = 1536
GLA_CHUNK = 128
DFT_BM = 1024
DFT_BK = 512

BF16 = jnp.bfloat16
F32 = jnp.float32


def _cparams(sem):
    return pltpu.CompilerParams(dimension_semantics=sem, vmem_limit_bytes=VMEM_LIMIT)


def _dot(a, b):
    return jnp.dot(a, b, preferred_element_type=F32)


def _rms(x):
    return x * lax.rsqrt(jnp.mean(x * x, axis=-1, keepdims=True) + EPS)


def _modulate(x, gain, shift, scale):
    return (_rms(x) * gain) * (1.0 + scale) + shift


def _ada_kernel(c_ref, w_ref, b_ref, o_ref):
    c = c_ref[...]
    s = (c * jax.nn.sigmoid(c)).astype(BF16)
    o_ref[...] = _dot(s, w_ref[...].astype(BF16)) + b_ref[...]


def _ada_rows(cond, ada_w, ada_b):
    n_out = N_ADA * D_MODEL
    return pl.pallas_call(
        _ada_kernel,
        grid=(DEPTH, n_out // ADA_TILE),
        in_specs=[
            pl.BlockSpec((ADA_ROWS, D_MODEL), lambda i, n: (0, 0)),
            pl.BlockSpec((None, D_MODEL, ADA_TILE), lambda i, n: (i, 0, n)),
            pl.BlockSpec((None, 1, ADA_TILE), lambda i, n: (i, 0, n)),
        ],
        out_specs=pl.BlockSpec((None, ADA_ROWS, ADA_TILE), lambda i, n: (i, 0, n)),
        out_shape=jax.ShapeDtypeStruct((DEPTH, ADA_ROWS, n_out), F32),
        compiler_params=_cparams(("arbitrary", "arbitrary")),
        name="ada_rows",
    )(cond, ada_w, ada_b.reshape(DEPTH, 1, n_out))


def _tile_spec(colmajor):
    if colmajor:
        return pl.BlockSpec((None, GRID_W, COLS_PER_TILE, D_MODEL), lambda b, t, *_: (b, 0, t, 0))
    return pl.BlockSpec((None, TOK_TILE, D_MODEL), lambda b, t, *_: (b, t, 0))


def _load_tile(x_ref, colmajor):
    if not colmajor:
        return x_ref[...]
    return jnp.concatenate([x_ref[:, c, :] for c in range(COLS_PER_TILE)], axis=0)


def _store_tile(o_ref, val, colmajor):
    if not colmajor:
        o_ref[...] = val
        return
    for c in range(COLS_PER_TILE):
        o_ref[:, c, :] = val[c * GRID_W:(c + 1) * GRID_W, :]


def _mod_spec(j):
    return pl.BlockSpec((None, 1, D_MODEL), lambda b, t, *_: (b, 0, j))


def _row_spec():
    return pl.BlockSpec((1, D_MODEL), lambda b, t, *_: (0, 0))


def _as_stream(x, colmajor):
    if colmajor:
        bq, s, d = x.shape
        return x.reshape(bq, GRID_W, s // GRID_W, d)
    return x


def _gla_in_kernel(x_ref, sh_ref, sc_ref, g_ref, w_ref, wlow_ref, o_ref, low_ref, h_ref, *, colmajor):
    n = pl.program_id(2)

    @pl.when(n == 0)
    def _():
        h = _modulate(_load_tile(x_ref, colmajor), g_ref[...], sh_ref[...], sc_ref[...]).astype(BF16)
        h_ref[...] = h
        low_ref[...] = _dot(h, wlow_ref[...])

    o_ref[...] = _dot(h_ref[...], w_ref[...]).astype(BF16)


def _gla_in(x, mods, gain, w_in, layer, w_low, colmajor):
    bq, s, _ = x.shape
    return pl.pallas_call(
        partial(_gla_in_kernel, colmajor=colmajor),
        grid=(bq, s // TOK_TILE, GLA_MAIN_DIM // IN_TILE),
        in_specs=[
            _tile_spec(colmajor), _mod_spec(0), _mod_spec(1), _row_spec(),
            pl.BlockSpec((None, D_MODEL, IN_TILE), lambda b, t, n: (layer, 0, n)),
            pl.BlockSpec((D_MODEL, LANES), lambda b, t, n: (0, 0)),
        ],
        out_specs=[
            pl.BlockSpec((None, TOK_TILE, IN_TILE), lambda b, t, n: (b, t, n)),
            pl.BlockSpec((None, TOK_TILE, LANES), lambda b, t, n: (b, t, 0)),
        ],
        out_shape=[
            jax.ShapeDtypeStruct((bq, s, GLA_MAIN_DIM), BF16),
            jax.ShapeDtypeStruct((bq, s, LANES), F32),
        ],
        scratch_shapes=[pltpu.VMEM((TOK_TILE, D_MODEL), BF16)],
        compiler_params=_cparams(("arbitrary", "arbitrary", "arbitrary")),
        name="gla_in",
    )(_as_stream(x, colmajor), mods, mods, gain, w_in, w_low)


def _gla_chunk(q, k, v, low, w2, bg, tri, mask, s_prev, reverse):
    c = GLA_CHUNK
    z = _dot(low, w2) + bg
    g = (jnp.minimum(z, 0.0) - jnp.log(1.0 + jnp.exp(-jnp.abs(z)))) * (1.0 / GLA_GATE_TAU)
    g_hi = g.astype(BF16)
    g_lo = (g - g_hi.astype(F32)).astype(BF16)
    cum = _dot(tri, g_hi) + _dot(tri, g_lo)
    if reverse:
        ref, tot = cum[c // 2:c // 2 + 1, :], cum[0:1, :]
    else:
        ref, tot = cum[c // 2 - 1:c // 2, :], cum[c - 1:c, :]
    qf = q.astype(F32) * (GLA_HEAD_K ** -0.5)
    kf = k.astype(F32)
    q_mid = (qf * jnp.exp(cum - ref)).astype(BF16)
    k_mid = (kf * jnp.exp(ref - cum)).astype(BF16)
    q_dec = (qf * jnp.exp(cum)).astype(BF16)
    k_end = (kf * jnp.exp(tot - cum)).astype(BF16)
    scores = lax.dot_general(q_mid, k_mid, (((1,), (1,)), ((), ())), preferred_element_type=F32)
    scores = jnp.where(mask, scores, 0.0).astype(BF16)
    o = _dot(scores, v) + _dot(q_dec, s_prev.astype(BF16))
    ds = lax.dot_general(k_end, v, (((0,), (0,)), ((), ())), preferred_element_type=F32)
    dec = jnp.exp(jnp.broadcast_to(tot, (LANES, GLA_HEAD_K)).T)
    dec = jnp.concatenate([dec] * (GLA_HEAD_V // LANES), axis=1)
    return o, s_prev * dec + ds


def _gla_kernel(qc_ref, kc_ref, vc_ref, rc_ref, lowc_ref, ql_ref, kl_ref, vl_ref, rl_ref, lowl_ref,
                w2_ref, bg_ref, hg_ref, oc_ref, ol_ref, s_ref, acc_ref):
    c = GLA_CHUNK
    row = lax.broadcasted_iota(jnp.int32, (c, c), 0)
    col = lax.broadcasted_iota(jnp.int32, (c, c), 1)
    masks = (col <= row, col >= row)
    tris = tuple(m.astype(BF16) for m in masks)
    hg = hg_ref[...]
    s_ref[...] = jnp.zeros_like(s_ref)

    def phase(q_ref, k_ref, v_ref, r_ref, low_ref, o_ref, n_chunks):
        half = n_chunks // 2

        def one(d, idx):
            rows = pl.ds(pl.multiple_of(idx * c, c), c)
            o, s_new = _gla_chunk(q_ref[rows, :], k_ref[rows, :], v_ref[rows, :],
                                  low_ref[rows, :].astype(BF16), w2_ref[d], bg_ref[d],
                                  tris[d], masks[d], s_ref[d], reverse=(d == 1))
            s_ref[d] = s_new
            return rows, o

        def first_visit(n, carry):
            for d, idx in ((0, n), (1, n_chunks - 1 - n)):
                rows, o = one(d, idx)
                acc_ref[rows, :] = o
            return carry

        def second_visit(n, carry):
            for d, idx in ((0, n), (1, n_chunks - 1 - n)):
                rows, o = one(d, idx)
                o = _rms(o + acc_ref[rows, :]) * hg
                r = r_ref[rows, :].astype(F32)
                o_ref[rows, :] = (o * (r * jax.nn.sigmoid(r))).astype(BF16)
            return carry

        lax.fori_loop(0, half, first_visit, 0)
        lax.fori_loop(half, n_chunks, second_visit, 0)

    phase(qc_ref, kc_ref, vc_ref, rc_ref, lowc_ref, oc_ref, CTX_LEN // c)
    phase(ql_ref, kl_ref, vl_ref, rl_ref, lowl_ref, ol_ref, SEQ // c)


def _gla_scan(qkvr_c, low_c, qkvr_l, low_l, w2, bg, head_gain):
    b = qkvr_l.shape[0]
    nk = GLA_KEY_DIM // GLA_HEAD_K

    def stream_specs(length):
        return [
            pl.BlockSpec((None, length, GLA_HEAD_K), lambda i, h: (i, 0, h)),
            pl.BlockSpec((None, length, GLA_HEAD_K), lambda i, h: (i, 0, nk + h)),
            pl.BlockSpec((None, length, GLA_HEAD_V), lambda i, h: (i, 0, nk + h)),
            pl.BlockSpec((None, length, GLA_HEAD_V), lambda i, h: (i, 0, 2 * nk + h)),
            pl.BlockSpec((None, length, LANES), lambda i, h: (i, 0, 0)),
        ]

    def out_spec(length):
        return pl.BlockSpec((None, length, GLA_HEAD_V), lambda i, h: (i, 0, h))

    return pl.pallas_call(
        _gla_kernel,
        grid=(b, GLA_HEADS),
        in_specs=stream_specs(CTX_LEN) + stream_specs(SEQ) + [
            pl.BlockSpec((2, LANES, GLA_HEAD_K), lambda i, h: (0, 0, h)),
            pl.BlockSpec((2, 1, GLA_HEAD_K), lambda i, h: (0, 0, h)),
            pl.BlockSpec((1, GLA_HEAD_V), lambda i, h: (0, 0)),
        ],
        out_specs=[out_spec(CTX_LEN), out_spec(SEQ)],
        out_shape=[
            jax.ShapeDtypeStruct((b, CTX_LEN, GLA_VALUE_DIM), BF16),
            jax.ShapeDtypeStruct((b, SEQ, GLA_VALUE_DIM), BF16),
        ],
        scratch_shapes=[
            pltpu.VMEM((2, GLA_HEAD_K, GLA_HEAD_V), F32),
            pltpu.VMEM((SEQ, GLA_HEAD_V), F32),
        ],
        compiler_params=_cparams(("arbitrary", "arbitrary")),
        name="gla_scan",
    )(qkvr_c, qkvr_c, qkvr_c, qkvr_c, low_c, qkvr_l, qkvr_l, qkvr_l, qkvr_l, low_l, w2, bg, head_gain)


def _fnet_ch_kernel(x_ref, sh_ref, sc_ref, g_ref, w_ref, y1_ref, y2_ref):
    h = _modulate(x_ref[...], g_ref[...], sh_ref[...], sc_ref[...]).astype(BF16)
    w = w_ref[...]
    gd = FNET_GROUP_DIM
    for g in range(FNET_GROUPS):
        r = _dot(h[:, g * gd:(g + 1) * gd], w).astype(BF16)
        y1_ref[:, g * gd:(g + 1) * gd] = r[:, :gd]
        y2_ref[:, g * gd:(g + 1) * gd] = r[:, gd:]


def _fnet_channels(x, mods, gain, w_ch):
    bq, s, _ = x.shape
    out = jax.ShapeDtypeStruct((bq, s, D_MODEL), BF16)
    return pl.pallas_call(
        _fnet_ch_kernel,
        grid=(bq, s // TOK_TILE),
        in_specs=[
            _tile_spec(False), _mod_spec(0), _mod_spec(1), _row_spec(),
            pl.BlockSpec((FNET_GROUP_DIM, 2 * FNET_GROUP_DIM), lambda b, t: (0, 0)),
        ],
        out_specs=[_tile_spec(False), _tile_spec(False)],
        out_shape=[out, out],
        compiler_params=_cparams(("arbitrary", "arbitrary")),
        name="fnet_channels",
    )(x, mods, mods, gain, w_ch)


def _fnet_tok_kernel(gc_ref, gs_ref, y1_ref, y2_ref, o_ref, acc_ref):
    k = pl.program_id(2)

    @pl.when(k == 0)
    def _():
        acc_ref[...] = jnp.zeros_like(acc_ref)

    acc_ref[...] += _dot(gc_ref[...], y1_ref[...]) + _dot(gs_ref[...], y2_ref[...])

    @pl.when(k == pl.num_programs(2) - 1)
    def _():
        o_ref[...] = acc_ref[...].astype(BF16)


def _fnet_tokens(gc, gs, y1, y2):
    b, length, _ = y1.shape
    bm, bk = min(DFT_BM, length), min(DFT_BK, length)
    y_spec = pl.BlockSpec((None, bk, D_MODEL), lambda i, m, k: (i, k, 0))
    g_spec = pl.BlockSpec((bm, bk), lambda i, m, k: (m, k))
    return pl.pallas_call(
        _fnet_tok_kernel,
        grid=(b, length // bm, length // bk),
        in_specs=[g_spec, g_spec, y_spec, y_spec],
        out_specs=pl.BlockSpec((None, bm, D_MODEL), lambda i, m, k: (i, m, 0)),
        out_shape=jax.ShapeDtypeStruct((b, length, D_MODEL), BF16),
        scratch_shapes=[pltpu.VMEM((bm, D_MODEL), F32)],
        compiler_params=_cparams(("arbitrary", "arbitrary", "arbitrary")),
        name="fnet_tokens",
    )(gc, gs, y1, y2)


def _dft_tables(n, split):
    lp = jnp.arange(n, dtype=jnp.int32)[:, None]
    hi = jnp.arange(n // split, dtype=jnp.int32)[None, :]
    lo = jnp.arange(split, dtype=jnp.int32)[None, :]
    ang_hi = ((lp * hi) % (n // split)).astype(F32) * (2.0 * np.pi * split / n)
    ang_lo = ((lp * lo) % n).astype(F32) * (2.0 * np.pi / n)
    ch, sh_, cl, sl = jnp.cos(ang_hi), jnp.sin(ang_hi), jnp.cos(ang_lo), jnp.sin(ang_lo)
    cos = ch[:, :, None] * cl[:, None, :] - sh_[:, :, None] * sl[:, None, :]
    sin = sh_[:, :, None] * cl[:, None, :] + ch[:, :, None] * sl[:, None, :]
    return cos.reshape(n, n), sin.reshape(n, n)


def _out_ffn_kernel(x_ref, a_ref, gtm_ref, shf_ref, scf_ref, gtf_ref, g1_ref, g2_ref, g3_ref,
                    wout_ref, wg_ref, wu_ref, wd_ref, o_ref, h_ref, acc_ref, *, colmajor):
    k = pl.program_id(2)

    x1_ref = o_ref.reshape(TOK_TILE, D_MODEL) if colmajor else o_ref

    @pl.when(k == 0)
    def _():
        y = _dot(a_ref[...], wout_ref[...])
        x1 = _load_tile(x_ref, colmajor) + gtm_ref[...] * (_rms(y) * g1_ref[...])
        x1_ref[...] = x1
        h_ref[...] = _modulate(x1, g2_ref[...], shf_ref[...], scf_ref[...]).astype(BF16)
        acc_ref[...] = jnp.zeros_like(acc_ref)

    h = h_ref[...]
    gate = _dot(h, wg_ref[...])
    up = _dot(h, wu_ref[...])
    act = (gate * jax.nn.sigmoid(gate) * up).astype(BF16)
    acc_ref[...] += _dot(act, wd_ref[...])

    @pl.when(k == pl.num_programs(2) - 1)
    def _():
        out = x1_ref[...] + gtf_ref[...] * (_rms(acc_ref[...]) * g3_ref[...])
        _store_tile(o_ref, out, colmajor)


def _out_ffn(x, a, mods, g1, g2, g3, w_out, mix_layer, w_gu, w_down, layer, colmajor):
    bq, s, _ = x.shape
    nf = D_FF // FF_TILE
    res = pl.pallas_call(
        partial(_out_ffn_kernel, colmajor=colmajor),
        grid=(bq, s // TOK_TILE, nf),
        in_specs=[
            _tile_spec(colmajor), _tile_spec(False),
            _mod_spec(2), _mod_spec(3), _mod_spec(4), _mod_spec(5),
            _row_spec(), _row_spec(), _row_spec(),
            pl.BlockSpec((None, D_MODEL, D_MODEL), lambda b, t, k: (mix_layer, 0, 0),
                         pipeline_mode=pl.Buffered(1)),
            pl.BlockSpec((None, D_MODEL, FF_TILE), lambda b, t, k: (layer, 0, k)),
            pl.BlockSpec((None, D_MODEL, FF_TILE), lambda b, t, k: (layer, 0, nf + k)),
            pl.BlockSpec((None, FF_TILE, D_MODEL), lambda b, t, k: (layer, k, 0)),
        ],
        out_specs=_tile_spec(colmajor),
        out_shape=jax.ShapeDtypeStruct(_as_stream(x, colmajor).shape, F32),
        scratch_shapes=[
            pltpu.VMEM((TOK_TILE, D_MODEL), BF16),
            pltpu.VMEM((TOK_TILE, D_MODEL), F32),
        ],
        compiler_params=_cparams(("arbitrary", "arbitrary", "arbitrary")),
        name="out_ffn",
    )(_as_stream(x, colmajor), a, mods, mods, mods, mods, g1, g2, g3, w_out, w_gu, w_gu, w_down)
    return res.reshape(x.shape)


def kernel(x, c, ctx, c_ctx, ada_w, ada_b, norm_gains, gla_w_in, gla_wg2_f, gla_bg_f, gla_wg2_b, gla_bg_b,
           gla_head_gain, gla_w_out, fnet_w_out, ffn_w_gu, ffn_w_down):
    batch = x.shape[0]
    cond = jnp.zeros((ADA_ROWS, D_MODEL), F32).at[:batch].set(c).at[batch].set(c_ctx)
    mods = _ada_rows(cond, ada_w, ada_b)
    ctx_s = ctx.reshape(1, batch * CTX_LEN, D_MODEL)

    n_ch = FNET_GROUP_DIM
    cos_c, sin_c = _dft_tables(n_ch, 16)
    w_ch = (jnp.concatenate([cos_c, sin_c], axis=1) * (n_ch ** -0.5)).astype(BF16)
    tok_tables = {}
    for length, split in ((SEQ, 64), (CTX_LEN, 16)):
        cos_t, sin_t = _dft_tables(length, split)
        tok_tables[length] = ((cos_t * (length ** -0.5)).astype(BF16), (sin_t * -(length ** -0.5)).astype(BF16))

    gla_w_in_bf = gla_w_in.astype(BF16)
    gla_w_out_bf = gla_w_out.astype(BF16)
    fnet_w_out_bf = fnet_w_out.astype(BF16)
    w_gu_bf = ffn_w_gu.astype(BF16)
    w_down_bf = ffn_w_down.astype(BF16)

    for i in range(DEPTH):
        need_ctx = i < DEPTH - 1
        j = i // 2
        mod_lat = mods[i, :batch].reshape(batch, 1, N_ADA * D_MODEL)
        mod_ctx = mods[i, batch:batch + 1].reshape(1, 1, N_ADA * D_MODEL)
        gains = [norm_gains[i, n].reshape(1, D_MODEL) for n in range(4)]
        colmajor = False
        if i % 2 == 0:
            colmajor = j % 2 == 1
            w_low = jnp.pad(gla_w_in[j, :, GLA_MAIN_DIM:],
                            ((0, 0), (0, LANES - 2 * GLA_GATE_RANK))).astype(BF16)
            w2 = jnp.zeros((2, LANES, GLA_KEY_DIM), F32)
            w2 = w2.at[0, :GLA_GATE_RANK].set(gla_wg2_f[j]).at[1, GLA_GATE_RANK:2 * GLA_GATE_RANK].set(gla_wg2_b[j])
            bg = jnp.stack([gla_bg_f[j], gla_bg_b[j]]).reshape(2, 1, GLA_KEY_DIM)
            qkvr_l, low_l = _gla_in(x, mod_lat, gains[0], gla_w_in_bf, j, w_low, colmajor)
            qkvr_c, low_c = _gla_in(ctx_s, mod_ctx, gains[0], gla_w_in_bf, j, w_low, False)
            a_ctx, a_lat = _gla_scan(
                qkvr_c.reshape(batch, CTX_LEN, GLA_MAIN_DIM), low_c.reshape(batch, CTX_LEN, LANES),
                qkvr_l, low_l, w2.astype(BF16), bg, gla_head_gain[j].reshape(1, GLA_HEAD_V))
            w_mix = gla_w_out_bf
        else:
            y1, y2 = _fnet_channels(x, mod_lat, gains[0], w_ch)
            a_lat = _fnet_tokens(*tok_tables[SEQ], y1, y2)
            if need_ctx:
                y1, y2 = _fnet_channels(ctx_s, mod_ctx, gains[0], w_ch)
                a_ctx = _fnet_tokens(*tok_tables[CTX_LEN], y1.reshape(batch, CTX_LEN, D_MODEL),
                                     y2.reshape(batch, CTX_LEN, D_MODEL))
            w_mix = fnet_w_out_bf
        x = _out_ffn(x, a_lat, mod_lat, gains[1], gains[2], gains[3], w_mix, j, w_gu_bf, w_down_bf, i, colmajor)
        if need_ctx:
            ctx_s = _out_ffn(ctx_s, a_ctx.reshape(1, batch * CTX_LEN, D_MODEL), mod_ctx,
                             gains[1], gains[2], gains[3], w_mix, j, w_gu_bf, w_down_bf, i, False)
    return x
```

```python
from functools import partial

import jax
import jax.numpy as jnp
import numpy as np
from jax import lax
from jax.experimental import pallas as pl
from jax.experimental.pallas import tpu as pltpu

D_MODEL = 2048
SEQ = 4096
CTX_LEN = 256
GRID_W = 64
DEPTH = 4
GLA_HEADS = 4
GLA_HEAD_K = 256
GLA_HEAD_V = 512
GLA_KEY_DIM = GLA_HEADS * GLA_HEAD_K
GLA_VALUE_DIM = GLA_HEADS * GLA_HEAD_V
GLA_GATE_RANK = 16
GLA_GATE_TAU = 16.0
GLA_MAIN_DIM = 2 * GLA_KEY_DIM + 2 * GLA_VALUE_DIM
FNET_GROUPS = 4
FNET_GROUP_DIM = D_MODEL // FNET_GROUPS
D_FF = 5632
N_ADA = 6
EPS = 1e-6

LANES = 128
ADA_ROWS = 8
VMEM_LIMIT = 56 * 1024 * 1024

TOK_TILE = 512
COLS_PER_TILE = TOK_TILE // GRID_W
ROW_SLAB = 512
FF_TILE = 512
IN_TILE = 1536
ADA_TILE = 1536
GLA_CHUNK = 128
GLA_UNROLL = 4
DFT_BM = 1024
DFT_BK = 512

BF16 = jnp.bfloat16
F32 = jnp.float32


def _cparams(sem):
    return pltpu.CompilerParams(dimension_semantics=sem, vmem_limit_bytes=VMEM_LIMIT)


def _dot(a, b):
    return jnp.dot(a, b, preferred_element_type=F32)


def _rms(x):
    return x * lax.rsqrt(jnp.mean(x * x, axis=-1, keepdims=True) + EPS)


def _modulate(x, gain, shift, scale):
    return (_rms(x) * gain) * (1.0 + scale) + shift


def _ada_kernel(c_ref, w_ref, b_ref, o_ref):
    c = c_ref[...]
    s = (c * jax.nn.sigmoid(c)).astype(BF16)
    o_ref[...] = _dot(s, w_ref[...].astype(BF16)) + b_ref[...]


def _ada_rows(cond, ada_w, ada_b):
    n_out = N_ADA * D_MODEL
    return pl.pallas_call(
        _ada_kernel,
        grid=(DEPTH, n_out // ADA_TILE),
        in_specs=[
            pl.BlockSpec((ADA_ROWS, D_MODEL), lambda i, n: (0, 0)),
            pl.BlockSpec((None, D_MODEL, ADA_TILE), lambda i, n: (i, 0, n)),
            pl.BlockSpec((None, 1, ADA_TILE), lambda i, n: (i, 0, n)),
        ],
        out_specs=pl.BlockSpec((None, ADA_ROWS, ADA_TILE), lambda i, n: (i, 0, n)),
        out_shape=jax.ShapeDtypeStruct((DEPTH, ADA_ROWS, n_out), F32),
        compiler_params=_cparams(("arbitrary", "arbitrary")),
        name="ada_rows",
    )(cond, ada_w, ada_b.reshape(DEPTH, 1, n_out))


def _tile_spec(colmajor):
    if colmajor:
        return pl.BlockSpec((None, GRID_W, COLS_PER_TILE, D_MODEL), lambda b, t, *_: (b, 0, t, 0))
    return pl.BlockSpec((None, TOK_TILE, D_MODEL), lambda b, t, *_: (b, t, 0))


def _load_tile(x_ref, colmajor):
    if not colmajor:
        return x_ref[...]
    return jnp.concatenate([x_ref[:, c, :] for c in range(COLS_PER_TILE)], axis=0)


def _load_rows(x_ref, rows, colmajor):
    if not colmajor:
        return x_ref[rows, :]
    cols = range(rows.start // GRID_W, rows.stop // GRID_W)
    return jnp.concatenate([x_ref[:, c, :] for c in cols], axis=0)


def _store_tile(o_ref, val, colmajor):
    if not colmajor:
        o_ref[...] = val
        return
    for c in range(COLS_PER_TILE):
        o_ref[:, c, :] = val[c * GRID_W:(c + 1) * GRID_W, :]


def _mod_spec(j):
    return pl.BlockSpec((None, 1, D_MODEL), lambda b, t, *_: (b, 0, j))


def _row_spec():
    return pl.BlockSpec((1, D_MODEL), lambda b, t, *_: (0, 0))


def _as_stream(x, colmajor):
    if colmajor:
        bq, s, d = x.shape
        return x.reshape(bq, GRID_W, s // GRID_W, d)
    return x


def _gla_in_kernel(x_ref, sh_ref, sc_ref, g_ref, w_ref, wlow_ref, o_ref, low_ref, h_ref, *, colmajor):
    n = pl.program_id(2)

    @pl.when(n == 0)
    def _():
        h = _modulate(_load_tile(x_ref, colmajor), g_ref[...], sh_ref[...], sc_ref[...]).astype(BF16)
        h_ref[...] = h
        low_ref[...] = _dot(h, wlow_ref[...])

    o_ref[...] = _dot(h_ref[...], w_ref[...]).astype(BF16)


def _gla_in(x, mods, gain, w_in, layer, w_low, colmajor):
    bq, s, _ = x.shape
    return pl.pallas_call(
        partial(_gla_in_kernel, colmajor=colmajor),
        grid=(bq, s // TOK_TILE, GLA_MAIN_DIM // IN_TILE),
        in_specs=[
            _tile_spec(colmajor), _mod_spec(0), _mod_spec(1), _row_spec(),
            pl.BlockSpec((None, D_MODEL, IN_TILE), lambda b, t, n: (layer, 0, n)),
            pl.BlockSpec((D_MODEL, LANES), lambda b, t, n: (0, 0)),
        ],
        out_specs=[
            pl.BlockSpec((None, TOK_TILE, IN_TILE), lambda b, t, n: (b, t, n)),
            pl.BlockSpec((None, TOK_TILE, LANES), lambda b, t, n: (b, t, 0)),
        ],
        out_shape=[
            jax.ShapeDtypeStruct((bq, s, GLA_MAIN_DIM), BF16),
            jax.ShapeDtypeStruct((bq, s, LANES), F32),
        ],
        scratch_shapes=[pltpu.VMEM((TOK_TILE, D_MODEL), BF16)],
        compiler_params=_cparams(("arbitrary", "arbitrary", "arbitrary")),
        name="gla_in",
    )(_as_stream(x, colmajor), mods, mods, gain, w_in, w_low)


def _log_gate(z):
    return (jnp.minimum(z, 0.0) - jnp.log(1.0 + jnp.exp(-jnp.abs(z)))) * (1.0 / GLA_GATE_TAU)


def _gla_group(q_ref, k_ref, v_ref, low_ref, w2_ref, bg_ref, tris, masks, s_ref, starts, u):
    c, dk = GLA_CHUNK, GLA_HEAD_K
    span = [pl.ds(pl.multiple_of(starts[d], c), u * c) for d in (0, 1)]
    z = [_dot(low_ref[span[d], :].astype(BF16), w2_ref[d]) + bg_ref[d] for d in (0, 1)]
    g = [_log_gate(zd) for zd in z]
    cums = []
    for d in (0, 1):
        parts = []
        for j in range(u):
            gj = g[d][j * c:(j + 1) * c, :]
            hi = gj.astype(BF16)
            parts += [hi, (gj - hi.astype(F32)).astype(BF16)]
        cum = _dot(tris[d], jnp.concatenate(parts, axis=1))
        cums.append([cum[:, 2 * j * dk:(2 * j + 1) * dk] + cum[:, (2 * j + 1) * dk:(2 * j + 2) * dk]
                     for j in range(u)])
    order = [(d, j if d == 0 else u - 1 - j) for j in range(u) for d in (0, 1)]
    work = {}
    for d, j in order:
        rows = pl.ds(pl.multiple_of(starts[d] + j * c, c), c)
        cum = cums[d][j]
        if d == 1:
            ref, tot = cum[c // 2:c // 2 + 1, :], cum[0:1, :]
        else:
            ref, tot = cum[c // 2 - 1:c // 2, :], cum[c - 1:c, :]
        qf = q_ref[rows, :].astype(F32) * (GLA_HEAD_K ** -0.5)
        kf = k_ref[rows, :].astype(F32)
        work[d, j] = dict(
            rows=rows, tot=tot, v=v_ref[rows, :],
            q_mid=(qf * jnp.exp(cum - ref)).astype(BF16), k_mid=(kf * jnp.exp(ref - cum)).astype(BF16),
            q_dec=(qf * jnp.exp(cum)).astype(BF16), k_end=(kf * jnp.exp(tot - cum)).astype(BF16))
    for key in order:
        w = work[key]
        s = lax.dot_general(w["q_mid"], w["k_mid"], (((1,), (1,)), ((), ())), preferred_element_type=F32)
        w["scores"] = jnp.where(masks[key[0]], s, 0.0).astype(BF16)
    for key in order:
        w = work[key]
        w["o"] = _dot(w["scores"], w["v"])
        w["ds"] = lax.dot_general(w["k_end"], w["v"], (((0,), (0,)), ((), ())), preferred_element_type=F32)
        dec = jnp.broadcast_to(jnp.exp(w["tot"]), (LANES, dk)).T
        w["dec"] = jnp.concatenate([dec] * (GLA_HEAD_V // LANES), axis=1)
    out = []
    state = [s_ref[0], s_ref[1]]
    for d, j in order:
        w = work[d, j]
        out.append((w["rows"], w["o"] + _dot(w["q_dec"], state[d].astype(BF16))))
        state[d] = state[d] * w["dec"] + w["ds"]
    s_ref[0] = state[0]
    s_ref[1] = state[1]
    return out


def _gla_kernel(qc_ref, kc_ref, vc_ref, rc_ref, lowc_ref, ql_ref, kl_ref, vl_ref, rl_ref, lowl_ref,
                w2_ref, bg_ref, hg_ref, oc_ref, ol_ref, s_ref, acc_ref):
    c = GLA_CHUNK
    row = lax.broadcasted_iota(jnp.int32, (c, c), 0)
    col = lax.broadcasted_iota(jnp.int32, (c, c), 1)
    masks = (col <= row, col >= row)
    tris = tuple(m.astype(BF16) for m in masks)
    hg = hg_ref[...]
    s_ref[...] = jnp.zeros_like(s_ref)

    def phase(q_ref, k_ref, v_ref, r_ref, low_ref, o_ref, length):
        u = min(GLA_UNROLL, length // (2 * c))
        n_groups = length // (u * c)
        half = n_groups // 2

        def group(n):
            starts = (n * (u * c), (n_groups - 1 - n) * (u * c))
            return _gla_group(q_ref, k_ref, v_ref, low_ref, w2_ref, bg_ref, tris, masks, s_ref, starts, u)

        def first_visit(n, carry):
            for rows, o in group(n):
                acc_ref[rows, :] = o
            return carry

        def second_visit(n, carry):
            for rows, o in group(n):
                o = _rms(o + acc_ref[rows, :]) * hg
                r = r_ref[rows, :].astype(F32)
                o_ref[rows, :] = (o * (r * jax.nn.sigmoid(r))).astype(BF16)
            return carry

        lax.fori_loop(0, half, first_visit, 0)
        lax.fori_loop(half, n_groups, second_visit, 0)

    phase(qc_ref, kc_ref, vc_ref, rc_ref, lowc_ref, oc_ref, CTX_LEN)
    phase(ql_ref, kl_ref, vl_ref, rl_ref, lowl_ref, ol_ref, SEQ)


def _gla_scan(qkvr_c, low_c, qkvr_l, low_l, w2, bg, head_gain):
    b = qkvr_l.shape[0]
    nk = GLA_KEY_DIM // GLA_HEAD_K

    def stream_specs(length):
        return [
            pl.BlockSpec((None, length, GLA_HEAD_K), lambda i, h: (i, 0, h)),
            pl.BlockSpec((None, length, GLA_HEAD_K), lambda i, h: (i, 0, nk + h)),
            pl.BlockSpec((None, length, GLA_HEAD_V), lambda i, h: (i, 0, nk + h)),
            pl.BlockSpec((None, length, GLA_HEAD_V), lambda i, h: (i, 0, 2 * nk + h)),
            pl.BlockSpec((None, length, LANES), lambda i, h: (i, 0, 0)),
        ]

    def out_spec(length):
        return pl.BlockSpec((None, length, GLA_HEAD_V), lambda i, h: (i, 0, h))

    return pl.pallas_call(
        _gla_kernel,
        grid=(b, GLA_HEADS),
        in_specs=stream_specs(CTX_LEN) + stream_specs(SEQ) + [
            pl.BlockSpec((2, LANES, GLA_HEAD_K), lambda i, h: (0, 0, h)),
            pl.BlockSpec((2, 1, GLA_HEAD_K), lambda i, h: (0, 0, h)),
            pl.BlockSpec((1, GLA_HEAD_V), lambda i, h: (0, 0)),
        ],
        out_specs=[out_spec(CTX_LEN), out_spec(SEQ)],
        out_shape=[
            jax.ShapeDtypeStruct((b, CTX_LEN, GLA_VALUE_DIM), BF16),
            jax.ShapeDtypeStruct((b, SEQ, GLA_VALUE_DIM), BF16),
        ],
        scratch_shapes=[
            pltpu.VMEM((2, GLA_HEAD_K, GLA_HEAD_V), F32),
            pltpu.VMEM((SEQ, GLA_HEAD_V), F32),
        ],
        compiler_params=_cparams(("arbitrary", "arbitrary")),
        name="gla_scan",
    )(qkvr_c, qkvr_c, qkvr_c, qkvr_c, low_c, qkvr_l, qkvr_l, qkvr_l, qkvr_l, low_l, w2, bg, head_gain)


def _fnet_ch_kernel(x_ref, sh_ref, sc_ref, g_ref, w_ref, y1_ref, y2_ref):
    h = _modulate(x_ref[...], g_ref[...], sh_ref[...], sc_ref[...]).astype(BF16)
    w = w_ref[...]
    gd = FNET_GROUP_DIM
    for g in range(FNET_GROUPS):
        r = _dot(h[:, g * gd:(g + 1) * gd], w).astype(BF16)
        y1_ref[:, g * gd:(g + 1) * gd] = r[:, :gd]
        y2_ref[:, g * gd:(g + 1) * gd] = r[:, gd:]


def _fnet_channels(x, mods, gain, w_ch):
    bq, s, _ = x.shape
    out = jax.ShapeDtypeStruct((bq, s, D_MODEL), BF16)
    return pl.pallas_call(
        _fnet_ch_kernel,
        grid=(bq, s // TOK_TILE),
        in_specs=[
            _tile_spec(False), _mod_spec(0), _mod_spec(1), _row_spec(),
            pl.BlockSpec((FNET_GROUP_DIM, 2 * FNET_GROUP_DIM), lambda b, t: (0, 0)),
        ],
        out_specs=[_tile_spec(False), _tile_spec(False)],
        out_shape=[out, out],
        compiler_params=_cparams(("arbitrary", "arbitrary")),
        name="fnet_channels",
    )(x, mods, mods, gain, w_ch)


def _fnet_tok_kernel(gc_ref, gs_ref, y1_ref, y2_ref, o_ref, acc_ref):
    k = pl.program_id(2)

    @pl.when(k == 0)
    def _():
        acc_ref[...] = jnp.zeros_like(acc_ref)

    acc_ref[...] += _dot(gc_ref[...], y1_ref[...]) + _dot(gs_ref[...], y2_ref[...])

    @pl.when(k == pl.num_programs(2) - 1)
    def _():
        o_ref[...] = acc_ref[...].astype(BF16)


def _fnet_tokens(gc, gs, y1, y2):
    b, length, _ = y1.shape
    bm, bk = min(DFT_BM, length), min(DFT_BK, length)
    y_spec = pl.BlockSpec((None, bk, D_MODEL), lambda i, m, k: (i, k, 0))
    g_spec = pl.BlockSpec((bm, bk), lambda i, m, k: (m, k))
    return pl.pallas_call(
        _fnet_tok_kernel,
        grid=(b, length // bm, length // bk),
        in_specs=[g_spec, g_spec, y_spec, y_spec],
        out_specs=pl.BlockSpec((None, bm, D_MODEL), lambda i, m, k: (i, m, 0)),
        out_shape=jax.ShapeDtypeStruct((b, length, D_MODEL), BF16),
        scratch_shapes=[pltpu.VMEM((bm, D_MODEL), F32)],
        compiler_params=_cparams(("arbitrary", "arbitrary", "arbitrary")),
        name="fnet_tokens",
    )(gc, gs, y1, y2)


def _dft_tables(n, split):
    lp = jnp.arange(n, dtype=jnp.int32)[:, None]
    hi = jnp.arange(n // split, dtype=jnp.int32)[None, :]
    lo = jnp.arange(split, dtype=jnp.int32)[None, :]
    ang_hi = ((lp * hi) % (n // split)).astype(F32) * (2.0 * np.pi * split / n)
    ang_lo = ((lp * lo) % n).astype(F32) * (2.0 * np.pi / n)
    ch, sh_, cl, sl = jnp.cos(ang_hi), jnp.sin(ang_hi), jnp.cos(ang_lo), jnp.sin(ang_lo)
    cos = ch[:, :, None] * cl[:, None, :] - sh_[:, :, None] * sl[:, None, :]
    sin = sh_[:, :, None] * cl[:, None, :] + ch[:, :, None] * sl[:, None, :]
    return cos.reshape(n, n), sin.reshape(n, n)


def _slabs():
    return [slice(i * ROW_SLAB, (i + 1) * ROW_SLAB) for i in range(TOK_TILE // ROW_SLAB)]


def _out_ffn_kernel(x_ref, a_ref, gtm_ref, shf_ref, scf_ref, gtf_ref, g1_ref, g2_ref, g3_ref,
                    wout_ref, wg_ref, wu_ref, wd_ref, o_ref, h_ref, acc_ref, *, colmajor):
    k = pl.program_id(2)

    x1_ref = o_ref.reshape(TOK_TILE, D_MODEL) if colmajor else o_ref

    @pl.when(k == 0)
    def _():
        w_out = wout_ref[...]
        ys = [_dot(a_ref[rows, :], w_out) for rows in _slabs()]
        for rows, y in zip(_slabs(), ys):
            x1 = _load_rows(x_ref, rows, colmajor) + gtm_ref[...] * (_rms(y) * g1_ref[...])
            x1_ref[rows, :] = x1
            h_ref[rows, :] = _modulate(x1, g2_ref[...], shf_ref[...], scf_ref[...]).astype(BF16)
        acc_ref[...] = jnp.zeros_like(acc_ref)

    h = h_ref[...]
    gate = _dot(h, wg_ref[...])
    up = _dot(h, wu_ref[...])
    act = (gate * jax.nn.sigmoid(gate) * up).astype(BF16)
    acc_ref[...] += _dot(act, wd_ref[...])

    @pl.when(k == pl.num_programs(2) - 1)
    def _():
        out = x1_ref[...] + gtf_ref[...] * (_rms(acc_ref[...]) * g3_ref[...])
        _store_tile(o_ref, out, colmajor)


def _out_ffn(x, a, mods, g1, g2, g3, w_out, mix_layer, w_gu, w_down, layer, colmajor):
    bq, s, _ = x.shape
    nf = D_FF // FF_TILE
    res = pl.pallas_call(
        partial(_out_ffn_kernel, colmajor=colmajor),
        grid=(bq, s // TOK_TILE, nf),
        in_specs=[
            _tile_spec(colmajor), _tile_spec(False),
            _mod_spec(2), _mod_spec(3), _mod_spec(4), _mod_spec(5),
            _row_spec(), _row_spec(), _row_spec(),
            pl.BlockSpec((None, D_MODEL, D_MODEL), lambda b, t, k: (mix_layer, 0, 0),
                         pipeline_mode=pl.Buffered(1)),
            pl.BlockSpec((None, D_MODEL, FF_TILE), lambda b, t, k: (layer, 0, k)),
            pl.BlockSpec((None, D_MODEL, FF_TILE), lambda b, t, k: (layer, 0, nf + k)),
            pl.BlockSpec((None, FF_TILE, D_MODEL), lambda b, t, k: (layer, k, 0)),
        ],
        out_specs=_tile_spec(colmajor),
        out_shape=jax.ShapeDtypeStruct(_as_stream(x, colmajor).shape, F32),
        scratch_shapes=[
            pltpu.VMEM((TOK_TILE, D_MODEL), BF16),
            pltpu.VMEM((TOK_TILE, D_MODEL), F32),
        ],
        compiler_params=_cparams(("arbitrary", "arbitrary", "arbitrary")),
        name="out_ffn",
    )(_as_stream(x, colmajor), a, mods, mods, mods, mods, g1, g2, g3, w_out, w_gu, w_gu, w_down)
    return res.reshape(x.shape)


def kernel(x, c, ctx, c_ctx, ada_w, ada_b, norm_gains, gla_w_in, gla_wg2_f, gla_bg_f, gla_wg2_b, gla_bg_b,
           gla_head_gain, gla_w_out, fnet_w_out, ffn_w_gu, ffn_w_down):
    batch = x.shape[0]
    cond = jnp.zeros((ADA_ROWS, D_MODEL), F32).at[:batch].set(c).at[batch].set(c_ctx)
    mods = _ada_rows(cond, ada_w, ada_b)
    ctx_s = ctx.reshape(1, batch * CTX_LEN, D_MODEL)

    n_ch = FNET_GROUP_DIM
    cos_c, sin_c = _dft_tables(n_ch, 16)
    w_ch = (jnp.concatenate([cos_c, sin_c], axis=1) * (n_ch ** -0.5)).astype(BF16)
    tok_tables = {}
    for length, split in ((SEQ, 64), (CTX_LEN, 16)):
        cos_t, sin_t = _dft_tables(length, split)
        tok_tables[length] = ((cos_t * (length ** -0.5)).astype(BF16), (sin_t * -(length ** -0.5)).astype(BF16))

    gla_w_in_bf = gla_w_in.astype(BF16)
    gla_w_out_bf = gla_w_out.astype(BF16)
    fnet_w_out_bf = fnet_w_out.astype(BF16)
    w_gu_bf = ffn_w_gu.astype(BF16)
    w_down_bf = ffn_w_down.astype(BF16)

    for i in range(DEPTH):
        need_ctx = i < DEPTH - 1
        j = i // 2
        mod_lat = mods[i, :batch].reshape(batch, 1, N_ADA * D_MODEL)
        mod_ctx = mods[i, batch:batch + 1].reshape(1, 1, N_ADA * D_MODEL)
        gains = [norm_gains[i, n].reshape(1, D_MODEL) for n in range(4)]
        colmajor = False
        if i % 2 == 0:
            colmajor = j % 2 == 1
            w_low = jnp.pad(gla_w_in[j, :, GLA_MAIN_DIM:],
                            ((0, 0), (0, LANES - 2 * GLA_GATE_RANK))).astype(BF16)
            w2 = jnp.zeros((2, LANES, GLA_KEY_DIM), F32)
            w2 = w2.at[0, :GLA_GATE_RANK].set(gla_wg2_f[j]).at[1, GLA_GATE_RANK:2 * GLA_GATE_RANK].set(gla_wg2_b[j])
            bg = jnp.stack([gla_bg_f[j], gla_bg_b[j]]).reshape(2, 1, GLA_KEY_DIM)
            qkvr_l, low_l = _gla_in(x, mod_lat, gains[0], gla_w_in_bf, j, w_low, colmajor)
            qkvr_c, low_c = _gla_in(ctx_s, mod_ctx, gains[0], gla_w_in_bf, j, w_low, False)
            a_ctx, a_lat = _gla_scan(
                qkvr_c.reshape(batch, CTX_LEN, GLA_MAIN_DIM), low_c.reshape(batch, CTX_LEN, LANES),
                qkvr_l, low_l, w2.astype(BF16), bg, gla_head_gain[j].reshape(1, GLA_HEAD_V))
            w_mix = gla_w_out_bf
        else:
            y1, y2 = _fnet_channels(x, mod_lat, gains[0], w_ch)
            a_lat = _fnet_tokens(*tok_tables[SEQ], y1, y2)
            if need_ctx:
                y1, y2 = _fnet_channels(ctx_s, mod_ctx, gains[0], w_ch)
                a_ctx = _fnet_tokens(*tok_tables[CTX_LEN], y1.reshape(batch, CTX_LEN, D_MODEL),
                                     y2.reshape(batch, CTX_LEN, D_MODEL))
            w_mix = fnet_w_out_bf
        x = _out_ffn(x, a_lat, mod_lat, gains[1], gains[2], gains[3], w_mix, j, w_gu_bf, w_down_bf, i, colmajor)
        if need_ctx:
            ctx_s = _out_ffn(ctx_s, a_ctx.reshape(1, batch * CTX_LEN, D_MODEL), mod_ctx,
                             gains[1], gains[2], gains[3], w_mix, j, w_gu_bf, w_down_bf, i, False)
    return x
```

```python
from functools import partial

import jax
import jax.numpy as jnp
import numpy as np
from jax import lax
from jax.experimental import pallas as pl
from jax.experimental.pallas import tpu as pltpu

D_MODEL = 2048
SEQ = 4096
CTX_LEN = 256
GRID_W = 64
DEPTH = 4
GLA_HEADS = 4
GLA_HEAD_K = 256
GLA_HEAD_V = 512
GLA_KEY_DIM = GLA_HEADS * GLA_HEAD_K
GLA_VALUE_DIM = GLA_HEADS * GLA_HEAD_V
GLA_GATE_RANK = 16
GLA_GATE_TAU = 16.0
GLA_MAIN_DIM = 2 * GLA_KEY_DIM + 2 * GLA_VALUE_DIM
FNET_GROUPS = 4
FNET_GROUP_DIM = D_MODEL // FNET_GROUPS
D_FF = 5632
N_ADA = 6
EPS = 1e-6

LANES = 128
ADA_ROWS = 8
VMEM_LIMIT = 56 * 1024 * 1024

TOK_TILE = 512
COLS_PER_TILE = TOK_TILE // GRID_W
FF_TILE = 512
IN_TILE = 1536
IN_TILE_F32 = 768
FF_TILE_F32 = 256
ADA_TILE = 1536
GLA_CHUNK = 128
GLA_UNROLL = 4
DFT_BM = 1024
DFT_BK = 512
DFT_RADIX = 8
COMBINE_ROWS = 128

BF16 = jnp.bfloat16
F32 = jnp.float32


def _cparams(sem):
    return pltpu.CompilerParams(dimension_semantics=sem, vmem_limit_bytes=VMEM_LIMIT)


def _dot(a, b):
    return jnp.dot(a, b, preferred_element_type=F32)


def _rms(x):
    return x * lax.rsqrt(jnp.mean(x * x, axis=-1, keepdims=True) + EPS)


def _modulate(x, gain, shift, scale):
    return (_rms(x) * gain) * (1.0 + scale) + shift


def _ada_kernel(c_ref, w_ref, b_ref, o_ref):
    c = c_ref[...]
    s = (c * jax.nn.sigmoid(c)).astype(BF16)
    o_ref[...] = _dot(s, w_ref[...].astype(BF16)) + b_ref[...]


def _ada_rows(cond, ada_w, ada_b):
    n_out = N_ADA * D_MODEL
    return pl.pallas_call(
        _ada_kernel,
        grid=(DEPTH, n_out // ADA_TILE),
        in_specs=[
            pl.BlockSpec((ADA_ROWS, D_MODEL), lambda i, n: (0, 0)),
            pl.BlockSpec((None, D_MODEL, ADA_TILE), lambda i, n: (i, 0, n)),
            pl.BlockSpec((None, 1, ADA_TILE), lambda i, n: (i, 0, n)),
        ],
        out_specs=pl.BlockSpec((None, ADA_ROWS, ADA_TILE), lambda i, n: (i, 0, n)),
        out_shape=jax.ShapeDtypeStruct((DEPTH, ADA_ROWS, n_out), F32),
        compiler_params=_cparams(("arbitrary", "arbitrary")),
        name="ada_rows",
    )(cond, ada_w, ada_b.reshape(DEPTH, 1, n_out))


def _tile_spec(colmajor):
    if colmajor:
        return pl.BlockSpec((None, GRID_W, COLS_PER_TILE, D_MODEL), lambda b, t, *_: (b, 0, t, 0))
    return pl.BlockSpec((None, TOK_TILE, D_MODEL), lambda b, t, *_: (b, t, 0))


def _load_tile(x_ref, colmajor):
    if not colmajor:
        return x_ref[...]
    return jnp.concatenate([x_ref[:, c, :] for c in range(COLS_PER_TILE)], axis=0)


def _store_tile(o_ref, val, colmajor):
    if not colmajor:
        o_ref[...] = val
        return
    for c in range(COLS_PER_TILE):
        o_ref[:, c, :] = val[c * GRID_W:(c + 1) * GRID_W, :]


def _mod_spec(j):
    return pl.BlockSpec((None, 1, D_MODEL), lambda b, t, *_: (b, 0, j))


def _row_spec():
    return pl.BlockSpec((1, D_MODEL), lambda b, t, *_: (0, 0))


def _as_stream(x, colmajor):
    if colmajor:
        bq, s, d = x.shape
        return x.reshape(bq, GRID_W, s // GRID_W, d)
    return x


def _gla_in_kernel(x_ref, sh_ref, sc_ref, g_ref, w_ref, wlow_ref, o_ref, low_ref, *rest, colmajor, emit_bf16):
    h_ref = rest[-1]
    n = pl.program_id(2)

    @pl.when(n == 0)
    def _():
        h = _modulate(_load_tile(x_ref, colmajor), g_ref[...], sh_ref[...], sc_ref[...]).astype(BF16)
        h_ref[...] = h
        low_ref[...] = _dot(h, wlow_ref[...])

    w = w_ref[...]
    if emit_bf16:
        w = w.astype(BF16)
        rest[0][...] = w
    o_ref[...] = _dot(h_ref[...], w).astype(BF16)


def _gla_in(x, mods, gain, w_in, w_low, colmajor, layer=None):
    bq, s, _ = x.shape
    emit = layer is not None
    tn = IN_TILE_F32 if emit else IN_TILE
    if emit:
        assert bq * s == TOK_TILE, "each weight block must be visited exactly once"
        w_spec = pl.BlockSpec((None, D_MODEL, tn), lambda b, t, n: (layer, 0, n))
    else:
        w_spec = pl.BlockSpec((D_MODEL, tn), lambda b, t, n: (0, n))
    out_specs = [
        pl.BlockSpec((None, TOK_TILE, tn), lambda b, t, n: (b, t, n)),
        pl.BlockSpec((None, TOK_TILE, LANES), lambda b, t, n: (b, t, 0)),
    ]
    out_shape = [
        jax.ShapeDtypeStruct((bq, s, GLA_MAIN_DIM), BF16),
        jax.ShapeDtypeStruct((bq, s, LANES), F32),
    ]
    if emit:
        out_specs.append(pl.BlockSpec((D_MODEL, tn), lambda b, t, n: (0, n)))
        out_shape.append(jax.ShapeDtypeStruct((D_MODEL, GLA_MAIN_DIM), BF16))
    return pl.pallas_call(
        partial(_gla_in_kernel, colmajor=colmajor, emit_bf16=emit),
        grid=(bq, s // TOK_TILE, GLA_MAIN_DIM // tn),
        in_specs=[
            _tile_spec(colmajor), _mod_spec(0), _mod_spec(1), _row_spec(),
            w_spec,
            pl.BlockSpec((D_MODEL, LANES), lambda b, t, n: (0, 0)),
        ],
        out_specs=out_specs,
        out_shape=out_shape,
        scratch_shapes=[pltpu.VMEM((TOK_TILE, D_MODEL), BF16)],
        compiler_params=_cparams(("arbitrary", "arbitrary", "arbitrary")),
        name="gla_in",
    )(_as_stream(x, colmajor), mods, mods, gain, w_in, w_low)


def _log_gate(z):
    return (jnp.minimum(z, 0.0) - jnp.log(1.0 + jnp.exp(-jnp.abs(z)))) * (1.0 / GLA_GATE_TAU)


def _gla_group(q_ref, k_ref, v_ref, low_ref, w2_ref, bg_ref, tris, masks, s_ref, starts, u):
    c, dk = GLA_CHUNK, GLA_HEAD_K
    span = [pl.ds(pl.multiple_of(starts[d], c), u * c) for d in (0, 1)]
    z = [_dot(low_ref[span[d], :].astype(BF16), w2_ref[d]) + bg_ref[d] for d in (0, 1)]
    g = [_log_gate(zd) for zd in z]
    cums = []
    for d in (0, 1):
        parts = []
        for j in range(u):
            gj = g[d][j * c:(j + 1) * c, :]
            hi = gj.astype(BF16)
            parts += [hi, (gj - hi.astype(F32)).astype(BF16)]
        cum = _dot(tris[d], jnp.concatenate(parts, axis=1))
        cums.append([cum[:, 2 * j * dk:(2 * j + 1) * dk] + cum[:, (2 * j + 1) * dk:(2 * j + 2) * dk]
                     for j in range(u)])
    order = [(d, j if d == 0 else u - 1 - j) for j in range(u) for d in (0, 1)]
    work = {}
    for d, j in order:
        rows = pl.ds(pl.multiple_of(starts[d] + j * c, c), c)
        cum = cums[d][j]
        if d == 1:
            ref, tot = cum[c // 2:c // 2 + 1, :], cum[0:1, :]
        else:
            ref, tot = cum[c // 2 - 1:c // 2, :], cum[c - 1:c, :]
        qf = q_ref[rows, :].astype(F32) * (GLA_HEAD_K ** -0.5)
        kf = k_ref[rows, :].astype(F32)
        work[d, j] = dict(
            rows=rows, tot=tot, v=v_ref[rows, :],
            q_mid=(qf * jnp.exp(cum - ref)).astype(BF16), k_mid=(kf * jnp.exp(ref - cum)).astype(BF16),
            q_dec=(qf * jnp.exp(cum)).astype(BF16), k_end=(kf * jnp.exp(tot - cum)).astype(BF16))
    for key in order:
        w = work[key]
        s = lax.dot_general(w["q_mid"], w["k_mid"], (((1,), (1,)), ((), ())), preferred_element_type=F32)
        w["scores"] = jnp.where(masks[key[0]], s, 0.0).astype(BF16)
    for key in order:
        w = work[key]
        w["o"] = _dot(w["scores"], w["v"])
        w["ds"] = lax.dot_general(w["k_end"], w["v"], (((0,), (0,)), ((), ())), preferred_element_type=F32)
        dec = jnp.broadcast_to(jnp.exp(w["tot"]), (LANES, dk)).T
        w["dec"] = jnp.concatenate([dec] * (GLA_HEAD_V // LANES), axis=1)
    out = []
    state = [s_ref[0], s_ref[1]]
    for d, j in order:
        w = work[d, j]
        out.append((w["rows"], w["o"] + _dot(w["q_dec"], state[d].astype(BF16))))
        state[d] = state[d] * w["dec"] + w["ds"]
    s_ref[0] = state[0]
    s_ref[1] = state[1]
    return out


def _gla_kernel(qc_ref, kc_ref, vc_ref, rc_ref, lowc_ref, ql_ref, kl_ref, vl_ref, rl_ref, lowl_ref,
                w2_ref, bg_ref, hg_ref, oc_ref, ol_ref, s_ref, acc_ref):
    c = GLA_CHUNK
    row = lax.broadcasted_iota(jnp.int32, (c, c), 0)
    col = lax.broadcasted_iota(jnp.int32, (c, c), 1)
    masks = (col <= row, col >= row)
    tris = tuple(m.astype(BF16) for m in masks)
    hg = hg_ref[...]
    s_ref[...] = jnp.zeros_like(s_ref)

    def phase(q_ref, k_ref, v_ref, r_ref, low_ref, o_ref, length):
        u = min(GLA_UNROLL, length // (2 * c))
        n_groups = length // (u * c)
        half = n_groups // 2

        def group(n):
            starts = (n * (u * c), (n_groups - 1 - n) * (u * c))
            return _gla_group(q_ref, k_ref, v_ref, low_ref, w2_ref, bg_ref, tris, masks, s_ref, starts, u)

        def first_visit(n, carry):
            for rows, o in group(n):
                acc_ref[rows, :] = o
            return carry

        def second_visit(n, carry):
            for rows, o in group(n):
                o = _rms(o + acc_ref[rows, :]) * hg
                r = r_ref[rows, :].astype(F32)
                o_ref[rows, :] = (o * (r * jax.nn.sigmoid(r))).astype(BF16)
            return carry

        lax.fori_loop(0, half, first_visit, 0)
        lax.fori_loop(half, n_groups, second_visit, 0)

    phase(qc_ref, kc_ref, vc_ref, rc_ref, lowc_ref, oc_ref, CTX_LEN)
    phase(ql_ref, kl_ref, vl_ref, rl_ref, lowl_ref, ol_ref, SEQ)


def _gla_scan(qkvr_c, low_c, qkvr_l, low_l, w2, bg, head_gain):
    b = qkvr_l.shape[0]
    nk = GLA_KEY_DIM // GLA_HEAD_K

    def stream_specs(length):
        return [
            pl.BlockSpec((None, length, GLA_HEAD_K), lambda i, h: (i, 0, h)),
            pl.BlockSpec((None, length, GLA_HEAD_K), lambda i, h: (i, 0, nk + h)),
            pl.BlockSpec((None, length, GLA_HEAD_V), lambda i, h: (i, 0, nk + h)),
            pl.BlockSpec((None, length, GLA_HEAD_V), lambda i, h: (i, 0, 2 * nk + h)),
            pl.BlockSpec((None, length, LANES), lambda i, h: (i, 0, 0)),
        ]

    def out_spec(length):
        return pl.BlockSpec((None, length, GLA_HEAD_V), lambda i, h: (i, 0, h))

    return pl.pallas_call(
        _gla_kernel,
        grid=(b, GLA_HEADS),
        in_specs=stream_specs(CTX_LEN) + stream_specs(SEQ) + [
            pl.BlockSpec((2, LANES, GLA_HEAD_K), lambda i, h: (0, 0, h)),
            pl.BlockSpec((2, 1, GLA_HEAD_K), lambda i, h: (0, 0, h)),
            pl.BlockSpec((1, GLA_HEAD_V), lambda i, h: (0, 0)),
        ],
        out_specs=[out_spec(CTX_LEN), out_spec(SEQ)],
        out_shape=[
            jax.ShapeDtypeStruct((b, CTX_LEN, GLA_VALUE_DIM), BF16),
            jax.ShapeDtypeStruct((b, SEQ, GLA_VALUE_DIM), BF16),
        ],
        scratch_shapes=[
            pltpu.VMEM((2, GLA_HEAD_K, GLA_HEAD_V), F32),
            pltpu.VMEM((SEQ, GLA_HEAD_V), F32),
        ],
        compiler_params=_cparams(("arbitrary", "arbitrary")),
        name="gla_scan",
    )(qkvr_c, qkvr_c, qkvr_c, qkvr_c, low_c, qkvr_l, qkvr_l, qkvr_l, qkvr_l, low_l, w2, bg, head_gain)


def _fnet_ch_kernel(x_ref, sh_ref, sc_ref, g_ref, w_ref, y1_ref, y2_ref, *, by_phase):
    if by_phase:
        x = jnp.concatenate([x_ref[:, s, :] for s in range(DFT_RADIX)], axis=0)
    else:
        x = x_ref[...]
    h = _modulate(x, g_ref[...], sh_ref[...], sc_ref[...]).astype(BF16)
    w = w_ref[...]
    gd = FNET_GROUP_DIM
    rows = TOK_TILE // DFT_RADIX
    for g in range(FNET_GROUPS):
        r = _dot(h[:, g * gd:(g + 1) * gd], w).astype(BF16)
        cols = slice(g * gd, (g + 1) * gd)
        if by_phase:
            for s in range(DFT_RADIX):
                y1_ref[s, :, cols] = r[s * rows:(s + 1) * rows, :gd]
                y2_ref[s, :, cols] = r[s * rows:(s + 1) * rows, gd:]
        else:
            y1_ref[:, cols] = r[:, :gd]
            y2_ref[:, cols] = r[:, gd:]


def _fnet_channels(x, mods, gain, w_ch, by_phase):
    bq, s, _ = x.shape
    if by_phase:
        rows = TOK_TILE // DFT_RADIX
        x = x.reshape(bq, s // DFT_RADIX, DFT_RADIX, D_MODEL)
        x_spec = pl.BlockSpec((None, rows, DFT_RADIX, D_MODEL), lambda b, t: (b, t, 0, 0))
        y_spec = pl.BlockSpec((None, DFT_RADIX, rows, D_MODEL), lambda b, t: (b, 0, t, 0))
        out = jax.ShapeDtypeStruct((bq, DFT_RADIX, s // DFT_RADIX, D_MODEL), BF16)
    else:
        x_spec = y_spec = _tile_spec(False)
        out = jax.ShapeDtypeStruct((bq, s, D_MODEL), BF16)
    return pl.pallas_call(
        partial(_fnet_ch_kernel, by_phase=by_phase),
        grid=(bq, s // TOK_TILE),
        in_specs=[
            x_spec, _mod_spec(0), _mod_spec(1), _row_spec(),
            pl.BlockSpec((FNET_GROUP_DIM, 2 * FNET_GROUP_DIM), lambda b, t: (0, 0)),
        ],
        out_specs=[y_spec, y_spec],
        out_shape=[out, out],
        compiler_params=_cparams(("arbitrary", "arbitrary")),
        name="fnet_channels",
    )(x, mods, mods, gain, w_ch)


def _fnet_sub_kernel(w_ref, y1_ref, y2_ref, o_ref):
    n = y1_ref.shape[0]
    o_ref[...] = (_dot(w_ref[:, :n], y1_ref[...]) + _dot(w_ref[:, n:], y2_ref[...])).astype(BF16)


def _fnet_sub_dft(w_sub, y1, y2):
    b, radix, n, _ = y1.shape
    y_spec = pl.BlockSpec((None, None, n, D_MODEL), lambda i, s: (i, s, 0, 0))
    return pl.pallas_call(
        _fnet_sub_kernel,
        grid=(b, radix),
        in_specs=[pl.BlockSpec((2 * n, 2 * n), lambda i, s: (0, 0)), y_spec, y_spec],
        out_specs=pl.BlockSpec((None, None, 2 * n, D_MODEL), lambda i, s: (i, s, 0, 0)),
        out_shape=jax.ShapeDtypeStruct((b, radix, 2 * n, D_MODEL), BF16),
        compiler_params=_cparams(("arbitrary", "arbitrary")),
        name="fnet_sub_dft",
    )(w_sub, y1, y2)


def _fft8_real(xs):
    r2 = float(np.sqrt(0.5))

    def add(a, b):
        return a[0] + b[0], a[1] + b[1]

    def sub(a, b):
        return a[0] - b[0], a[1] - b[1]

    def mul_neg_i(a):
        return a[1], -a[0]

    u = [add(xs[k], xs[k + 4]) for k in range(4)]
    d = [sub(xs[k], xs[k + 4]) for k in range(4)]
    v = [d[0],
         ((d[1][0] + d[1][1]) * r2, (d[1][1] - d[1][0]) * r2),
         mul_neg_i(d[2]),
         ((d[3][1] - d[3][0]) * r2, -(d[3][0] + d[3][1]) * r2)]

    def fft4_real(y):
        p0, p1 = add(y[0], y[2]), add(y[1], y[3])
        q0, q1 = sub(y[0], y[2]), mul_neg_i(sub(y[1], y[3]))
        return [p0[0] + p1[0], q0[0] + q1[0], p0[0] - p1[0], q0[0] - q1[0]]

    even, odd = fft4_real(u), fft4_real(v)
    return [even[0], odd[0], even[1], odd[1], even[2], odd[2], even[3], odd[3]]


def _fnet_combine_kernel(a_ref, b_ref, ca_ref, sa_ref, o_ref):
    bp = a_ref.shape[1]
    sub_rows = 16

    def body(rg, carry):
        rows = pl.ds(pl.multiple_of(rg * sub_rows, sub_rows), sub_rows)
        ca = [ca_ref[s, rows, :] for s in range(DFT_RADIX)]
        sa = [sa_ref[s, rows, :] for s in range(DFT_RADIX)]
        for lc in range(D_MODEL // LANES):
            lanes = slice(lc * LANES, (lc + 1) * LANES)
            xs = []
            for s in range(DFT_RADIX):
                a = a_ref[s, rows, lanes].astype(F32)
                b = b_ref[s, rows, lanes].astype(F32)
                if s == 0:
                    xs.append((a, -b))
                else:
                    xs.append((ca[s] * a - sa[s] * b, -(sa[s] * a + ca[s] * b)))
            for q, z in enumerate(_fft8_real(xs)):
                o_ref[q, rows, lanes] = z.astype(BF16)
        return carry

    lax.fori_loop(0, bp // sub_rows, body, 0)


def _fnet_combine(ab, ca, sa):
    b, radix, n2, _ = ab.shape
    n = n2 // 2
    bp = COMBINE_ROWS
    ab_spec = lambda off: pl.BlockSpec((None, radix, bp, D_MODEL), lambda i, p: (i, 0, off + p, 0))
    tw_spec = pl.BlockSpec((radix, bp, LANES), lambda i, p: (0, p, 0))
    z = pl.pallas_call(
        _fnet_combine_kernel,
        grid=(b, n // bp),
        in_specs=[ab_spec(0), ab_spec(n // bp), tw_spec, tw_spec],
        out_specs=pl.BlockSpec((None, radix, bp, D_MODEL), lambda i, p: (i, 0, p, 0)),
        out_shape=jax.ShapeDtypeStruct((b, radix, n, D_MODEL), BF16),
        compiler_params=_cparams(("arbitrary", "arbitrary")),
        name="fnet_combine",
    )(ab, ab, ca, sa)
    return z.reshape(b, radix * n, D_MODEL)


def _fnet_tok_kernel(gc_ref, gs_ref, y1_ref, y2_ref, o_ref, acc_ref):
    k = pl.program_id(2)

    @pl.when(k == 0)
    def _():
        acc_ref[...] = jnp.zeros_like(acc_ref)

    acc_ref[...] += _dot(gc_ref[...], y1_ref[...]) + _dot(gs_ref[...], y2_ref[...])

    @pl.when(k == pl.num_programs(2) - 1)
    def _():
        o_ref[...] = acc_ref[...].astype(BF16)


def _fnet_tokens(gc, gs, y1, y2):
    b, length, _ = y1.shape
    bm, bk = min(DFT_BM, length), min(DFT_BK, length)
    y_spec = pl.BlockSpec((None, bk, D_MODEL), lambda i, m, k: (i, k, 0))
    g_spec = pl.BlockSpec((bm, bk), lambda i, m, k: (m, k))
    return pl.pallas_call(
        _fnet_tok_kernel,
        grid=(b, length // bm, length // bk),
        in_specs=[g_spec, g_spec, y_spec, y_spec],
        out_specs=pl.BlockSpec((None, bm, D_MODEL), lambda i, m, k: (i, m, 0)),
        out_shape=jax.ShapeDtypeStruct((b, length, D_MODEL), BF16),
        scratch_shapes=[pltpu.VMEM((bm, D_MODEL), F32)],
        compiler_params=_cparams(("arbitrary", "arbitrary", "arbitrary")),
        name="fnet_tokens",
    )(gc, gs, y1, y2)


def _dft_tables(n, split):
    lp = jnp.arange(n, dtype=jnp.int32)[:, None]
    hi = jnp.arange(n // split, dtype=jnp.int32)[None, :]
    lo = jnp.arange(split, dtype=jnp.int32)[None, :]
    ang_hi = ((lp * hi) % (n // split)).astype(F32) * (2.0 * np.pi * split / n)
    ang_lo = ((lp * lo) % n).astype(F32) * (2.0 * np.pi / n)
    ch, sh_, cl, sl = jnp.cos(ang_hi), jnp.sin(ang_hi), jnp.cos(ang_lo), jnp.sin(ang_lo)
    cos = ch[:, :, None] * cl[:, None, :] - sh_[:, :, None] * sl[:, None, :]
    sin = sh_[:, :, None] * cl[:, None, :] + ch[:, :, None] * sl[:, None, :]
    return cos.reshape(n, n), sin.reshape(n, n)


def _out_ffn_kernel(x_ref, a_ref, gtm_ref, shf_ref, scf_ref, gtf_ref, g1_ref, g2_ref, g3_ref,
                    wout_ref, wg_ref, wu_ref, wd_ref, o_ref, *rest, colmajor, emit_bf16):
    h_ref, acc_ref = rest[-2:]
    k = pl.program_id(2)

    x1_ref = o_ref.reshape(TOK_TILE, D_MODEL) if colmajor else o_ref

    @pl.when(k == 0)
    def _():
        y = _dot(a_ref[...], wout_ref[...])
        x1 = _load_tile(x_ref, colmajor) + gtm_ref[...] * (_rms(y) * g1_ref[...])
        x1_ref[...] = x1
        h_ref[...] = _modulate(x1, g2_ref[...], shf_ref[...], scf_ref[...]).astype(BF16)
        acc_ref[...] = jnp.zeros_like(acc_ref)

    wg, wu, wd = wg_ref[...], wu_ref[...], wd_ref[...]
    if emit_bf16:
        wg, wu, wd = wg.astype(BF16), wu.astype(BF16), wd.astype(BF16)
        rest[0][...] = wg
        rest[1][...] = wu
        rest[2][...] = wd
    h = h_ref[...]
    gate = _dot(h, wg)
    up = _dot(h, wu)
    act = (gate * jax.nn.sigmoid(gate) * up).astype(BF16)
    acc_ref[...] += _dot(act, wd)

    @pl.when(k == pl.num_programs(2) - 1)
    def _():
        out = x1_ref[...] + gtf_ref[...] * (_rms(acc_ref[...]) * g3_ref[...])
        _store_tile(o_ref, out, colmajor)


def _out_ffn(x, a, mods, g1, g2, g3, w_out, mix_layer, ffn_w, colmajor, layer=None):
    bq, s, _ = x.shape
    emit = layer is not None
    tf = FF_TILE_F32 if emit else FF_TILE
    nf = D_FF // tf
    if emit:
        assert bq * s == TOK_TILE, "each weight block must be visited exactly once"
        w_gu, w_down = ffn_w
        weights = (w_gu, w_gu, w_down)
        w_specs = [
            pl.BlockSpec((None, D_MODEL, tf), lambda b, t, k: (layer, 0, k)),
            pl.BlockSpec((None, D_MODEL, tf), lambda b, t, k: (layer, 0, nf + k)),
            pl.BlockSpec((None, tf, D_MODEL), lambda b, t, k: (layer, k, 0)),
        ]
    else:
        weights = ffn_w
        w_specs = [
            pl.BlockSpec((D_MODEL, tf), lambda b, t, k: (0, k)),
            pl.BlockSpec((D_MODEL, tf), lambda b, t, k: (0, k)),
            pl.BlockSpec((tf, D_MODEL), lambda b, t, k: (k, 0)),
        ]
    out_specs = [_tile_spec(colmajor)]
    out_shape = [jax.ShapeDtypeStruct(_as_stream(x, colmajor).shape, F32)]
    if emit:
        out_specs += [
            pl.BlockSpec((D_MODEL, tf), lambda b, t, k: (0, k)),
            pl.BlockSpec((D_MODEL, tf), lambda b, t, k: (0, k)),
            pl.BlockSpec((tf, D_MODEL), lambda b, t, k: (k, 0)),
        ]
        out_shape += [
            jax.ShapeDtypeStruct((D_MODEL, D_FF), BF16),
            jax.ShapeDtypeStruct((D_MODEL, D_FF), BF16),
            jax.ShapeDtypeStruct((D_FF, D_MODEL), BF16),
        ]
    res = pl.pallas_call(
        partial(_out_ffn_kernel, colmajor=colmajor, emit_bf16=emit),
        grid=(bq, s // TOK_TILE, nf),
        in_specs=[
            _tile_spec(colmajor), _tile_spec(False),
            _mod_spec(2), _mod_spec(3), _mod_spec(4), _mod_spec(5),
            _row_spec(), _row_spec(), _row_spec(),
            pl.BlockSpec((None, D_MODEL, D_MODEL), lambda b, t, k: (mix_layer, 0, 0),
                         pipeline_mode=pl.Buffered(1)),
        ] + w_specs,
        out_specs=out_specs,
        out_shape=out_shape,
        scratch_shapes=[
            pltpu.VMEM((TOK_TILE, D_MODEL), BF16),
            pltpu.VMEM((TOK_TILE, D_MODEL), F32),
        ],
        compiler_params=_cparams(("arbitrary", "arbitrary", "arbitrary")),
        name="out_ffn",
    )(_as_stream(x, colmajor), a, mods, mods, mods, mods, g1, g2, g3, w_out, *weights)
    if emit:
        return res[0].reshape(x.shape), tuple(res[1:])
    return res[0].reshape(x.shape)


def kernel(x, c, ctx, c_ctx, ada_w, ada_b, norm_gains, gla_w_in, gla_wg2_f, gla_bg_f, gla_wg2_b, gla_bg_b,
           gla_head_gain, gla_w_out, fnet_w_out, ffn_w_gu, ffn_w_down):
    batch = x.shape[0]
    cond = jnp.zeros((ADA_ROWS, D_MODEL), F32).at[:batch].set(c).at[batch].set(c_ctx)
    mods = _ada_rows(cond, ada_w, ada_b)
    ctx_s = ctx.reshape(1, batch * CTX_LEN, D_MODEL)

    n_ch = FNET_GROUP_DIM
    cos_c, sin_c = _dft_tables(n_ch, 16)
    w_ch = (jnp.concatenate([cos_c, sin_c], axis=1) * (n_ch ** -0.5)).astype(BF16)
    cos_t, sin_t = _dft_tables(CTX_LEN, 16)
    ctx_tables = ((cos_t * (CTX_LEN ** -0.5)).astype(BF16), (sin_t * -(CTX_LEN ** -0.5)).astype(BF16))
    n_sub = SEQ // DFT_RADIX
    cos_s, sin_s = _dft_tables(n_sub, 16)
    w_sub = (jnp.block([[cos_s, -sin_s], [sin_s, cos_s]]) * (SEQ ** -0.5)).astype(BF16)
    tw_ang = ((jnp.arange(DFT_RADIX, dtype=jnp.int32)[:, None] * jnp.arange(n_sub, dtype=jnp.int32)[None, :])
              % SEQ).astype(F32) * (2.0 * np.pi / SEQ)
    tw_cos = jnp.broadcast_to(jnp.cos(tw_ang)[:, :, None], (DFT_RADIX, n_sub, LANES))
    tw_sin = jnp.broadcast_to(jnp.sin(tw_ang)[:, :, None], (DFT_RADIX, n_sub, LANES))

    gla_w_out_bf = gla_w_out.astype(BF16)
    fnet_w_out_bf = fnet_w_out.astype(BF16)

    for i in range(DEPTH):
        need_ctx = i < DEPTH - 1
        j = i // 2
        mod_lat = mods[i, :batch].reshape(batch, 1, N_ADA * D_MODEL)
        mod_ctx = mods[i, batch:batch + 1].reshape(1, 1, N_ADA * D_MODEL)
        gains = [norm_gains[i, n].reshape(1, D_MODEL) for n in range(4)]
        colmajor = False
        if i % 2 == 0:
            colmajor = j % 2 == 1
            w_low = jnp.pad(gla_w_in[j, :, GLA_MAIN_DIM:],
                            ((0, 0), (0, LANES - 2 * GLA_GATE_RANK))).astype(BF16)
            w2 = jnp.zeros((2, LANES, GLA_KEY_DIM), F32)
            w2 = w2.at[0, :GLA_GATE_RANK].set(gla_wg2_f[j]).at[1, GLA_GATE_RANK:2 * GLA_GATE_RANK].set(gla_wg2_b[j])
            bg = jnp.stack([gla_bg_f[j], gla_bg_b[j]]).reshape(2, 1, GLA_KEY_DIM)
            qkvr_c, low_c, w_in_bf = _gla_in(ctx_s, mod_ctx, gains[0], gla_w_in, w_low, False, layer=j)
            qkvr_l, low_l = _gla_in(x, mod_lat, gains[0], w_in_bf, w_low, colmajor)
            a_ctx, a_lat = _gla_scan(
                qkvr_c.reshape(batch, CTX_LEN, GLA_MAIN_DIM), low_c.reshape(batch, CTX_LEN, LANES),
                qkvr_l, low_l, w2.astype(BF16), bg, gla_head_gain[j].reshape(1, GLA_HEAD_V))
            w_mix = gla_w_out_bf
        else:
            y1, y2 = _fnet_channels(x, mod_lat, gains[0], w_ch, True)
            a_lat = _fnet_combine(_fnet_sub_dft(w_sub, y1, y2), tw_cos, tw_sin)
            if need_ctx:
                y1, y2 = _fnet_channels(ctx_s, mod_ctx, gains[0], w_ch, False)
                a_ctx = _fnet_tokens(*ctx_tables, y1.reshape(batch, CTX_LEN, D_MODEL),
                                     y2.reshape(batch, CTX_LEN, D_MODEL))
            w_mix = fnet_w_out_bf
        if need_ctx:
            ctx_s, ffn_bf = _out_ffn(ctx_s, a_ctx.reshape(1, batch * CTX_LEN, D_MODEL), mod_ctx,
                                     gains[1], gains[2], gains[3], w_mix, j, (ffn_w_gu, ffn_w_down), False,
                                     layer=i)
        else:
            ffn_bf = (ffn_w_gu[i, :, :D_FF].astype(BF16), ffn_w_gu[i, :, D_FF:].astype(BF16),
                      ffn_w_down[i].astype(BF16))
        x = _out_ffn(x, a_lat, mod_lat, gains[1], gains[2], gains[3], w_mix, j, ffn_bf, colmajor)
    return x
```

```python
from functools import partial

import jax
import jax.numpy as jnp
import numpy as np
from jax import lax
from jax.experimental import pallas as pl
from jax.experimental.pallas import tpu as pltpu

D_MODEL = 2048
SEQ = 4096
CTX_LEN = 256
GRID_W = 64
DEPTH = 4
GLA_HEADS = 4
GLA_HEAD_K = 256
GLA_HEAD_V = 512
GLA_KEY_DIM = GLA_HEADS * GLA_HEAD_K
GLA_VALUE_DIM = GLA_HEADS * GLA_HEAD_V
GLA_GATE_RANK = 16
GLA_GATE_TAU = 16.0
GLA_MAIN_DIM = 2 * GLA_KEY_DIM + 2 * GLA_VALUE_DIM
FNET_GROUPS = 4
FNET_GROUP_DIM = D_MODEL // FNET_GROUPS
D_FF = 5632
N_ADA = 6
EPS = 1e-6

LANES = 128
ADA_ROWS = 8
VMEM_LIMIT = 56 * 1024 * 1024

TOK_TILE = 512
COLS_PER_TILE = TOK_TILE // GRID_W
FF_TILE = 512
IN_TILE = 1536
IN_TILE_F32 = 768
FF_TILE_F32 = 256
ADA_TILE = 1536
GLA_CHUNK = 128
GLA_UNROLL = 4
DFT_BM = 1024
DFT_BK = 512
DFT_RADIX = 8
COMBINE_ROWS = 128

BF16 = jnp.bfloat16
F32 = jnp.float32


def _cparams(sem):
    return pltpu.CompilerParams(dimension_semantics=sem, vmem_limit_bytes=VMEM_LIMIT)


def _dot(a, b):
    return jnp.dot(a, b, preferred_element_type=F32)


def _rms(x):
    return x * lax.rsqrt(jnp.mean(x * x, axis=-1, keepdims=True) + EPS)


def _modulate(x, gain, shift, scale):
    return (_rms(x) * gain) * (1.0 + scale) + shift


def _ada_kernel(c_ref, w_ref, b_ref, o_ref):
    c = c_ref[...]
    s = (c * jax.nn.sigmoid(c)).astype(BF16)
    o_ref[...] = _dot(s, w_ref[...].astype(BF16)) + b_ref[...]


def _ada_rows(cond, ada_w, ada_b):
    n_out = N_ADA * D_MODEL
    return pl.pallas_call(
        _ada_kernel,
        grid=(DEPTH, n_out // ADA_TILE),
        in_specs=[
            pl.BlockSpec((ADA_ROWS, D_MODEL), lambda i, n: (0, 0)),
            pl.BlockSpec((None, D_MODEL, ADA_TILE), lambda i, n: (i, 0, n)),
            pl.BlockSpec((None, 1, ADA_TILE), lambda i, n: (i, 0, n)),
        ],
        out_specs=pl.BlockSpec((None, ADA_ROWS, ADA_TILE), lambda i, n: (i, 0, n)),
        out_shape=jax.ShapeDtypeStruct((DEPTH, ADA_ROWS, n_out), F32),
        compiler_params=_cparams(("arbitrary", "arbitrary")),
        name="ada_rows",
    )(cond, ada_w, ada_b.reshape(DEPTH, 1, n_out))


def _tile_spec(colmajor):
    if colmajor:
        return pl.BlockSpec((None, GRID_W, COLS_PER_TILE, D_MODEL), lambda b, t, *_: (b, 0, t, 0))
    return pl.BlockSpec((None, TOK_TILE, D_MODEL), lambda b, t, *_: (b, t, 0))


def _load_tile(x_ref, colmajor):
    if not colmajor:
        return x_ref[...]
    return jnp.concatenate([x_ref[:, c, :] for c in range(COLS_PER_TILE)], axis=0)


def _store_tile(o_ref, val, colmajor):
    if not colmajor:
        o_ref[...] = val
        return
    for c in range(COLS_PER_TILE):
        o_ref[:, c, :] = val[c * GRID_W:(c + 1) * GRID_W, :]


def _mod_spec(j):
    return pl.BlockSpec((None, 1, D_MODEL), lambda b, t, *_: (b, 0, j))


def _row_spec():
    return pl.BlockSpec((1, D_MODEL), lambda b, t, *_: (0, 0))


def _as_stream(x, colmajor):
    if colmajor:
        bq, s, d = x.shape
        return x.reshape(bq, GRID_W, s // GRID_W, d)
    return x


def _gla_in_kernel(x_ref, sh_ref, sc_ref, g_ref, w_ref, wlow_ref, o_ref, low_ref, *rest, colmajor, emit_bf16):
    h_ref = rest[-1]
    n = pl.program_id(2)

    @pl.when(n == 0)
    def _():
        h = _modulate(_load_tile(x_ref, colmajor), g_ref[...], sh_ref[...], sc_ref[...]).astype(BF16)
        h_ref[...] = h
        low_ref[...] = _dot(h, wlow_ref[...])

    w = w_ref[...]
    if emit_bf16:
        w = w.astype(BF16)
        rest[0][...] = w
    o_ref[...] = _dot(h_ref[...], w).astype(BF16)


def _gla_in(x, mods, gain, w_in, w_low, colmajor, layer=None):
    bq, s, _ = x.shape
    emit = layer is not None
    tn = IN_TILE_F32 if emit else IN_TILE
    if emit:
        assert bq * s == TOK_TILE, "each weight block must be visited exactly once"
        w_spec = pl.BlockSpec((None, D_MODEL, tn), lambda b, t, n: (layer, 0, n))
    else:
        w_spec = pl.BlockSpec((None, D_MODEL, tn), lambda b, t, n: (n, 0, 0))
    out_specs = [
        pl.BlockSpec((None, TOK_TILE, tn), lambda b, t, n: (b, t, n)),
        pl.BlockSpec((None, TOK_TILE, LANES), lambda b, t, n: (b, t, 0)),
    ]
    out_shape = [
        jax.ShapeDtypeStruct((bq, s, GLA_MAIN_DIM), BF16),
        jax.ShapeDtypeStruct((bq, s, LANES), F32),
    ]
    if emit:
        per = IN_TILE // tn
        out_specs.append(pl.BlockSpec((None, D_MODEL, tn), lambda b, t, n: (n // per, 0, n % per)))
        out_shape.append(jax.ShapeDtypeStruct((GLA_MAIN_DIM // IN_TILE, D_MODEL, IN_TILE), BF16))
    return pl.pallas_call(
        partial(_gla_in_kernel, colmajor=colmajor, emit_bf16=emit),
        grid=(bq, s // TOK_TILE, GLA_MAIN_DIM // tn),
        in_specs=[
            _tile_spec(colmajor), _mod_spec(0), _mod_spec(1), _row_spec(),
            w_spec,
            pl.BlockSpec((D_MODEL, LANES), lambda b, t, n: (0, 0)),
        ],
        out_specs=out_specs,
        out_shape=out_shape,
        scratch_shapes=[pltpu.VMEM((TOK_TILE, D_MODEL), BF16)],
        compiler_params=_cparams(("arbitrary", "arbitrary", "arbitrary")),
        name="gla_in",
    )(_as_stream(x, colmajor), mods, mods, gain, w_in, w_low)


def _log_gate(z):
    return (jnp.minimum(z, 0.0) - jnp.log(1.0 + jnp.exp(-jnp.abs(z)))) * (1.0 / GLA_GATE_TAU)


def _gla_group(q_ref, k_ref, v_ref, low_ref, w2_ref, bg_ref, tris, masks, s_ref, starts, u):
    c, dk = GLA_CHUNK, GLA_HEAD_K
    span = [pl.ds(pl.multiple_of(starts[d], c), u * c) for d in (0, 1)]
    z = [_dot(low_ref[span[d], :].astype(BF16), w2_ref[d]) + bg_ref[d] for d in (0, 1)]
    g = [_log_gate(zd) for zd in z]
    cums = []
    for d in (0, 1):
        parts = []
        for j in range(u):
            gj = g[d][j * c:(j + 1) * c, :]
            hi = gj.astype(BF16)
            parts += [hi, (gj - hi.astype(F32)).astype(BF16)]
        cum = _dot(tris[d], jnp.concatenate(parts, axis=1))
        cums.append([cum[:, 2 * j * dk:(2 * j + 1) * dk] + cum[:, (2 * j + 1) * dk:(2 * j + 2) * dk]
                     for j in range(u)])
    order = [(d, j if d == 0 else u - 1 - j) for j in range(u) for d in (0, 1)]
    work = {}
    for d, j in order:
        rows = pl.ds(pl.multiple_of(starts[d] + j * c, c), c)
        cum = cums[d][j]
        if d == 1:
            ref, tot = cum[c // 2:c // 2 + 1, :], cum[0:1, :]
        else:
            ref, tot = cum[c // 2 - 1:c // 2, :], cum[c - 1:c, :]
        qf = q_ref[rows, :].astype(F32) * (GLA_HEAD_K ** -0.5)
        kf = k_ref[rows, :].astype(F32)
        work[d, j] = dict(
            rows=rows, tot=tot, v=v_ref[rows, :],
            q_mid=(qf * jnp.exp(cum - ref)).astype(BF16), k_mid=(kf * jnp.exp(ref - cum)).astype(BF16),
            q_dec=(qf * jnp.exp(cum)).astype(BF16), k_end=(kf * jnp.exp(tot - cum)).astype(BF16))
    for key in order:
        w = work[key]
        s = lax.dot_general(w["q_mid"], w["k_mid"], (((1,), (1,)), ((), ())), preferred_element_type=F32)
        w["scores"] = jnp.where(masks[key[0]], s, 0.0).astype(BF16)
    for key in order:
        w = work[key]
        w["o"] = _dot(w["scores"], w["v"])
        w["ds"] = lax.dot_general(w["k_end"], w["v"], (((0,), (0,)), ((), ())), preferred_element_type=F32)
        dec = jnp.broadcast_to(jnp.exp(w["tot"]), (LANES, dk)).T
        w["dec"] = jnp.concatenate([dec] * (GLA_HEAD_V // LANES), axis=1)
    out = []
    state = [s_ref[0], s_ref[1]]
    for d, j in order:
        w = work[d, j]
        out.append((w["rows"], w["o"] + _dot(w["q_dec"], state[d].astype(BF16))))
        state[d] = state[d] * w["dec"] + w["ds"]
    s_ref[0] = state[0]
    s_ref[1] = state[1]
    return out


def _gla_kernel(qc_ref, kc_ref, vc_ref, rc_ref, lowc_ref, ql_ref, kl_ref, vl_ref, rl_ref, lowl_ref,
                w2_ref, bg_ref, hg_ref, oc_ref, ol_ref, s_ref, acc_ref):
    c = GLA_CHUNK
    row = lax.broadcasted_iota(jnp.int32, (c, c), 0)
    col = lax.broadcasted_iota(jnp.int32, (c, c), 1)
    masks = (col <= row, col >= row)
    tris = tuple(m.astype(BF16) for m in masks)
    hg = hg_ref[...]
    s_ref[...] = jnp.zeros_like(s_ref)

    def phase(q_ref, k_ref, v_ref, r_ref, low_ref, o_ref, length):
        u = min(GLA_UNROLL, length // (2 * c))
        n_groups = length // (u * c)
        half = n_groups // 2

        def group(n):
            starts = (n * (u * c), (n_groups - 1 - n) * (u * c))
            return _gla_group(q_ref, k_ref, v_ref, low_ref, w2_ref, bg_ref, tris, masks, s_ref, starts, u)

        def first_visit(n, carry):
            for rows, o in group(n):
                acc_ref[rows, :] = o
            return carry

        def second_visit(n, carry):
            for rows, o in group(n):
                o = _rms(o + acc_ref[rows, :]) * hg
                r = r_ref[rows, :].astype(F32)
                o_ref[rows, :] = (o * (r * jax.nn.sigmoid(r))).astype(BF16)
            return carry

        lax.fori_loop(0, half, first_visit, 0)
        lax.fori_loop(half, n_groups, second_visit, 0)

    phase(qc_ref, kc_ref, vc_ref, rc_ref, lowc_ref, oc_ref, CTX_LEN)
    phase(ql_ref, kl_ref, vl_ref, rl_ref, lowl_ref, ol_ref, SEQ)


def _gla_scan(qkvr_c, low_c, qkvr_l, low_l, w2, bg, head_gain):
    b = qkvr_l.shape[0]
    nk = GLA_KEY_DIM // GLA_HEAD_K

    def stream_specs(length):
        return [
            pl.BlockSpec((None, length, GLA_HEAD_K), lambda i, h: (i, 0, h)),
            pl.BlockSpec((None, length, GLA_HEAD_K), lambda i, h: (i, 0, nk + h)),
            pl.BlockSpec((None, length, GLA_HEAD_V), lambda i, h: (i, 0, nk + h)),
            pl.BlockSpec((None, length, GLA_HEAD_V), lambda i, h: (i, 0, 2 * nk + h)),
            pl.BlockSpec((None, length, LANES), lambda i, h: (i, 0, 0)),
        ]

    def out_spec(length):
        return pl.BlockSpec((None, length, GLA_HEAD_V), lambda i, h: (i, 0, h))

    return pl.pallas_call(
        _gla_kernel,
        grid=(b, GLA_HEADS),
        in_specs=stream_specs(CTX_LEN) + stream_specs(SEQ) + [
            pl.BlockSpec((2, LANES, GLA_HEAD_K), lambda i, h: (0, 0, h)),
            pl.BlockSpec((2, 1, GLA_HEAD_K), lambda i, h: (0, 0, h)),
            pl.BlockSpec((1, GLA_HEAD_V), lambda i, h: (0, 0)),
        ],
        out_specs=[out_spec(CTX_LEN), out_spec(SEQ)],
        out_shape=[
            jax.ShapeDtypeStruct((b, CTX_LEN, GLA_VALUE_DIM), BF16),
            jax.ShapeDtypeStruct((b, SEQ, GLA_VALUE_DIM), BF16),
        ],
        scratch_shapes=[
            pltpu.VMEM((2, GLA_HEAD_K, GLA_HEAD_V), F32),
            pltpu.VMEM((SEQ, GLA_HEAD_V), F32),
        ],
        compiler_params=_cparams(("arbitrary", "arbitrary")),
        name="gla_scan",
    )(qkvr_c, qkvr_c, qkvr_c, qkvr_c, low_c, qkvr_l, qkvr_l, qkvr_l, qkvr_l, low_l, w2, bg, head_gain)


def _fnet_ch_kernel(x_ref, sh_ref, sc_ref, g_ref, w_ref, y1_ref, y2_ref, *, by_phase):
    if by_phase:
        x = jnp.concatenate([x_ref[:, s, :] for s in range(DFT_RADIX)], axis=0)
    else:
        x = x_ref[...]
    h = _modulate(x, g_ref[...], sh_ref[...], sc_ref[...]).astype(BF16)
    w = w_ref[...]
    gd = FNET_GROUP_DIM
    rows = TOK_TILE // DFT_RADIX
    for g in range(FNET_GROUPS):
        r = _dot(h[:, g * gd:(g + 1) * gd], w).astype(BF16)
        cols = slice(g * gd, (g + 1) * gd)
        if by_phase:
            for s in range(DFT_RADIX):
                y1_ref[s, :, cols] = r[s * rows:(s + 1) * rows, :gd]
                y2_ref[s, :, cols] = r[s * rows:(s + 1) * rows, gd:]
        else:
            y1_ref[:, cols] = r[:, :gd]
            y2_ref[:, cols] = r[:, gd:]


def _fnet_channels(x, mods, gain, w_ch, by_phase):
    bq, s, _ = x.shape
    if by_phase:
        rows = TOK_TILE // DFT_RADIX
        x = x.reshape(bq, s // DFT_RADIX, DFT_RADIX, D_MODEL)
        x_spec = pl.BlockSpec((None, rows, DFT_RADIX, D_MODEL), lambda b, t: (b, t, 0, 0))
        y_spec = pl.BlockSpec((None, DFT_RADIX, rows, D_MODEL), lambda b, t: (b, 0, t, 0))
        out = jax.ShapeDtypeStruct((bq, DFT_RADIX, s // DFT_RADIX, D_MODEL), BF16)
    else:
        x_spec = y_spec = _tile_spec(False)
        out = jax.ShapeDtypeStruct((bq, s, D_MODEL), BF16)
    return pl.pallas_call(
        partial(_fnet_ch_kernel, by_phase=by_phase),
        grid=(bq, s // TOK_TILE),
        in_specs=[
            x_spec, _mod_spec(0), _mod_spec(1), _row_spec(),
            pl.BlockSpec((FNET_GROUP_DIM, 2 * FNET_GROUP_DIM), lambda b, t: (0, 0)),
        ],
        out_specs=[y_spec, y_spec],
        out_shape=[out, out],
        compiler_params=_cparams(("arbitrary", "arbitrary")),
        name="fnet_channels",
    )(x, mods, mods, gain, w_ch)


def _fnet_sub_kernel(w_ref, y1_ref, y2_ref, o_ref):
    n = y1_ref.shape[0]
    o_ref[...] = (_dot(w_ref[:, :n], y1_ref[...]) + _dot(w_ref[:, n:], y2_ref[...])).astype(BF16)


def _fnet_sub_dft(w_sub, y1, y2):
    b, radix, n, _ = y1.shape
    y_spec = pl.BlockSpec((None, None, n, D_MODEL), lambda i, s: (i, s, 0, 0))
    return pl.pallas_call(
        _fnet_sub_kernel,
        grid=(b, radix),
        in_specs=[pl.BlockSpec((2 * n, 2 * n), lambda i, s: (0, 0)), y_spec, y_spec],
        out_specs=pl.BlockSpec((None, None, 2 * n, D_MODEL), lambda i, s: (i, s, 0, 0)),
        out_shape=jax.ShapeDtypeStruct((b, radix, 2 * n, D_MODEL), BF16),
        compiler_params=_cparams(("arbitrary", "arbitrary")),
        name="fnet_sub_dft",
    )(w_sub, y1, y2)


def _fft8_real(xs):
    r2 = float(np.sqrt(0.5))

    def add(a, b):
        return a[0] + b[0], a[1] + b[1]

    def sub(a, b):
        return a[0] - b[0], a[1] - b[1]

    def mul_neg_i(a):
        return a[1], -a[0]

    u = [add(xs[k], xs[k + 4]) for k in range(4)]
    d = [sub(xs[k], xs[k + 4]) for k in range(4)]
    v = [d[0],
         ((d[1][0] + d[1][1]) * r2, (d[1][1] - d[1][0]) * r2),
         mul_neg_i(d[2]),
         ((d[3][1] - d[3][0]) * r2, -(d[3][0] + d[3][1]) * r2)]

    def fft4_real(y):
        p0, p1 = add(y[0], y[2]), add(y[1], y[3])
        q0, q1 = sub(y[0], y[2]), mul_neg_i(sub(y[1], y[3]))
        return [p0[0] + p1[0], q0[0] + q1[0], p0[0] - p1[0], q0[0] - q1[0]]

    even, odd = fft4_real(u), fft4_real(v)
    return [even[0], odd[0], even[1], odd[1], even[2], odd[2], even[3], odd[3]]


def _fnet_combine_kernel(a_ref, b_ref, ca_ref, sa_ref, o_ref):
    bp = a_ref.shape[1]
    sub_rows = 16

    def body(rg, carry):
        rows = pl.ds(pl.multiple_of(rg * sub_rows, sub_rows), sub_rows)
        ca = [ca_ref[s, rows, :] for s in range(DFT_RADIX)]
        sa = [sa_ref[s, rows, :] for s in range(DFT_RADIX)]
        for lc in range(D_MODEL // LANES):
            lanes = slice(lc * LANES, (lc + 1) * LANES)
            xs = []
            for s in range(DFT_RADIX):
                a = a_ref[s, rows, lanes].astype(F32)
                b = b_ref[s, rows, lanes].astype(F32)
                if s == 0:
                    xs.append((a, -b))
                else:
                    xs.append((ca[s] * a - sa[s] * b, -(sa[s] * a + ca[s] * b)))
            for q, z in enumerate(_fft8_real(xs)):
                o_ref[q, rows, lanes] = z.astype(BF16)
        return carry

    lax.fori_loop(0, bp // sub_rows, body, 0)


def _fnet_combine(ab, ca, sa):
    b, radix, n2, _ = ab.shape
    n = n2 // 2
    bp = COMBINE_ROWS
    ab_spec = lambda off: pl.BlockSpec((None, radix, bp, D_MODEL), lambda i, p: (i, 0, off + p, 0))
    tw_spec = pl.BlockSpec((radix, bp, LANES), lambda i, p: (0, p, 0))
    z = pl.pallas_call(
        _fnet_combine_kernel,
        grid=(b, n // bp),
        in_specs=[ab_spec(0), ab_spec(n // bp), tw_spec, tw_spec],
        out_specs=pl.BlockSpec((None, radix, bp, D_MODEL), lambda i, p: (i, 0, p, 0)),
        out_shape=jax.ShapeDtypeStruct((b, radix, n, D_MODEL), BF16),
        compiler_params=_cparams(("arbitrary", "arbitrary")),
        name="fnet_combine",
    )(ab, ab, ca, sa)
    return z.reshape(b, radix * n, D_MODEL)


def _fnet_tok_kernel(gc_ref, gs_ref, y1_ref, y2_ref, o_ref, acc_ref):
    k = pl.program_id(2)

    @pl.when(k == 0)
    def _():
        acc_ref[...] = jnp.zeros_like(acc_ref)

    acc_ref[...] += _dot(gc_ref[...], y1_ref[...]) + _dot(gs_ref[...], y2_ref[...])

    @pl.when(k == pl.num_programs(2) - 1)
    def _():
        o_ref[...] = acc_ref[...].astype(BF16)


def _fnet_tokens(gc, gs, y1, y2):
    b, length, _ = y1.shape
    bm, bk = min(DFT_BM, length), min(DFT_BK, length)
    y_spec = pl.BlockSpec((None, bk, D_MODEL), lambda i, m, k: (i, k, 0))
    g_spec = pl.BlockSpec((bm, bk), lambda i, m, k: (m, k))
    return pl.pallas_call(
        _fnet_tok_kernel,
        grid=(b, length // bm, length // bk),
        in_specs=[g_spec, g_spec, y_spec, y_spec],
        out_specs=pl.BlockSpec((None, bm, D_MODEL), lambda i, m, k: (i, m, 0)),
        out_shape=jax.ShapeDtypeStruct((b, length, D_MODEL), BF16),
        scratch_shapes=[pltpu.VMEM((bm, D_MODEL), F32)],
        compiler_params=_cparams(("arbitrary", "arbitrary", "arbitrary")),
        name="fnet_tokens",
    )(gc, gs, y1, y2)


def _dft_tables(n, split):
    lp = jnp.arange(n, dtype=jnp.int32)[:, None]
    hi = jnp.arange(n // split, dtype=jnp.int32)[None, :]
    lo = jnp.arange(split, dtype=jnp.int32)[None, :]
    ang_hi = ((lp * hi) % (n // split)).astype(F32) * (2.0 * np.pi * split / n)
    ang_lo = ((lp * lo) % n).astype(F32) * (2.0 * np.pi / n)
    ch, sh_, cl, sl = jnp.cos(ang_hi), jnp.sin(ang_hi), jnp.cos(ang_lo), jnp.sin(ang_lo)
    cos = ch[:, :, None] * cl[:, None, :] - sh_[:, :, None] * sl[:, None, :]
    sin = sh_[:, :, None] * cl[:, None, :] + ch[:, :, None] * sl[:, None, :]
    return cos.reshape(n, n), sin.reshape(n, n)


def _out_ffn_kernel(x_ref, a_ref, gtm_ref, shf_ref, scf_ref, gtf_ref, g1_ref, g2_ref, g3_ref,
                    wout_ref, wg_ref, wu_ref, wd_ref, o_ref, *rest, colmajor, emit_bf16):
    h_ref, acc_ref = rest[-2:]
    k = pl.program_id(2)

    x1_ref = o_ref.reshape(TOK_TILE, D_MODEL) if colmajor else o_ref

    @pl.when(k == 0)
    def _():
        y = _dot(a_ref[...], wout_ref[...])
        x1 = _load_tile(x_ref, colmajor) + gtm_ref[...] * (_rms(y) * g1_ref[...])
        x1_ref[...] = x1
        h_ref[...] = _modulate(x1, g2_ref[...], shf_ref[...], scf_ref[...]).astype(BF16)
        acc_ref[...] = jnp.zeros_like(acc_ref)

    wg, wu, wd = wg_ref[...], wu_ref[...], wd_ref[...]
    if emit_bf16:
        wg, wu, wd = wg.astype(BF16), wu.astype(BF16), wd.astype(BF16)
        rest[0][...] = wg
        rest[1][...] = wu
        rest[2][...] = wd
    h = h_ref[...]
    gate = _dot(h, wg)
    up = _dot(h, wu)
    act = (gate * jax.nn.sigmoid(gate) * up).astype(BF16)
    acc_ref[...] += _dot(act, wd)

    @pl.when(k == pl.num_programs(2) - 1)
    def _():
        out = x1_ref[...] + gtf_ref[...] * (_rms(acc_ref[...]) * g3_ref[...])
        _store_tile(o_ref, out, colmajor)


def _out_ffn(x, a, mods, g1, g2, g3, w_out, mix_layer, ffn_w, colmajor, layer=None):
    bq, s, _ = x.shape
    emit = layer is not None
    tf = FF_TILE_F32 if emit else FF_TILE
    nf = D_FF // tf
    if emit:
        assert bq * s == TOK_TILE, "each weight block must be visited exactly once"
        w_gu, w_down = ffn_w
        weights = (w_gu, w_gu, w_down)
        w_specs = [
            pl.BlockSpec((None, D_MODEL, tf), lambda b, t, k: (layer, 0, k)),
            pl.BlockSpec((None, D_MODEL, tf), lambda b, t, k: (layer, 0, nf + k)),
            pl.BlockSpec((None, tf, D_MODEL), lambda b, t, k: (layer, k, 0)),
        ]
    else:
        weights = ffn_w
        w_specs = [
            pl.BlockSpec((None, D_MODEL, tf), lambda b, t, k: (k, 0, 0)),
            pl.BlockSpec((None, D_MODEL, tf), lambda b, t, k: (k, 0, 0)),
            pl.BlockSpec((tf, D_MODEL), lambda b, t, k: (k, 0)),
        ]
    out_specs = [_tile_spec(colmajor)]
    out_shape = [jax.ShapeDtypeStruct(_as_stream(x, colmajor).shape, F32)]
    if emit:
        per = FF_TILE // tf
        out_specs += [
            pl.BlockSpec((None, D_MODEL, tf), lambda b, t, k: (k // per, 0, k % per)),
            pl.BlockSpec((None, D_MODEL, tf), lambda b, t, k: (k // per, 0, k % per)),
            pl.BlockSpec((tf, D_MODEL), lambda b, t, k: (k, 0)),
        ]
        out_shape += [
            jax.ShapeDtypeStruct((D_FF // FF_TILE, D_MODEL, FF_TILE), BF16),
            jax.ShapeDtypeStruct((D_FF // FF_TILE, D_MODEL, FF_TILE), BF16),
            jax.ShapeDtypeStruct((D_FF, D_MODEL), BF16),
        ]
    res = pl.pallas_call(
        partial(_out_ffn_kernel, colmajor=colmajor, emit_bf16=emit),
        grid=(bq, s // TOK_TILE, nf),
        in_specs=[
            _tile_spec(colmajor), _tile_spec(False),
            _mod_spec(2), _mod_spec(3), _mod_spec(4), _mod_spec(5),
            _row_spec(), _row_spec(), _row_spec(),
            pl.BlockSpec((None, D_MODEL, D_MODEL), lambda b, t, k: (mix_layer, 0, 0),
                         pipeline_mode=pl.Buffered(1)),
        ] + w_specs,
        out_specs=out_specs,
        out_shape=out_shape,
        scratch_shapes=[
            pltpu.VMEM((TOK_TILE, D_MODEL), BF16),
            pltpu.VMEM((TOK_TILE, D_MODEL), F32),
        ],
        compiler_params=_cparams(("arbitrary", "arbitrary", "arbitrary")),
        name="out_ffn",
    )(_as_stream(x, colmajor), a, mods, mods, mods, mods, g1, g2, g3, w_out, *weights)
    if emit:
        return res[0].reshape(x.shape), tuple(res[1:])
    return res[0].reshape(x.shape)


def _cast_gu_kernel(g_ref, u_ref, go_ref, uo_ref):
    go_ref[...] = g_ref[...].astype(BF16)
    uo_ref[...] = u_ref[...].astype(BF16)


def _cast_gate_up(w_gu, layer):
    nf = D_FF // FF_TILE
    out = jax.ShapeDtypeStruct((nf, D_MODEL, FF_TILE), BF16)
    o_spec = pl.BlockSpec((None, D_MODEL, FF_TILE), lambda k: (k, 0, 0))
    return pl.pallas_call(
        _cast_gu_kernel,
        grid=(nf,),
        in_specs=[
            pl.BlockSpec((None, D_MODEL, FF_TILE), lambda k: (layer, 0, k)),
            pl.BlockSpec((None, D_MODEL, FF_TILE), lambda k: (layer, 0, nf + k)),
        ],
        out_specs=[o_spec, o_spec],
        out_shape=[out, out],
        compiler_params=_cparams(("arbitrary",)),
        name="cast_gate_up",
    )(w_gu, w_gu)


def kernel(x, c, ctx, c_ctx, ada_w, ada_b, norm_gains, gla_w_in, gla_wg2_f, gla_bg_f, gla_wg2_b, gla_bg_b,
           gla_head_gain, gla_w_out, fnet_w_out, ffn_w_gu, ffn_w_down):
    batch = x.shape[0]
    cond = jnp.zeros((ADA_ROWS, D_MODEL), F32).at[:batch].set(c).at[batch].set(c_ctx)
    mods = _ada_rows(cond, ada_w, ada_b)
    ctx_s = ctx.reshape(1, batch * CTX_LEN, D_MODEL)

    n_ch = FNET_GROUP_DIM
    cos_c, sin_c = _dft_tables(n_ch, 16)
    w_ch = (jnp.concatenate([cos_c, sin_c], axis=1) * (n_ch ** -0.5)).astype(BF16)
    cos_t, sin_t = _dft_tables(CTX_LEN, 16)
    ctx_tables = ((cos_t * (CTX_LEN ** -0.5)).astype(BF16), (sin_t * -(CTX_LEN ** -0.5)).astype(BF16))
    n_sub = SEQ // DFT_RADIX
    cos_s, sin_s = _dft_tables(n_sub, 16)
    w_sub = (jnp.block([[cos_s, -sin_s], [sin_s, cos_s]]) * (SEQ ** -0.5)).astype(BF16)
    tw_ang = ((jnp.arange(DFT_RADIX, dtype=jnp.int32)[:, None] * jnp.arange(n_sub, dtype=jnp.int32)[None, :])
              % SEQ).astype(F32) * (2.0 * np.pi / SEQ)
    tw_cos = jnp.broadcast_to(jnp.cos(tw_ang)[:, :, None], (DFT_RADIX, n_sub, LANES))
    tw_sin = jnp.broadcast_to(jnp.sin(tw_ang)[:, :, None], (DFT_RADIX, n_sub, LANES))

    gla_w_out_bf = gla_w_out.astype(BF16)
    fnet_w_out_bf = fnet_w_out.astype(BF16)

    for i in range(DEPTH):
        need_ctx = i < DEPTH - 1
        j = i // 2
        mod_lat = mods[i, :batch].reshape(batch, 1, N_ADA * D_MODEL)
        mod_ctx = mods[i, batch:batch + 1].reshape(1, 1, N_ADA * D_MODEL)
        gains = [norm_gains[i, n].reshape(1, D_MODEL) for n in range(4)]
        colmajor = False
        if i % 2 == 0:
            colmajor = j % 2 == 1
            w_low = lax.optimization_barrier(gla_w_in[j, :, GLA_MAIN_DIM:])
            w_low = jnp.pad(w_low, ((0, 0), (0, LANES - 2 * GLA_GATE_RANK))).astype(BF16)
            w2 = jnp.zeros((2, LANES, GLA_KEY_DIM), F32)
            w2 = w2.at[0, :GLA_GATE_RANK].set(gla_wg2_f[j]).at[1, GLA_GATE_RANK:2 * GLA_GATE_RANK].set(gla_wg2_b[j])
            bg = jnp.stack([gla_bg_f[j], gla_bg_b[j]]).reshape(2, 1, GLA_KEY_DIM)
            qkvr_c, low_c, w_in_bf = _gla_in(ctx_s, mod_ctx, gains[0], gla_w_in, w_low, False, layer=j)
            qkvr_l, low_l = _gla_in(x, mod_lat, gains[0], w_in_bf, w_low, colmajor)
            a_ctx, a_lat = _gla_scan(
                qkvr_c.reshape(batch, CTX_LEN, GLA_MAIN_DIM), low_c.reshape(batch, CTX_LEN, LANES),
                qkvr_l, low_l, w2.astype(BF16), bg, gla_head_gain[j].reshape(1, GLA_HEAD_V))
            w_mix = gla_w_out_bf
        else:
            y1, y2 = _fnet_channels(x, mod_lat, gains[0], w_ch, True)
            a_lat = _fnet_combine(_fnet_sub_dft(w_sub, y1, y2), tw_cos, tw_sin)
            if need_ctx:
                y1, y2 = _fnet_channels(ctx_s, mod_ctx, gains[0], w_ch, False)
                a_ctx = _fnet_tokens(*ctx_tables, y1.reshape(batch, CTX_LEN, D_MODEL),
                                     y2.reshape(batch, CTX_LEN, D_MODEL))
            w_mix = fnet_w_out_bf
        if need_ctx:
            ctx_s, ffn_bf = _out_ffn(ctx_s, a_ctx.reshape(1, batch * CTX_LEN, D_MODEL), mod_ctx,
                                     gains[1], gains[2], gains[3], w_mix, j, (ffn_w_gu, ffn_w_down), False,
                                     layer=i)
        else:
            ffn_bf = (*_cast_gate_up(ffn_w_gu, i), ffn_w_down[i].astype(BF16))
        x = _out_ffn(x, a_lat, mod_lat, gains[1], gains[2], gains[3], w_mix, j, ffn_bf, colmajor)
    return x
```

```python
from functools import partial

import jax
import jax.numpy as jnp
import numpy as np
from jax import lax
from jax.experimental import pallas as pl
from jax.experimental.pallas import tpu as pltpu

D_MODEL = 2048
SEQ = 4096
CTX_LEN = 256
GRID_W = 64
DEPTH = 4
GLA_HEADS = 4
GLA_HEAD_K = 256
GLA_HEAD_V = 512
GLA_KEY_DIM = GLA_HEADS * GLA_HEAD_K
GLA_VALUE_DIM = GLA_HEADS * GLA_HEAD_V
GLA_GATE_RANK = 16
GLA_GATE_TAU = 16.0
GLA_MAIN_DIM = 2 * GLA_KEY_DIM + 2 * GLA_VALUE_DIM
FNET_GROUPS = 4
FNET_GROUP_DIM = D_MODEL // FNET_GROUPS
D_FF = 5632
N_ADA = 6
EPS = 1e-6

LANES = 128
ADA_ROWS = 8
VMEM_LIMIT = 56 * 1024 * 1024

TOK_TILE = 512
IN_TOK_TILE = 1024
FF_TILE = 512
IN_TILE = 1536
IN_TILE_F32 = 768
FF_TILE_F32 = 256
ADA_TILE = 1536
GLA_CHUNK = 128
GLA_UNROLL = 4
DFT_BM = 1024
DFT_BK = 512
DFT_RADIX = 8
COMBINE_ROWS = 128

BF16 = jnp.bfloat16
F32 = jnp.float32


def _cparams(sem):
    return pltpu.CompilerParams(dimension_semantics=sem, vmem_limit_bytes=VMEM_LIMIT)


def _dot(a, b):
    return jnp.dot(a, b, preferred_element_type=F32)


def _rms(x):
    return x * lax.rsqrt(jnp.mean(x * x, axis=-1, keepdims=True) + EPS)


def _modulate(x, gain, shift, scale):
    return (_rms(x) * gain) * (1.0 + scale) + shift


def _ada_kernel(c_ref, w_ref, b_ref, o_ref):
    c = c_ref[...]
    s = (c * jax.nn.sigmoid(c)).astype(BF16)
    o_ref[...] = _dot(s, w_ref[...].astype(BF16)) + b_ref[...]


def _ada_rows(cond, ada_w, ada_b):
    n_out = N_ADA * D_MODEL
    return pl.pallas_call(
        _ada_kernel,
        grid=(DEPTH, n_out // ADA_TILE),
        in_specs=[
            pl.BlockSpec((ADA_ROWS, D_MODEL), lambda i, n: (0, 0)),
            pl.BlockSpec((None, D_MODEL, ADA_TILE), lambda i, n: (i, 0, n)),
            pl.BlockSpec((None, 1, ADA_TILE), lambda i, n: (i, 0, n)),
        ],
        out_specs=pl.BlockSpec((None, ADA_ROWS, ADA_TILE), lambda i, n: (i, 0, n)),
        out_shape=jax.ShapeDtypeStruct((DEPTH, ADA_ROWS, n_out), F32),
        compiler_params=_cparams(("arbitrary", "arbitrary")),
        name="ada_rows",
    )(cond, ada_w, ada_b.reshape(DEPTH, 1, n_out))


def _tile_spec(colmajor, rows=TOK_TILE):
    if colmajor:
        return pl.BlockSpec((None, GRID_W, rows // GRID_W, D_MODEL), lambda b, t, *_: (b, 0, t, 0))
    return pl.BlockSpec((None, rows, D_MODEL), lambda b, t, *_: (b, t, 0))


def _load_tile(x_ref, colmajor):
    if not colmajor:
        return x_ref[...]
    return jnp.concatenate([x_ref[:, c, :] for c in range(x_ref.shape[1])], axis=0)


def _store_tile(o_ref, val, colmajor):
    if not colmajor:
        o_ref[...] = val
        return
    for c in range(o_ref.shape[1]):
        o_ref[:, c, :] = val[c * GRID_W:(c + 1) * GRID_W, :]


def _mod_spec(j):
    return pl.BlockSpec((None, 1, D_MODEL), lambda b, t, *_: (b, 0, j))


def _row_spec():
    return pl.BlockSpec((1, D_MODEL), lambda b, t, *_: (0, 0))


def _as_stream(x, colmajor):
    if colmajor:
        bq, s, d = x.shape
        return x.reshape(bq, GRID_W, s // GRID_W, d)
    return x


def _gla_in_kernel(x_ref, sh_ref, sc_ref, g_ref, w_ref, wlow_ref, o_ref, low_ref, *rest, colmajor, emit_bf16):
    h_ref = rest[-1]
    n = pl.program_id(2)

    @pl.when(n == 0)
    def _():
        h = _modulate(_load_tile(x_ref, colmajor), g_ref[...], sh_ref[...], sc_ref[...]).astype(BF16)
        h_ref[...] = h
        low_ref[...] = _dot(h, wlow_ref[...])

    w = w_ref[...]
    if emit_bf16:
        w = w.astype(BF16)
        rest[0][...] = w
    o_ref[...] = _dot(h_ref[...], w).astype(BF16)


def _gla_in(x, mods, gain, w_in, w_low, colmajor, layer=None):
    bq, s, _ = x.shape
    emit = layer is not None
    tn = IN_TILE_F32 if emit else IN_TILE
    tm = min(IN_TOK_TILE, s)
    if emit:
        assert bq * s == tm, "each weight block must be visited exactly once"
        w_spec = pl.BlockSpec((None, D_MODEL, tn), lambda b, t, n: (layer, 0, n))
    else:
        w_spec = pl.BlockSpec((None, D_MODEL, tn), lambda b, t, n: (n, 0, 0))
    out_specs = [
        pl.BlockSpec((None, tm, tn), lambda b, t, n: (b, t, n)),
        pl.BlockSpec((None, tm, LANES), lambda b, t, n: (b, t, 0)),
    ]
    out_shape = [
        jax.ShapeDtypeStruct((bq, s, GLA_MAIN_DIM), BF16),
        jax.ShapeDtypeStruct((bq, s, LANES), F32),
    ]
    if emit:
        per = IN_TILE // tn
        out_specs.append(pl.BlockSpec((None, D_MODEL, tn), lambda b, t, n: (n // per, 0, n % per)))
        out_shape.append(jax.ShapeDtypeStruct((GLA_MAIN_DIM // IN_TILE, D_MODEL, IN_TILE), BF16))
    return pl.pallas_call(
        partial(_gla_in_kernel, colmajor=colmajor, emit_bf16=emit),
        grid=(bq, s // tm, GLA_MAIN_DIM // tn),
        in_specs=[
            _tile_spec(colmajor, tm), _mod_spec(0), _mod_spec(1), _row_spec(),
            w_spec,
            pl.BlockSpec((D_MODEL, LANES), lambda b, t, n: (0, 0)),
        ],
        out_specs=out_specs,
        out_shape=out_shape,
        scratch_shapes=[pltpu.VMEM((tm, D_MODEL), BF16)],
        compiler_params=_cparams(("arbitrary", "arbitrary", "arbitrary")),
        name="gla_in",
    )(_as_stream(x, colmajor), mods, mods, gain, w_in, w_low)


def _log_gate(z):
    return (jnp.minimum(z, 0.0) - jnp.log(1.0 + jnp.exp(-jnp.abs(z)))) * (1.0 / GLA_GATE_TAU)


def _gla_group(q_ref, k_ref, v_ref, low_ref, w2_ref, bg_ref, tris, masks, s_ref, starts, u):
    c, dk = GLA_CHUNK, GLA_HEAD_K
    span = [pl.ds(pl.multiple_of(starts[d], c), u * c) for d in (0, 1)]
    z = [_dot(low_ref[span[d], :].astype(BF16), w2_ref[d]) + bg_ref[d] for d in (0, 1)]
    g = [_log_gate(zd) for zd in z]
    cums = []
    for d in (0, 1):
        parts = []
        for j in range(u):
            gj = g[d][j * c:(j + 1) * c, :]
            hi = gj.astype(BF16)
            parts += [hi, (gj - hi.astype(F32)).astype(BF16)]
        cum = _dot(tris[d], jnp.concatenate(parts, axis=1))
        cums.append([cum[:, 2 * j * dk:(2 * j + 1) * dk] + cum[:, (2 * j + 1) * dk:(2 * j + 2) * dk]
                     for j in range(u)])
    order = [(d, j if d == 0 else u - 1 - j) for j in range(u) for d in (0, 1)]
    work = {}
    for d, j in order:
        rows = pl.ds(pl.multiple_of(starts[d] + j * c, c), c)
        cum = cums[d][j]
        if d == 1:
            ref, tot = cum[c // 2:c // 2 + 1, :], cum[0:1, :]
        else:
            ref, tot = cum[c // 2 - 1:c // 2, :], cum[c - 1:c, :]
        qf = q_ref[rows, :].astype(F32) * (GLA_HEAD_K ** -0.5)
        kf = k_ref[rows, :].astype(F32)
        work[d, j] = dict(
            rows=rows, tot=tot, v=v_ref[rows, :],
            q_mid=(qf * jnp.exp(cum - ref)).astype(BF16), k_mid=(kf * jnp.exp(ref - cum)).astype(BF16),
            q_dec=(qf * jnp.exp(cum)).astype(BF16), k_end=(kf * jnp.exp(tot - cum)).astype(BF16))
    for key in order:
        w = work[key]
        s = lax.dot_general(w["q_mid"], w["k_mid"], (((1,), (1,)), ((), ())), preferred_element_type=F32)
        w["scores"] = jnp.where(masks[key[0]], s, 0.0).astype(BF16)
    for key in order:
        w = work[key]
        w["o"] = _dot(w["scores"], w["v"])
        w["ds"] = lax.dot_general(w["k_end"], w["v"], (((0,), (0,)), ((), ())), preferred_element_type=F32)
        dec = jnp.broadcast_to(jnp.exp(w["tot"]), (LANES, dk)).T
        w["dec"] = jnp.concatenate([dec] * (GLA_HEAD_V // LANES), axis=1)
    out = []
    state = [s_ref[0], s_ref[1]]
    for d, j in order:
        w = work[d, j]
        out.append((w["rows"], w["o"] + _dot(w["q_dec"], state[d].astype(BF16))))
        state[d] = state[d] * w["dec"] + w["ds"]
    s_ref[0] = state[0]
    s_ref[1] = state[1]
    return out


def _gla_kernel(qc_ref, kc_ref, vc_ref, rc_ref, lowc_ref, ql_ref, kl_ref, vl_ref, rl_ref, lowl_ref,
                w2_ref, bg_ref, hg_ref, oc_ref, ol_ref, s_ref, acc_ref):
    c = GLA_CHUNK
    row = lax.broadcasted_iota(jnp.int32, (c, c), 0)
    col = lax.broadcasted_iota(jnp.int32, (c, c), 1)
    masks = (col <= row, col >= row)
    tris = tuple(m.astype(BF16) for m in masks)
    hg = hg_ref[...]
    s_ref[...] = jnp.zeros_like(s_ref)

    def phase(q_ref, k_ref, v_ref, r_ref, low_ref, o_ref, length):
        u = min(GLA_UNROLL, length // (2 * c))
        n_groups = length // (u * c)
        half = n_groups // 2

        def group(n):
            starts = (n * (u * c), (n_groups - 1 - n) * (u * c))
            return _gla_group(q_ref, k_ref, v_ref, low_ref, w2_ref, bg_ref, tris, masks, s_ref, starts, u)

        def first_visit(n, carry):
            for rows, o in group(n):
                acc_ref[rows, :] = o
            return carry

        def second_visit(n, carry):
            for rows, o in group(n):
                o = _rms(o + acc_ref[rows, :]) * hg
                r = r_ref[rows, :].astype(F32)
                o_ref[rows, :] = (o * (r * jax.nn.sigmoid(r))).astype(BF16)
            return carry

        lax.fori_loop(0, half, first_visit, 0)
        lax.fori_loop(half, n_groups, second_visit, 0)

    phase(qc_ref, kc_ref, vc_ref, rc_ref, lowc_ref, oc_ref, CTX_LEN)
    phase(ql_ref, kl_ref, vl_ref, rl_ref, lowl_ref, ol_ref, SEQ)


def _gla_scan(qkvr_c, low_c, qkvr_l, low_l, w2, bg, head_gain):
    b = qkvr_l.shape[0]
    nk = GLA_KEY_DIM // GLA_HEAD_K

    def stream_specs(length):
        return [
            pl.BlockSpec((None, length, GLA_HEAD_K), lambda i, h: (i, 0, h)),
            pl.BlockSpec((None, length, GLA_HEAD_K), lambda i, h: (i, 0, nk + h)),
            pl.BlockSpec((None, length, GLA_HEAD_V), lambda i, h: (i, 0, nk + h)),
            pl.BlockSpec((None, length, GLA_HEAD_V), lambda i, h: (i, 0, 2 * nk + h)),
            pl.BlockSpec((None, length, LANES), lambda i, h: (i, 0, 0)),
        ]

    def out_spec(length):
        return pl.BlockSpec((None, length, GLA_HEAD_V), lambda i, h: (i, 0, h))

    return pl.pallas_call(
        _gla_kernel,
        grid=(b, GLA_HEADS),
        in_specs=stream_specs(CTX_LEN) + stream_specs(SEQ) + [
            pl.BlockSpec((2, LANES, GLA_HEAD_K), lambda i, h: (0, 0, h)),
            pl.BlockSpec((2, 1, GLA_HEAD_K), lambda i, h: (0, 0, h)),
            pl.BlockSpec((1, GLA_HEAD_V), lambda i, h: (0, 0)),
        ],
        out_specs=[out_spec(CTX_LEN), out_spec(SEQ)],
        out_shape=[
            jax.ShapeDtypeStruct((b, CTX_LEN, GLA_VALUE_DIM), BF16),
            jax.ShapeDtypeStruct((b, SEQ, GLA_VALUE_DIM), BF16),
        ],
        scratch_shapes=[
            pltpu.VMEM((2, GLA_HEAD_K, GLA_HEAD_V), F32),
            pltpu.VMEM((SEQ, GLA_HEAD_V), F32),
        ],
        compiler_params=_cparams(("arbitrary", "arbitrary")),
        name="gla_scan",
    )(qkvr_c, qkvr_c, qkvr_c, qkvr_c, low_c, qkvr_l, qkvr_l, qkvr_l, qkvr_l, low_l, w2, bg, head_gain)


def _fnet_ch_kernel(x_ref, sh_ref, sc_ref, g_ref, w_ref, y1_ref, y2_ref, *, by_phase):
    if by_phase:
        x = jnp.concatenate([x_ref[:, s, :] for s in range(DFT_RADIX)], axis=0)
    else:
        x = x_ref[...]
    h = _modulate(x, g_ref[...], sh_ref[...], sc_ref[...]).astype(BF16)
    w = w_ref[...]
    gd = FNET_GROUP_DIM
    rows = TOK_TILE // DFT_RADIX
    for g in range(FNET_GROUPS):
        r = _dot(h[:, g * gd:(g + 1) * gd], w).astype(BF16)
        cols = slice(g * gd, (g + 1) * gd)
        if by_phase:
            for s in range(DFT_RADIX):
                y1_ref[s, :, cols] = r[s * rows:(s + 1) * rows, :gd]
                y2_ref[s, :, cols] = r[s * rows:(s + 1) * rows, gd:]
        else:
            y1_ref[:, cols] = r[:, :gd]
            y2_ref[:, cols] = r[:, gd:]


def _fnet_channels(x, mods, gain, w_ch, by_phase):
    bq, s, _ = x.shape
    if by_phase:
        rows = TOK_TILE // DFT_RADIX
        x = x.reshape(bq, s // DFT_RADIX, DFT_RADIX, D_MODEL)
        x_spec = pl.BlockSpec((None, rows, DFT_RADIX, D_MODEL), lambda b, t: (b, t, 0, 0))
        y_spec = pl.BlockSpec((None, DFT_RADIX, rows, D_MODEL), lambda b, t: (b, 0, t, 0))
        out = jax.ShapeDtypeStruct((bq, DFT_RADIX, s // DFT_RADIX, D_MODEL), BF16)
    else:
        x_spec = y_spec = _tile_spec(False)
        out = jax.ShapeDtypeStruct((bq, s, D_MODEL), BF16)
    return pl.pallas_call(
        partial(_fnet_ch_kernel, by_phase=by_phase),
        grid=(bq, s // TOK_TILE),
        in_specs=[
            x_spec, _mod_spec(0), _mod_spec(1), _row_spec(),
            pl.BlockSpec((FNET_GROUP_DIM, 2 * FNET_GROUP_DIM), lambda b, t: (0, 0)),
        ],
        out_specs=[y_spec, y_spec],
        out_shape=[out, out],
        compiler_params=_cparams(("arbitrary", "arbitrary")),
        name="fnet_channels",
    )(x, mods, mods, gain, w_ch)


def _fnet_sub_kernel(w_ref, y1_ref, y2_ref, o_ref):
    n = y1_ref.shape[0]
    o_ref[...] = (_dot(w_ref[:, :n], y1_ref[...]) + _dot(w_ref[:, n:], y2_ref[...])).astype(BF16)


def _fnet_sub_dft(w_sub, y1, y2):
    b, radix, n, _ = y1.shape
    y_spec = pl.BlockSpec((None, None, n, D_MODEL), lambda i, s: (i, s, 0, 0))
    return pl.pallas_call(
        _fnet_sub_kernel,
        grid=(b, radix),
        in_specs=[pl.BlockSpec((2 * n, 2 * n), lambda i, s: (0, 0)), y_spec, y_spec],
        out_specs=pl.BlockSpec((None, None, 2 * n, D_MODEL), lambda i, s: (i, s, 0, 0)),
        out_shape=jax.ShapeDtypeStruct((b, radix, 2 * n, D_MODEL), BF16),
        compiler_params=_cparams(("arbitrary", "arbitrary")),
        name="fnet_sub_dft",
    )(w_sub, y1, y2)


def _fft8_real(xs):
    r2 = float(np.sqrt(0.5))

    def add(a, b):
        return a[0] + b[0], a[1] + b[1]

    def sub(a, b):
        return a[0] - b[0], a[1] - b[1]

    def mul_neg_i(a):
        return a[1], -a[0]

    u = [add(xs[k], xs[k + 4]) for k in range(4)]
    d = [sub(xs[k], xs[k + 4]) for k in range(4)]
    v = [d[0],
         ((d[1][0] + d[1][1]) * r2, (d[1][1] - d[1][0]) * r2),
         mul_neg_i(d[2]),
         ((d[3][1] - d[3][0]) * r2, -(d[3][0] + d[3][1]) * r2)]

    def fft4_real(y):
        p0, p1 = add(y[0], y[2]), add(y[1], y[3])
        q0, q1 = sub(y[0], y[2]), mul_neg_i(sub(y[1], y[3]))
        return [p0[0] + p1[0], q0[0] + q1[0], p0[0] - p1[0], q0[0] - q1[0]]

    even, odd = fft4_real(u), fft4_real(v)
    return [even[0], odd[0], even[1], odd[1], even[2], odd[2], even[3], odd[3]]


def _fnet_combine_kernel(a_ref, b_ref, ca_ref, sa_ref, o_ref):
    bp = a_ref.shape[1]
    sub_rows = 16

    def body(rg, carry):
        rows = pl.ds(pl.multiple_of(rg * sub_rows, sub_rows), sub_rows)
        ca = [ca_ref[s, rows, :] for s in range(DFT_RADIX)]
        sa = [sa_ref[s, rows, :] for s in range(DFT_RADIX)]
        for lc in range(D_MODEL // LANES):
            lanes = slice(lc * LANES, (lc + 1) * LANES)
            xs = []
            for s in range(DFT_RADIX):
                a = a_ref[s, rows, lanes].astype(F32)
                b = b_ref[s, rows, lanes].astype(F32)
                if s == 0:
                    xs.append((a, -b))
                else:
                    xs.append((ca[s] * a - sa[s] * b, -(sa[s] * a + ca[s] * b)))
            for q, z in enumerate(_fft8_real(xs)):
                o_ref[q, rows, lanes] = z.astype(BF16)
        return carry

    lax.fori_loop(0, bp // sub_rows, body, 0)


def _fnet_combine(ab, ca, sa):
    b, radix, n2, _ = ab.shape
    n = n2 // 2
    bp = COMBINE_ROWS
    ab_spec = lambda off: pl.BlockSpec((None, radix, bp, D_MODEL), lambda i, p: (i, 0, off + p, 0))
    tw_spec = pl.BlockSpec((radix, bp, LANES), lambda i, p: (0, p, 0))
    z = pl.pallas_call(
        _fnet_combine_kernel,
        grid=(b, n // bp),
        in_specs=[ab_spec(0), ab_spec(n // bp), tw_spec, tw_spec],
        out_specs=pl.BlockSpec((None, radix, bp, D_MODEL), lambda i, p: (i, 0, p, 0)),
        out_shape=jax.ShapeDtypeStruct((b, radix, n, D_MODEL), BF16),
        compiler_params=_cparams(("arbitrary", "arbitrary")),
        name="fnet_combine",
    )(ab, ab, ca, sa)
    return z.reshape(b, radix * n, D_MODEL)


def _fnet_tok_kernel(gc_ref, gs_ref, y1_ref, y2_ref, o_ref, acc_ref):
    k = pl.program_id(2)

    @pl.when(k == 0)
    def _():
        acc_ref[...] = jnp.zeros_like(acc_ref)

    acc_ref[...] += _dot(gc_ref[...], y1_ref[...]) + _dot(gs_ref[...], y2_ref[...])

    @pl.when(k == pl.num_programs(2) - 1)
    def _():
        o_ref[...] = acc_ref[...].astype(BF16)


def _fnet_tokens(gc, gs, y1, y2):
    b, length, _ = y1.shape
    bm, bk = min(DFT_BM, length), min(DFT_BK, length)
    y_spec = pl.BlockSpec((None, bk, D_MODEL), lambda i, m, k: (i, k, 0))
    g_spec = pl.BlockSpec((bm, bk), lambda i, m, k: (m, k))
    return pl.pallas_call(
        _fnet_tok_kernel,
        grid=(b, length // bm, length // bk),
        in_specs=[g_spec, g_spec, y_spec, y_spec],
        out_specs=pl.BlockSpec((None, bm, D_MODEL), lambda i, m, k: (i, m, 0)),
        out_shape=jax.ShapeDtypeStruct((b, length, D_MODEL), BF16),
        scratch_shapes=[pltpu.VMEM((bm, D_MODEL), F32)],
        compiler_params=_cparams(("arbitrary", "arbitrary", "arbitrary")),
        name="fnet_tokens",
    )(gc, gs, y1, y2)


def _dft_tables(n, split):
    lp = jnp.arange(n, dtype=jnp.int32)[:, None]
    hi = jnp.arange(n // split, dtype=jnp.int32)[None, :]
    lo = jnp.arange(split, dtype=jnp.int32)[None, :]
    ang_hi = ((lp * hi) % (n // split)).astype(F32) * (2.0 * np.pi * split / n)
    ang_lo = ((lp * lo) % n).astype(F32) * (2.0 * np.pi / n)
    ch, sh_, cl, sl = jnp.cos(ang_hi), jnp.sin(ang_hi), jnp.cos(ang_lo), jnp.sin(ang_lo)
    cos = ch[:, :, None] * cl[:, None, :] - sh_[:, :, None] * sl[:, None, :]
    sin = sh_[:, :, None] * cl[:, None, :] + ch[:, :, None] * sl[:, None, :]
    return cos.reshape(n, n), sin.reshape(n, n)


def _out_ffn_kernel(x_ref, a_ref, gtm_ref, shf_ref, scf_ref, gtf_ref, g1_ref, g2_ref, g3_ref,
                    wout_ref, wg_ref, wu_ref, wd_ref, o_ref, *rest, colmajor, emit_bf16):
    h_ref, acc_ref = rest[-2:]
    k = pl.program_id(2)

    x1_ref = o_ref.reshape(TOK_TILE, D_MODEL) if colmajor else o_ref

    @pl.when(k == 0)
    def _():
        y = _dot(a_ref[...], wout_ref[...])
        x1 = _load_tile(x_ref, colmajor) + gtm_ref[...] * (_rms(y) * g1_ref[...])
        x1_ref[...] = x1
        h_ref[...] = _modulate(x1, g2_ref[...], shf_ref[...], scf_ref[...]).astype(BF16)
        acc_ref[...] = jnp.zeros_like(acc_ref)

    wg, wu, wd = wg_ref[...], wu_ref[...], wd_ref[...]
    if emit_bf16:
        wg, wu, wd = wg.astype(BF16), wu.astype(BF16), wd.astype(BF16)
        rest[0][...] = wg
        rest[1][...] = wu
        rest[2][...] = wd
    h = h_ref[...]
    gate = _dot(h, wg)
    up = _dot(h, wu)
    act = (gate * jax.nn.sigmoid(gate) * up).astype(BF16)
    acc_ref[...] += _dot(act, wd)

    @pl.when(k == pl.num_programs(2) - 1)
    def _():
        out = x1_ref[...] + gtf_ref[...] * (_rms(acc_ref[...]) * g3_ref[...])
        _store_tile(o_ref, out, colmajor)


def _out_ffn(x, a, mods, g1, g2, g3, w_out, mix_layer, ffn_w, colmajor, layer=None):
    bq, s, _ = x.shape
    emit = layer is not None
    tf = FF_TILE_F32 if emit else FF_TILE
    nf = D_FF // tf
    if emit:
        assert bq * s == TOK_TILE, "each weight block must be visited exactly once"
        w_gu, w_down = ffn_w
        weights = (w_gu, w_gu, w_down)
        w_specs = [
            pl.BlockSpec((None, D_MODEL, tf), lambda b, t, k: (layer, 0, k)),
            pl.BlockSpec((None, D_MODEL, tf), lambda b, t, k: (layer, 0, nf + k)),
            pl.BlockSpec((None, tf, D_MODEL), lambda b, t, k: (layer, k, 0)),
        ]
    else:
        weights = ffn_w
        w_specs = [
            pl.BlockSpec((None, D_MODEL, tf), lambda b, t, k: (k, 0, 0)),
            pl.BlockSpec((None, D_MODEL, tf), lambda b, t, k: (k, 0, 0)),
            pl.BlockSpec((tf, D_MODEL), lambda b, t, k: (k, 0)),
        ]
    out_specs = [_tile_spec(colmajor)]
    out_shape = [jax.ShapeDtypeStruct(_as_stream(x, colmajor).shape, F32)]
    if emit:
        per = FF_TILE // tf
        out_specs += [
            pl.BlockSpec((None, D_MODEL, tf), lambda b, t, k: (k // per, 0, k % per)),
            pl.BlockSpec((None, D_MODEL, tf), lambda b, t, k: (k // per, 0, k % per)),
            pl.BlockSpec((tf, D_MODEL), lambda b, t, k: (k, 0)),
        ]
        out_shape += [
            jax.ShapeDtypeStruct((D_FF // FF_TILE, D_MODEL, FF_TILE), BF16),
            jax.ShapeDtypeStruct((D_FF // FF_TILE, D_MODEL, FF_TILE), BF16),
            jax.ShapeDtypeStruct((D_FF, D_MODEL), BF16),
        ]
    res = pl.pallas_call(
        partial(_out_ffn_kernel, colmajor=colmajor, emit_bf16=emit),
        grid=(bq, s // TOK_TILE, nf),
        in_specs=[
            _tile_spec(colmajor), _tile_spec(False),
            _mod_spec(2), _mod_spec(3), _mod_spec(4), _mod_spec(5),
            _row_spec(), _row_spec(), _row_spec(),
            pl.BlockSpec((None, D_MODEL, D_MODEL), lambda b, t, k: (mix_layer, 0, 0),
                         pipeline_mode=pl.Buffered(1)),
        ] + w_specs,
        out_specs=out_specs,
        out_shape=out_shape,
        scratch_shapes=[
            pltpu.VMEM((TOK_TILE, D_MODEL), BF16),
            pltpu.VMEM((TOK_TILE, D_MODEL), F32),
        ],
        compiler_params=_cparams(("arbitrary", "arbitrary", "arbitrary")),
        name="out_ffn",
    )(_as_stream(x, colmajor), a, mods, mods, mods, mods, g1, g2, g3, w_out, *weights)
    if emit:
        return res[0].reshape(x.shape), tuple(res[1:])
    return res[0].reshape(x.shape)


def _cast_gu_kernel(g_ref, u_ref, go_ref, uo_ref):
    go_ref[...] = g_ref[...].astype(BF16)
    uo_ref[...] = u_ref[...].astype(BF16)


def _cast_gate_up(w_gu, layer):
    nf = D_FF // FF_TILE
    out = jax.ShapeDtypeStruct((nf, D_MODEL, FF_TILE), BF16)
    o_spec = pl.BlockSpec((None, D_MODEL, FF_TILE), lambda k: (k, 0, 0))
    return pl.pallas_call(
        _cast_gu_kernel,
        grid=(nf,),
        in_specs=[
            pl.BlockSpec((None, D_MODEL, FF_TILE), lambda k: (layer, 0, k)),
            pl.BlockSpec((None, D_MODEL, FF_TILE), lambda k: (layer, 0, nf + k)),
        ],
        out_specs=[o_spec, o_spec],
        out_shape=[out, out],
        compiler_params=_cparams(("arbitrary",)),
        name="cast_gate_up",
    )(w_gu, w_gu)


def kernel(x, c, ctx, c_ctx, ada_w, ada_b, norm_gains, gla_w_in, gla_wg2_f, gla_bg_f, gla_wg2_b, gla_bg_b,
           gla_head_gain, gla_w_out, fnet_w_out, ffn_w_gu, ffn_w_down):
    batch = x.shape[0]
    cond = jnp.zeros((ADA_ROWS, D_MODEL), F32).at[:batch].set(c).at[batch].set(c_ctx)
    mods = _ada_rows(cond, ada_w, ada_b)
    ctx_s = ctx.reshape(1, batch * CTX_LEN, D_MODEL)

    n_ch = FNET_GROUP_DIM
    cos_c, sin_c = _dft_tables(n_ch, 16)
    w_ch = (jnp.concatenate([cos_c, sin_c], axis=1) * (n_ch ** -0.5)).astype(BF16)
    cos_t, sin_t = _dft_tables(CTX_LEN, 16)
    ctx_tables = ((cos_t * (CTX_LEN ** -0.5)).astype(BF16), (sin_t * -(CTX_LEN ** -0.5)).astype(BF16))
    n_sub = SEQ // DFT_RADIX
    cos_s, sin_s = _dft_tables(n_sub, 16)
    w_sub = (jnp.block([[cos_s, -sin_s], [sin_s, cos_s]]) * (SEQ ** -0.5)).astype(BF16)
    tw_ang = ((jnp.arange(DFT_RADIX, dtype=jnp.int32)[:, None] * jnp.arange(n_sub, dtype=jnp.int32)[None, :])
              % SEQ).astype(F32) * (2.0 * np.pi / SEQ)
    tw_cos = jnp.broadcast_to(jnp.cos(tw_ang)[:, :, None], (DFT_RADIX, n_sub, LANES))
    tw_sin = jnp.broadcast_to(jnp.sin(tw_ang)[:, :, None], (DFT_RADIX, n_sub, LANES))

    gla_w_out_bf = gla_w_out.astype(BF16)
    fnet_w_out_bf = fnet_w_out.astype(BF16)

    for i in range(DEPTH):
        need_ctx = i < DEPTH - 1
        j = i // 2
        mod_lat = mods[i, :batch].reshape(batch, 1, N_ADA * D_MODEL)
        mod_ctx = mods[i, batch:batch + 1].reshape(1, 1, N_ADA * D_MODEL)
        gains = [norm_gains[i, n].reshape(1, D_MODEL) for n in range(4)]
        colmajor = False
        if i % 2 == 0:
            colmajor = j % 2 == 1
            w_low = lax.optimization_barrier(gla_w_in[j, :, GLA_MAIN_DIM:])
            w_low = jnp.pad(w_low, ((0, 0), (0, LANES - 2 * GLA_GATE_RANK))).astype(BF16)
            w2 = jnp.zeros((2, LANES, GLA_KEY_DIM), F32)
            w2 = w2.at[0, :GLA_GATE_RANK].set(gla_wg2_f[j]).at[1, GLA_GATE_RANK:2 * GLA_GATE_RANK].set(gla_wg2_b[j])
            bg = jnp.stack([gla_bg_f[j], gla_bg_b[j]]).reshape(2, 1, GLA_KEY_DIM)
            qkvr_c, low_c, w_in_bf = _gla_in(ctx_s, mod_ctx, gains[0], gla_w_in, w_low, False, layer=j)
            qkvr_l, low_l = _gla_in(x, mod_lat, gains[0], w_in_bf, w_low, colmajor)
            a_ctx, a_lat = _gla_scan(
                qkvr_c.reshape(batch, CTX_LEN, GLA_MAIN_DIM), low_c.reshape(batch, CTX_LEN, LANES),
                qkvr_l, low_l, w2.astype(BF16), bg, gla_head_gain[j].reshape(1, GLA_HEAD_V))
            w_mix = gla_w_out_bf
        else:
            y1, y2 = _fnet_channels(x, mod_lat, gains[0], w_ch, True)
            a_lat = _fnet_combine(_fnet_sub_dft(w_sub, y1, y2), tw_cos, tw_sin)
            if need_ctx:
                y1, y2 = _fnet_channels(ctx_s, mod_ctx, gains[0], w_ch, False)
                a_ctx = _fnet_tokens(*ctx_tables, y1.reshape(batch, CTX_LEN, D_MODEL),
                                     y2.reshape(batch, CTX_LEN, D_MODEL))
            w_mix = fnet_w_out_bf
        if need_ctx:
            ctx_s, ffn_bf = _out_ffn(ctx_s, a_ctx.reshape(1, batch * CTX_LEN, D_MODEL), mod_ctx,
                                     gains[1], gains[2], gains[3], w_mix, j, (ffn_w_gu, ffn_w_down), False,
                                     layer=i)
        else:
            ffn_bf = (*_cast_gate_up(ffn_w_gu, i), ffn_w_down[i].astype(BF16))
        x = _out_ffn(x, a_lat, mod_lat, gains[1], gains[2], gains[3], w_mix, j, ffn_bf, colmajor)
    return x
```

```python
from functools import partial

import jax
import jax.numpy as jnp
import numpy as np
from jax import lax
from jax.experimental import pallas as pl
from jax.experimental.pallas import tpu as pltpu

D_MODEL = 2048
SEQ = 4096
CTX_LEN = 256
GRID_W = 64
DEPTH = 4
GLA_HEADS = 4
GLA_HEAD_K = 256
GLA_HEAD_V = 512
GLA_KEY_DIM = GLA_HEADS * GLA_HEAD_K
GLA_VALUE_DIM = GLA_HEADS * GLA_HEAD_V
GLA_GATE_RANK = 16
GLA_GATE_TAU = 16.0
GLA_MAIN_DIM = 2 * GLA_KEY_DIM + 2 * GLA_VALUE_DIM
FNET_GROUPS = 4
FNET_GROUP_DIM = D_MODEL // FNET_GROUPS
D_FF = 5632
N_ADA = 6
EPS = 1e-6

LANES = 128
ADA_ROWS = 8
VMEM_LIMIT = 56 * 1024 * 1024

TOK_TILE = 512
IN_TOK_TILE = 1024
FF_TILE = 512
IN_TILE = 1536
IN_TILE_F32 = 768
FF_TILE_F32 = 256
ADA_TILE = 1536
GLA_CHUNK = 128
GLA_UNROLL = 4
DFT_BM = 1024
DFT_BK = 512
DFT_RADIX = 8
COMBINE_ROWS = 128

BF16 = jnp.bfloat16
F32 = jnp.float32


def _cparams(sem):
    return pltpu.CompilerParams(dimension_semantics=sem, vmem_limit_bytes=VMEM_LIMIT)


def _dot(a, b):
    return jnp.dot(a, b, preferred_element_type=F32)


def _rms(x):
    return x * lax.rsqrt(jnp.mean(x * x, axis=-1, keepdims=True) + EPS)


def _modulate(x, gain, shift, scale):
    return (_rms(x) * gain) * (1.0 + scale) + shift


def _ada_kernel(c_ref, w_ref, b_ref, o_ref):
    c = c_ref[...]
    s = (c * jax.nn.sigmoid(c)).astype(BF16)
    o_ref[...] = _dot(s, w_ref[...].astype(BF16)) + b_ref[...]


def _ada_rows(cond, ada_w, ada_b):
    n_out = N_ADA * D_MODEL
    return pl.pallas_call(
        _ada_kernel,
        grid=(DEPTH, n_out // ADA_TILE),
        in_specs=[
            pl.BlockSpec((ADA_ROWS, D_MODEL), lambda i, n: (0, 0)),
            pl.BlockSpec((None, D_MODEL, ADA_TILE), lambda i, n: (i, 0, n)),
            pl.BlockSpec((None, 1, ADA_TILE), lambda i, n: (i, 0, n)),
        ],
        out_specs=pl.BlockSpec((None, ADA_ROWS, ADA_TILE), lambda i, n: (i, 0, n)),
        out_shape=jax.ShapeDtypeStruct((DEPTH, ADA_ROWS, n_out), F32),
        compiler_params=_cparams(("arbitrary", "arbitrary")),
        name="ada_rows",
    )(cond, ada_w, ada_b.reshape(DEPTH, 1, n_out))


def _tile_spec(colmajor, rows=TOK_TILE):
    if colmajor:
        return pl.BlockSpec((None, GRID_W, rows // GRID_W, D_MODEL), lambda b, t, *_: (b, 0, t, 0))
    return pl.BlockSpec((None, rows, D_MODEL), lambda b, t, *_: (b, t, 0))


def _load_tile(x_ref, colmajor):
    if not colmajor:
        return x_ref[...]
    return jnp.concatenate([x_ref[:, c, :] for c in range(x_ref.shape[1])], axis=0)


def _store_tile(o_ref, val, colmajor):
    if not colmajor:
        o_ref[...] = val
        return
    for c in range(o_ref.shape[1]):
        o_ref[:, c, :] = val[c * GRID_W:(c + 1) * GRID_W, :]


def _mod_spec(j):
    return pl.BlockSpec((None, 1, D_MODEL), lambda b, t, *_: (b, 0, j))


def _row_spec():
    return pl.BlockSpec((1, D_MODEL), lambda b, t, *_: (0, 0))


def _as_stream(x, colmajor):
    if colmajor:
        bq, s, d = x.shape
        return x.reshape(bq, GRID_W, s // GRID_W, d)
    return x


def _gla_in_kernel(x_ref, sh_ref, sc_ref, g_ref, w_ref, wlow_ref, o_ref, low_ref, *rest, colmajor, emit_bf16):
    h_ref = rest[-1]
    n = pl.program_id(2)

    @pl.when(n == 0)
    def _():
        h = _modulate(_load_tile(x_ref, colmajor), g_ref[...], sh_ref[...], sc_ref[...]).astype(BF16)
        h_ref[...] = h
        low_ref[...] = _dot(h, wlow_ref[...])

    w = w_ref[...]
    if emit_bf16:
        w = w.T.astype(BF16)
        rest[0][...] = w
    o_ref[...] = _dot(h_ref[...], w).astype(BF16)


def _gla_in(x, mods, gain, w_in, w_low, colmajor, layer=None):
    bq, s, _ = x.shape
    emit = layer is not None
    tn = IN_TILE_F32 if emit else IN_TILE
    tm = min(IN_TOK_TILE, s)
    if emit:
        assert bq * s == tm, "each weight block must be visited exactly once"
        w_spec = pl.BlockSpec((None, tn, D_MODEL), lambda b, t, n: (layer, n, 0))
    else:
        w_spec = pl.BlockSpec((None, D_MODEL, tn), lambda b, t, n: (n, 0, 0))
    out_specs = [
        pl.BlockSpec((None, tm, tn), lambda b, t, n: (b, t, n)),
        pl.BlockSpec((None, tm, LANES), lambda b, t, n: (b, t, 0)),
    ]
    out_shape = [
        jax.ShapeDtypeStruct((bq, s, GLA_MAIN_DIM), BF16),
        jax.ShapeDtypeStruct((bq, s, LANES), F32),
    ]
    if emit:
        per = IN_TILE // tn
        out_specs.append(pl.BlockSpec((None, D_MODEL, tn), lambda b, t, n: (n // per, 0, n % per)))
        out_shape.append(jax.ShapeDtypeStruct((GLA_MAIN_DIM // IN_TILE, D_MODEL, IN_TILE), BF16))
    return pl.pallas_call(
        partial(_gla_in_kernel, colmajor=colmajor, emit_bf16=emit),
        grid=(bq, s // tm, GLA_MAIN_DIM // tn),
        in_specs=[
            _tile_spec(colmajor, tm), _mod_spec(0), _mod_spec(1), _row_spec(),
            w_spec,
            pl.BlockSpec((D_MODEL, LANES), lambda b, t, n: (0, 0)),
        ],
        out_specs=out_specs,
        out_shape=out_shape,
        scratch_shapes=[pltpu.VMEM((tm, D_MODEL), BF16)],
        compiler_params=_cparams(("arbitrary", "arbitrary", "arbitrary")),
        name="gla_in",
    )(_as_stream(x, colmajor), mods, mods, gain, w_in, w_low)


def _log_gate(z):
    return (jnp.minimum(z, 0.0) - jnp.log(1.0 + jnp.exp(-jnp.abs(z)))) * (1.0 / GLA_GATE_TAU)


def _gla_group(q_ref, k_ref, v_ref, low_ref, w2_ref, bg_ref, tris, masks, s_ref, starts, u):
    c, dk = GLA_CHUNK, GLA_HEAD_K
    span = [pl.ds(pl.multiple_of(starts[d], c), u * c) for d in (0, 1)]
    z = [_dot(low_ref[span[d], :].astype(BF16), w2_ref[d]) + bg_ref[d] for d in (0, 1)]
    g = [_log_gate(zd) for zd in z]
    cums = []
    for d in (0, 1):
        parts = []
        for j in range(u):
            gj = g[d][j * c:(j + 1) * c, :]
            hi = gj.astype(BF16)
            parts += [hi, (gj - hi.astype(F32)).astype(BF16)]
        cum = _dot(tris[d], jnp.concatenate(parts, axis=1))
        cums.append([cum[:, 2 * j * dk:(2 * j + 1) * dk] + cum[:, (2 * j + 1) * dk:(2 * j + 2) * dk]
                     for j in range(u)])
    order = [(d, j if d == 0 else u - 1 - j) for j in range(u) for d in (0, 1)]
    work = {}
    for d, j in order:
        rows = pl.ds(pl.multiple_of(starts[d] + j * c, c), c)
        cum = cums[d][j]
        if d == 1:
            ref, tot = cum[c // 2:c // 2 + 1, :], cum[0:1, :]
        else:
            ref, tot = cum[c // 2 - 1:c // 2, :], cum[c - 1:c, :]
        qf = q_ref[rows, :].astype(F32) * (GLA_HEAD_K ** -0.5)
        kf = k_ref[rows, :].astype(F32)
        work[d, j] = dict(
            rows=rows, tot=tot, v=v_ref[rows, :],
            q_mid=(qf * jnp.exp(cum - ref)).astype(BF16), k_mid=(kf * jnp.exp(ref - cum)).astype(BF16),
            q_dec=(qf * jnp.exp(cum)).astype(BF16), k_end=(kf * jnp.exp(tot - cum)).astype(BF16))
    for key in order:
        w = work[key]
        s = lax.dot_general(w["q_mid"], w["k_mid"], (((1,), (1,)), ((), ())), preferred_element_type=F32)
        w["scores"] = jnp.where(masks[key[0]], s, 0.0).astype(BF16)
    for key in order:
        w = work[key]
        w["o"] = _dot(w["scores"], w["v"])
        w["ds"] = lax.dot_general(w["k_end"], w["v"], (((0,), (0,)), ((), ())), preferred_element_type=F32)
        dec = jnp.broadcast_to(jnp.exp(w["tot"]), (LANES, dk)).T
        w["dec"] = jnp.concatenate([dec] * (GLA_HEAD_V // LANES), axis=1)
    out = []
    state = [s_ref[0], s_ref[1]]
    for d, j in order:
        w = work[d, j]
        out.append((w["rows"], w["o"] + _dot(w["q_dec"], state[d].astype(BF16))))
        state[d] = state[d] * w["dec"] + w["ds"]
    s_ref[0] = state[0]
    s_ref[1] = state[1]
    return out


def _gla_kernel(qc_ref, kc_ref, vc_ref, rc_ref, lowc_ref, ql_ref, kl_ref, vl_ref, rl_ref, lowl_ref,
                w2_ref, bg_ref, hg_ref, oc_ref, ol_ref, s_ref, acc_ref):
    c = GLA_CHUNK
    row = lax.broadcasted_iota(jnp.int32, (c, c), 0)
    col = lax.broadcasted_iota(jnp.int32, (c, c), 1)
    masks = (col <= row, col >= row)
    tris = tuple(m.astype(BF16) for m in masks)
    hg = hg_ref[...]
    s_ref[...] = jnp.zeros_like(s_ref)

    def phase(q_ref, k_ref, v_ref, r_ref, low_ref, o_ref, length):
        u = min(GLA_UNROLL, length // (2 * c))
        n_groups = length // (u * c)
        half = n_groups // 2

        def group(n):
            starts = (n * (u * c), (n_groups - 1 - n) * (u * c))
            return _gla_group(q_ref, k_ref, v_ref, low_ref, w2_ref, bg_ref, tris, masks, s_ref, starts, u)

        def first_visit(n, carry):
            for rows, o in group(n):
                acc_ref[rows, :] = o
            return carry

        def second_visit(n, carry):
            for rows, o in group(n):
                o = _rms(o + acc_ref[rows, :]) * hg
                r = r_ref[rows, :].astype(F32)
                o_ref[rows, :] = (o * (r * jax.nn.sigmoid(r))).astype(BF16)
            return carry

        lax.fori_loop(0, half, first_visit, 0)
        lax.fori_loop(half, n_groups, second_visit, 0)

    phase(qc_ref, kc_ref, vc_ref, rc_ref, lowc_ref, oc_ref, CTX_LEN)
    phase(ql_ref, kl_ref, vl_ref, rl_ref, lowl_ref, ol_ref, SEQ)


def _gla_scan(qkvr_c, low_c, qkvr_l, low_l, w2, bg, head_gain):
    b = qkvr_l.shape[0]
    nk = GLA_KEY_DIM // GLA_HEAD_K

    def stream_specs(length):
        return [
            pl.BlockSpec((None, length, GLA_HEAD_K), lambda i, h: (i, 0, h)),
            pl.BlockSpec((None, length, GLA_HEAD_K), lambda i, h: (i, 0, nk + h)),
            pl.BlockSpec((None, length, GLA_HEAD_V), lambda i, h: (i, 0, nk + h)),
            pl.BlockSpec((None, length, GLA_HEAD_V), lambda i, h: (i, 0, 2 * nk + h)),
            pl.BlockSpec((None, length, LANES), lambda i, h: (i, 0, 0)),
        ]

    def out_spec(length):
        return pl.BlockSpec((None, length, GLA_HEAD_V), lambda i, h: (i, 0, h))

    return pl.pallas_call(
        _gla_kernel,
        grid=(b, GLA_HEADS),
        in_specs=stream_specs(CTX_LEN) + stream_specs(SEQ) + [
            pl.BlockSpec((2, LANES, GLA_HEAD_K), lambda i, h: (0, 0, h)),
            pl.BlockSpec((2, 1, GLA_HEAD_K), lambda i, h: (0, 0, h)),
            pl.BlockSpec((1, GLA_HEAD_V), lambda i, h: (0, 0)),
        ],
        out_specs=[out_spec(CTX_LEN), out_spec(SEQ)],
        out_shape=[
            jax.ShapeDtypeStruct((b, CTX_LEN, GLA_VALUE_DIM), BF16),
            jax.ShapeDtypeStruct((b, SEQ, GLA_VALUE_DIM), BF16),
        ],
        scratch_shapes=[
            pltpu.VMEM((2, GLA_HEAD_K, GLA_HEAD_V), F32),
            pltpu.VMEM((SEQ, GLA_HEAD_V), F32),
        ],
        compiler_params=_cparams(("arbitrary", "arbitrary")),
        name="gla_scan",
    )(qkvr_c, qkvr_c, qkvr_c, qkvr_c, low_c, qkvr_l, qkvr_l, qkvr_l, qkvr_l, low_l, w2, bg, head_gain)


def _fnet_ch_kernel(x_ref, sh_ref, sc_ref, g_ref, w_ref, y1_ref, y2_ref, *, by_phase):
    if by_phase:
        x = jnp.concatenate([x_ref[:, s, :] for s in range(DFT_RADIX)], axis=0)
    else:
        x = x_ref[...]
    h = _modulate(x, g_ref[...], sh_ref[...], sc_ref[...]).astype(BF16)
    w = w_ref[...]
    gd = FNET_GROUP_DIM
    rows = TOK_TILE // DFT_RADIX
    for g in range(FNET_GROUPS):
        r = _dot(h[:, g * gd:(g + 1) * gd], w).astype(BF16)
        cols = slice(g * gd, (g + 1) * gd)
        if by_phase:
            for s in range(DFT_RADIX):
                y1_ref[s, :, cols] = r[s * rows:(s + 1) * rows, :gd]
                y2_ref[s, :, cols] = r[s * rows:(s + 1) * rows, gd:]
        else:
            y1_ref[:, cols] = r[:, :gd]
            y2_ref[:, cols] = r[:, gd:]


def _fnet_channels(x, mods, gain, w_ch, by_phase):
    bq, s, _ = x.shape
    if by_phase:
        rows = TOK_TILE // DFT_RADIX
        x = x.reshape(bq, s // DFT_RADIX, DFT_RADIX, D_MODEL)
        x_spec = pl.BlockSpec((None, rows, DFT_RADIX, D_MODEL), lambda b, t: (b, t, 0, 0))
        y_spec = pl.BlockSpec((None, DFT_RADIX, rows, D_MODEL), lambda b, t: (b, 0, t, 0))
        out = jax.ShapeDtypeStruct((bq, DFT_RADIX, s // DFT_RADIX, D_MODEL), BF16)
    else:
        x_spec = y_spec = _tile_spec(False)
        out = jax.ShapeDtypeStruct((bq, s, D_MODEL), BF16)
    return pl.pallas_call(
        partial(_fnet_ch_kernel, by_phase=by_phase),
        grid=(bq, s // TOK_TILE),
        in_specs=[
            x_spec, _mod_spec(0), _mod_spec(1), _row_spec(),
            pl.BlockSpec((FNET_GROUP_DIM, 2 * FNET_GROUP_DIM), lambda b, t: (0, 0)),
        ],
        out_specs=[y_spec, y_spec],
        out_shape=[out, out],
        compiler_params=_cparams(("arbitrary", "arbitrary")),
        name="fnet_channels",
    )(x, mods, mods, gain, w_ch)


def _fnet_sub_kernel(w_ref, y1_ref, y2_ref, o_ref):
    n = y1_ref.shape[0]
    o_ref[...] = (_dot(w_ref[:, :n], y1_ref[...]) + _dot(w_ref[:, n:], y2_ref[...])).astype(BF16)


def _fnet_sub_dft(w_sub, y1, y2):
    b, radix, n, _ = y1.shape
    y_spec = pl.BlockSpec((None, None, n, D_MODEL), lambda i, s: (i, s, 0, 0))
    return pl.pallas_call(
        _fnet_sub_kernel,
        grid=(b, radix),
        in_specs=[pl.BlockSpec((2 * n, 2 * n), lambda i, s: (0, 0)), y_spec, y_spec],
        out_specs=pl.BlockSpec((None, None, 2 * n, D_MODEL), lambda i, s: (i, s, 0, 0)),
        out_shape=jax.ShapeDtypeStruct((b, radix, 2 * n, D_MODEL), BF16),
        compiler_params=_cparams(("arbitrary", "arbitrary")),
        name="fnet_sub_dft",
    )(w_sub, y1, y2)


def _fft8_real(xs):
    r2 = float(np.sqrt(0.5))

    def add(a, b):
        return a[0] + b[0], a[1] + b[1]

    def sub(a, b):
        return a[0] - b[0], a[1] - b[1]

    def mul_neg_i(a):
        return a[1], -a[0]

    u = [add(xs[k], xs[k + 4]) for k in range(4)]
    d = [sub(xs[k], xs[k + 4]) for k in range(4)]
    v = [d[0],
         ((d[1][0] + d[1][1]) * r2, (d[1][1] - d[1][0]) * r2),
         mul_neg_i(d[2]),
         ((d[3][1] - d[3][0]) * r2, -(d[3][0] + d[3][1]) * r2)]

    def fft4_real(y):
        p0, p1 = add(y[0], y[2]), add(y[1], y[3])
        q0, q1 = sub(y[0], y[2]), mul_neg_i(sub(y[1], y[3]))
        return [p0[0] + p1[0], q0[0] + q1[0], p0[0] - p1[0], q0[0] - q1[0]]

    even, odd = fft4_real(u), fft4_real(v)
    return [even[0], odd[0], even[1], odd[1], even[2], odd[2], even[3], odd[3]]


def _fnet_combine_kernel(a_ref, b_ref, ca_ref, sa_ref, o_ref):
    bp = a_ref.shape[1]
    sub_rows = 16

    def body(rg, carry):
        rows = pl.ds(pl.multiple_of(rg * sub_rows, sub_rows), sub_rows)
        ca = [ca_ref[s, rows, :] for s in range(DFT_RADIX)]
        sa = [sa_ref[s, rows, :] for s in range(DFT_RADIX)]
        for lc in range(D_MODEL // LANES):
            lanes = slice(lc * LANES, (lc + 1) * LANES)
            xs = []
            for s in range(DFT_RADIX):
                a = a_ref[s, rows, lanes].astype(F32)
                b = b_ref[s, rows, lanes].astype(F32)
                if s == 0:
                    xs.append((a, -b))
                else:
                    xs.append((ca[s] * a - sa[s] * b, -(sa[s] * a + ca[s] * b)))
            for q, z in enumerate(_fft8_real(xs)):
                o_ref[q, rows, lanes] = z.astype(BF16)
        return carry

    lax.fori_loop(0, bp // sub_rows, body, 0)


def _fnet_combine(ab, ca, sa):
    b, radix, n2, _ = ab.shape
    n = n2 // 2
    bp = COMBINE_ROWS
    ab_spec = lambda off: pl.BlockSpec((None, radix, bp, D_MODEL), lambda i, p: (i, 0, off + p, 0))
    tw_spec = pl.BlockSpec((radix, bp, LANES), lambda i, p: (0, p, 0))
    z = pl.pallas_call(
        _fnet_combine_kernel,
        grid=(b, n // bp),
        in_specs=[ab_spec(0), ab_spec(n // bp), tw_spec, tw_spec],
        out_specs=pl.BlockSpec((None, radix, bp, D_MODEL), lambda i, p: (i, 0, p, 0)),
        out_shape=jax.ShapeDtypeStruct((b, radix, n, D_MODEL), BF16),
        compiler_params=_cparams(("arbitrary", "arbitrary")),
        name="fnet_combine",
    )(ab, ab, ca, sa)
    return z.reshape(b, radix * n, D_MODEL)


def _fnet_tok_kernel(gc_ref, gs_ref, y1_ref, y2_ref, o_ref, acc_ref):
    k = pl.program_id(2)

    @pl.when(k == 0)
    def _():
        acc_ref[...] = jnp.zeros_like(acc_ref)

    acc_ref[...] += _dot(gc_ref[...], y1_ref[...]) + _dot(gs_ref[...], y2_ref[...])

    @pl.when(k == pl.num_programs(2) - 1)
    def _():
        o_ref[...] = acc_ref[...].astype(BF16)


def _fnet_tokens(gc, gs, y1, y2):
    b, length, _ = y1.shape
    bm, bk = min(DFT_BM, length), min(DFT_BK, length)
    y_spec = pl.BlockSpec((None, bk, D_MODEL), lambda i, m, k: (i, k, 0))
    g_spec = pl.BlockSpec((bm, bk), lambda i, m, k: (m, k))
    return pl.pallas_call(
        _fnet_tok_kernel,
        grid=(b, length // bm, length // bk),
        in_specs=[g_spec, g_spec, y_spec, y_spec],
        out_specs=pl.BlockSpec((None, bm, D_MODEL), lambda i, m, k: (i, m, 0)),
        out_shape=jax.ShapeDtypeStruct((b, length, D_MODEL), BF16),
        scratch_shapes=[pltpu.VMEM((bm, D_MODEL), F32)],
        compiler_params=_cparams(("arbitrary", "arbitrary", "arbitrary")),
        name="fnet_tokens",
    )(gc, gs, y1, y2)


def _dft_tables(n, split):
    lp = jnp.arange(n, dtype=jnp.int32)[:, None]
    hi = jnp.arange(n // split, dtype=jnp.int32)[None, :]
    lo = jnp.arange(split, dtype=jnp.int32)[None, :]
    ang_hi = ((lp * hi) % (n // split)).astype(F32) * (2.0 * np.pi * split / n)
    ang_lo = ((lp * lo) % n).astype(F32) * (2.0 * np.pi / n)
    ch, sh_, cl, sl = jnp.cos(ang_hi), jnp.sin(ang_hi), jnp.cos(ang_lo), jnp.sin(ang_lo)
    cos = ch[:, :, None] * cl[:, None, :] - sh_[:, :, None] * sl[:, None, :]
    sin = sh_[:, :, None] * cl[:, None, :] + ch[:, :, None] * sl[:, None, :]
    return cos.reshape(n, n), sin.reshape(n, n)


def _out_ffn_kernel(x_ref, a_ref, gtm_ref, shf_ref, scf_ref, gtf_ref, g1_ref, g2_ref, g3_ref,
                    wout_ref, wg_ref, wu_ref, wd_ref, o_ref, *rest, colmajor, emit_bf16):
    h_ref, acc_ref = rest[-2:]
    k = pl.program_id(2)

    x1_ref = o_ref.reshape(TOK_TILE, D_MODEL) if colmajor else o_ref

    @pl.when(k == 0)
    def _():
        y = _dot(a_ref[...], wout_ref[...])
        x1 = _load_tile(x_ref, colmajor) + gtm_ref[...] * (_rms(y) * g1_ref[...])
        x1_ref[...] = x1
        h_ref[...] = _modulate(x1, g2_ref[...], shf_ref[...], scf_ref[...]).astype(BF16)
        acc_ref[...] = jnp.zeros_like(acc_ref)

    wg, wu, wd = wg_ref[...], wu_ref[...], wd_ref[...]
    if emit_bf16:
        wg, wu, wd = wg.astype(BF16), wu.astype(BF16), wd.astype(BF16)
        rest[0][...] = wg
        rest[1][...] = wu
        rest[2][...] = wd
    h = h_ref[...]
    gate = _dot(h, wg)
    up = _dot(h, wu)
    act = (gate * jax.nn.sigmoid(gate) * up).astype(BF16)
    acc_ref[...] += _dot(act, wd)

    @pl.when(k == pl.num_programs(2) - 1)
    def _():
        out = x1_ref[...] + gtf_ref[...] * (_rms(acc_ref[...]) * g3_ref[...])
        _store_tile(o_ref, out, colmajor)


def _out_ffn(x, a, mods, g1, g2, g3, w_out, mix_layer, ffn_w, colmajor, layer=None):
    bq, s, _ = x.shape
    emit = layer is not None
    tf = FF_TILE_F32 if emit else FF_TILE
    nf = D_FF // tf
    if emit:
        assert bq * s == TOK_TILE, "each weight block must be visited exactly once"
        w_gu, w_down = ffn_w
        weights = (w_gu, w_gu, w_down)
        w_specs = [
            pl.BlockSpec((None, D_MODEL, tf), lambda b, t, k: (layer, 0, k)),
            pl.BlockSpec((None, D_MODEL, tf), lambda b, t, k: (layer, 0, nf + k)),
            pl.BlockSpec((None, tf, D_MODEL), lambda b, t, k: (layer, k, 0)),
        ]
    else:
        weights = ffn_w
        w_specs = [
            pl.BlockSpec((None, D_MODEL, tf), lambda b, t, k: (k, 0, 0)),
            pl.BlockSpec((None, D_MODEL, tf), lambda b, t, k: (k, 0, 0)),
            pl.BlockSpec((tf, D_MODEL), lambda b, t, k: (k, 0)),
        ]
    out_specs = [_tile_spec(colmajor)]
    out_shape = [jax.ShapeDtypeStruct(_as_stream(x, colmajor).shape, F32)]
    if emit:
        per = FF_TILE // tf
        out_specs += [
            pl.BlockSpec((None, D_MODEL, tf), lambda b, t, k: (k // per, 0, k % per)),
            pl.BlockSpec((None, D_MODEL, tf), lambda b, t, k: (k // per, 0, k % per)),
            pl.BlockSpec((tf, D_MODEL), lambda b, t, k: (k, 0)),
        ]
        out_shape += [
            jax.ShapeDtypeStruct((D_FF // FF_TILE, D_MODEL, FF_TILE), BF16),
            jax.ShapeDtypeStruct((D_FF // FF_TILE, D_MODEL, FF_TILE), BF16),
            jax.ShapeDtypeStruct((D_FF, D_MODEL), BF16),
        ]
    res = pl.pallas_call(
        partial(_out_ffn_kernel, colmajor=colmajor, emit_bf16=emit),
        grid=(bq, s // TOK_TILE, nf),
        in_specs=[
            _tile_spec(colmajor), _tile_spec(False),
            _mod_spec(2), _mod_spec(3), _mod_spec(4), _mod_spec(5),
            _row_spec(), _row_spec(), _row_spec(),
            pl.BlockSpec((None, D_MODEL, D_MODEL), lambda b, t, k: (mix_layer, 0, 0),
                         pipeline_mode=pl.Buffered(1)),
        ] + w_specs,
        out_specs=out_specs,
        out_shape=out_shape,
        scratch_shapes=[
            pltpu.VMEM((TOK_TILE, D_MODEL), BF16),
            pltpu.VMEM((TOK_TILE, D_MODEL), F32),
        ],
        compiler_params=_cparams(("arbitrary", "arbitrary", "arbitrary")),
        name="out_ffn",
    )(_as_stream(x, colmajor), a, mods, mods, mods, mods, g1, g2, g3, w_out, *weights)
    if emit:
        return res[0].reshape(x.shape), tuple(res[1:])
    return res[0].reshape(x.shape)


def _cast_gu_kernel(g_ref, u_ref, go_ref, uo_ref):
    go_ref[...] = g_ref[...].astype(BF16)
    uo_ref[...] = u_ref[...].astype(BF16)


def _cast_gate_up(w_gu, layer):
    nf = D_FF // FF_TILE
    out = jax.ShapeDtypeStruct((nf, D_MODEL, FF_TILE), BF16)
    o_spec = pl.BlockSpec((None, D_MODEL, FF_TILE), lambda k: (k, 0, 0))
    return pl.pallas_call(
        _cast_gu_kernel,
        grid=(nf,),
        in_specs=[
            pl.BlockSpec((None, D_MODEL, FF_TILE), lambda k: (layer, 0, k)),
            pl.BlockSpec((None, D_MODEL, FF_TILE), lambda k: (layer, 0, nf + k)),
        ],
        out_specs=[o_spec, o_spec],
        out_shape=[out, out],
        compiler_params=_cparams(("arbitrary",)),
        name="cast_gate_up",
    )(w_gu, w_gu)


def kernel(x, c, ctx, c_ctx, ada_w, ada_b, norm_gains, gla_w_in, gla_wg2_f, gla_bg_f, gla_wg2_b, gla_bg_b,
           gla_head_gain, gla_w_out, fnet_w_out, ffn_w_gu, ffn_w_down):
    batch = x.shape[0]
    cond = jnp.zeros((ADA_ROWS, D_MODEL), F32).at[:batch].set(c).at[batch].set(c_ctx)
    mods = _ada_rows(cond, ada_w, ada_b)
    ctx_s = ctx.reshape(1, batch * CTX_LEN, D_MODEL)

    n_ch = FNET_GROUP_DIM
    cos_c, sin_c = _dft_tables(n_ch, 16)
    w_ch = (jnp.concatenate([cos_c, sin_c], axis=1) * (n_ch ** -0.5)).astype(BF16)
    cos_t, sin_t = _dft_tables(CTX_LEN, 16)
    ctx_tables = ((cos_t * (CTX_LEN ** -0.5)).astype(BF16), (sin_t * -(CTX_LEN ** -0.5)).astype(BF16))
    n_sub = SEQ // DFT_RADIX
    cos_s, sin_s = _dft_tables(n_sub, 16)
    w_sub = (jnp.block([[cos_s, -sin_s], [sin_s, cos_s]]) * (SEQ ** -0.5)).astype(BF16)
    tw_ang = ((jnp.arange(DFT_RADIX, dtype=jnp.int32)[:, None] * jnp.arange(n_sub, dtype=jnp.int32)[None, :])
              % SEQ).astype(F32) * (2.0 * np.pi / SEQ)
    tw_cos = jnp.broadcast_to(jnp.cos(tw_ang)[:, :, None], (DFT_RADIX, n_sub, LANES))
    tw_sin = jnp.broadcast_to(jnp.sin(tw_ang)[:, :, None], (DFT_RADIX, n_sub, LANES))

    gla_w_out_bf = gla_w_out.astype(BF16)
    fnet_w_out_bf = fnet_w_out.astype(BF16)

    for i in range(DEPTH):
        need_ctx = i < DEPTH - 1
        j = i // 2
        mod_lat = mods[i, :batch].reshape(batch, 1, N_ADA * D_MODEL)
        mod_ctx = mods[i, batch:batch + 1].reshape(1, 1, N_ADA * D_MODEL)
        gains = [norm_gains[i, n].reshape(1, D_MODEL) for n in range(4)]
        colmajor = False
        if i % 2 == 0:
            colmajor = j % 2 == 1
            w_low = lax.optimization_barrier(gla_w_in[j, :, GLA_MAIN_DIM:])
            w_low = jnp.pad(w_low, ((0, 0), (0, LANES - 2 * GLA_GATE_RANK))).astype(BF16)
            w2 = jnp.zeros((2, LANES, GLA_KEY_DIM), F32)
            w2 = w2.at[0, :GLA_GATE_RANK].set(gla_wg2_f[j]).at[1, GLA_GATE_RANK:2 * GLA_GATE_RANK].set(gla_wg2_b[j])
            bg = jnp.stack([gla_bg_f[j], gla_bg_b[j]]).reshape(2, 1, GLA_KEY_DIM)
            qkvr_c, low_c, w_in_bf = _gla_in(ctx_s, mod_ctx, gains[0], jnp.swapaxes(gla_w_in, 1, 2), w_low,
                                             False, layer=j)
            qkvr_l, low_l = _gla_in(x, mod_lat, gains[0], w_in_bf, w_low, colmajor)
            a_ctx, a_lat = _gla_scan(
                qkvr_c.reshape(batch, CTX_LEN, GLA_MAIN_DIM), low_c.reshape(batch, CTX_LEN, LANES),
                qkvr_l, low_l, w2.astype(BF16), bg, gla_head_gain[j].reshape(1, GLA_HEAD_V))
            w_mix = gla_w_out_bf
        else:
            y1, y2 = _fnet_channels(x, mod_lat, gains[0], w_ch, True)
            a_lat = _fnet_combine(_fnet_sub_dft(w_sub, y1, y2), tw_cos, tw_sin)
            if need_ctx:
                y1, y2 = _fnet_channels(ctx_s, mod_ctx, gains[0], w_ch, False)
                a_ctx = _fnet_tokens(*ctx_tables, y1.reshape(batch, CTX_LEN, D_MODEL),
                                     y2.reshape(batch, CTX_LEN, D_MODEL))
            w_mix = fnet_w_out_bf
        if need_ctx:
            ctx_s, ffn_bf = _out_ffn(ctx_s, a_ctx.reshape(1, batch * CTX_LEN, D_MODEL), mod_ctx,
                                     gains[1], gains[2], gains[3], w_mix, j, (ffn_w_gu, ffn_w_down), False,
                                     layer=i)
        else:
            ffn_bf = (*_cast_gate_up(ffn_w_gu, i), ffn_w_down[i].astype(BF16))
        x = _out_ffn(x, a_lat, mod_lat, gains[1], gains[2], gains[3], w_mix, j, ffn_bf, colmajor)
    return x
```

```python
from functools import partial

import jax
import jax.numpy as jnp
import numpy as np
from jax import lax
from jax.experimental import pallas as pl
from jax.experimental.pallas import tpu as pltpu

D_MODEL = 2048
SEQ = 4096
CTX_LEN = 256
GRID_W = 64
DEPTH = 4
GLA_HEADS = 4
GLA_HEAD_K = 256
GLA_HEAD_V = 512
GLA_KEY_DIM = GLA_HEADS * GLA_HEAD_K
GLA_VALUE_DIM = GLA_HEADS * GLA_HEAD_V
GLA_GATE_RANK = 16
GLA_GATE_TAU = 16.0
GLA_MAIN_DIM = 2 * GLA_KEY_DIM + 2 * GLA_VALUE_DIM
FNET_GROUPS = 4
FNET_GROUP_DIM = D_MODEL // FNET_GROUPS
D_FF = 5632
N_ADA = 6
EPS = 1e-6

LANES = 128
ADA_ROWS = 8
VMEM_LIMIT = 56 * 1024 * 1024

TOK_TILE = 512
IN_TOK_TILE = 1024
FF_TILE = 512
IN_TILE = 1536
IN_TILE_F32 = 768
FF_TILE_F32 = 256
ADA_TILE = 1536
GLA_CHUNK = 128
GLA_UNROLL = 4
DFT_BM = 1024
DFT_BK = 512
DFT_RADIX = 8
COMBINE_ROWS = 128

BF16 = jnp.bfloat16
F32 = jnp.float32


def _cparams(sem):
    return pltpu.CompilerParams(dimension_semantics=sem, vmem_limit_bytes=VMEM_LIMIT)


def _dot(a, b):
    return jnp.dot(a, b, preferred_element_type=F32)


def _rms(x):
    return x * lax.rsqrt(jnp.mean(x * x, axis=-1, keepdims=True) + EPS)


def _modulate(x, gain, shift, scale):
    return (_rms(x) * gain) * (1.0 + scale) + shift


def _ada_kernel(c_ref, w_ref, b_ref, o_ref):
    c = c_ref[...]
    s = (c * jax.nn.sigmoid(c)).astype(BF16)
    o_ref[...] = _dot(s, w_ref[...].astype(BF16)) + b_ref[...]


def _ada_rows(cond, ada_w, ada_b):
    n_out = N_ADA * D_MODEL
    return pl.pallas_call(
        _ada_kernel,
        grid=(DEPTH, n_out // ADA_TILE),
        in_specs=[
            pl.BlockSpec((ADA_ROWS, D_MODEL), lambda i, n: (0, 0)),
            pl.BlockSpec((None, D_MODEL, ADA_TILE), lambda i, n: (i, 0, n)),
            pl.BlockSpec((None, 1, ADA_TILE), lambda i, n: (i, 0, n)),
        ],
        out_specs=pl.BlockSpec((None, ADA_ROWS, ADA_TILE), lambda i, n: (i, 0, n)),
        out_shape=jax.ShapeDtypeStruct((DEPTH, ADA_ROWS, n_out), F32),
        compiler_params=_cparams(("arbitrary", "arbitrary")),
        name="ada_rows",
    )(cond, ada_w, ada_b.reshape(DEPTH, 1, n_out))


def _tile_spec(colmajor, rows=TOK_TILE):
    if colmajor:
        return pl.BlockSpec((None, GRID_W, rows // GRID_W, D_MODEL), lambda b, t, *_: (b, 0, t, 0))
    return pl.BlockSpec((None, rows, D_MODEL), lambda b, t, *_: (b, t, 0))


def _load_tile(x_ref, colmajor):
    if not colmajor:
        return x_ref[...]
    return jnp.concatenate([x_ref[:, c, :] for c in range(x_ref.shape[1])], axis=0)


def _store_tile(o_ref, val, colmajor):
    if not colmajor:
        o_ref[...] = val
        return
    for c in range(o_ref.shape[1]):
        o_ref[:, c, :] = val[c * GRID_W:(c + 1) * GRID_W, :]


def _mod_spec(j):
    return pl.BlockSpec((None, 1, D_MODEL), lambda b, t, *_: (b, 0, j))


def _row_spec():
    return pl.BlockSpec((1, D_MODEL), lambda b, t, *_: (0, 0))


def _as_stream(x, colmajor):
    if colmajor:
        bq, s, d = x.shape
        return x.reshape(bq, GRID_W, s // GRID_W, d)
    return x


def _gla_in_kernel(x_ref, sh_ref, sc_ref, g_ref, w_ref, wlow_ref, o_ref, low_ref, *rest, colmajor, emit_bf16):
    h_ref = rest[-1]
    n = pl.program_id(2)

    @pl.when(n == 0)
    def _():
        h = _modulate(_load_tile(x_ref, colmajor), g_ref[...], sh_ref[...], sc_ref[...]).astype(BF16)
        h_ref[...] = h
        low_ref[...] = _dot(h, wlow_ref[...])

    w = w_ref[...]
    if emit_bf16:
        w = w.T.astype(BF16)
        rest[0][...] = w
    o_ref[...] = _dot(h_ref[...], w).astype(BF16)


def _gla_in(x, mods, gain, w_in, w_low, colmajor, layer=None):
    bq, s, _ = x.shape
    emit = layer is not None
    tn = IN_TILE_F32 if emit else IN_TILE
    tm = min(IN_TOK_TILE, s)
    if emit:
        assert bq * s == tm, "each weight block must be visited exactly once"
        w_spec = pl.BlockSpec((None, tn, D_MODEL), lambda b, t, n: (layer, n, 0))
    else:
        w_spec = pl.BlockSpec((None, D_MODEL, tn), lambda b, t, n: (n, 0, 0))
    out_specs = [
        pl.BlockSpec((None, tm, tn), lambda b, t, n: (b, t, n)),
        pl.BlockSpec((None, tm, LANES), lambda b, t, n: (b, t, 0)),
    ]
    out_shape = [
        jax.ShapeDtypeStruct((bq, s, GLA_MAIN_DIM), BF16),
        jax.ShapeDtypeStruct((bq, s, LANES), F32),
    ]
    if emit:
        per = IN_TILE // tn
        out_specs.append(pl.BlockSpec((None, D_MODEL, tn), lambda b, t, n: (n // per, 0, n % per)))
        out_shape.append(jax.ShapeDtypeStruct((GLA_MAIN_DIM // IN_TILE, D_MODEL, IN_TILE), BF16))
    return pl.pallas_call(
        partial(_gla_in_kernel, colmajor=colmajor, emit_bf16=emit),
        grid=(bq, s // tm, GLA_MAIN_DIM // tn),
        in_specs=[
            _tile_spec(colmajor, tm), _mod_spec(0), _mod_spec(1), _row_spec(),
            w_spec,
            pl.BlockSpec((D_MODEL, LANES), lambda b, t, n: (0, 0)),
        ],
        out_specs=out_specs,
        out_shape=out_shape,
        scratch_shapes=[pltpu.VMEM((tm, D_MODEL), BF16)],
        compiler_params=_cparams(("arbitrary", "arbitrary", "arbitrary")),
        name="gla_in",
    )(_as_stream(x, colmajor), mods, mods, gain, w_in, w_low)


def _log_gate(z):
    return (jnp.minimum(z, 0.0) - jnp.log(1.0 + jnp.exp(-jnp.abs(z)))) * (1.0 / GLA_GATE_TAU)


def _gla_group(q_ref, k_ref, v_ref, low_ref, w2_ref, bg_ref, tris, masks, s_ref, starts, u):
    c, dk = GLA_CHUNK, GLA_HEAD_K
    span = [pl.ds(pl.multiple_of(starts[d], c), u * c) for d in (0, 1)]
    z = [_dot(low_ref[span[d], :].astype(BF16), w2_ref[d]) + bg_ref[d] for d in (0, 1)]
    g = [_log_gate(zd) for zd in z]
    cums = []
    for d in (0, 1):
        parts = []
        for j in range(u):
            gj = g[d][j * c:(j + 1) * c, :]
            hi = gj.astype(BF16)
            parts += [hi, (gj - hi.astype(F32)).astype(BF16)]
        cum = _dot(tris[d], jnp.concatenate(parts, axis=1))
        cums.append([cum[:, 2 * j * dk:(2 * j + 1) * dk] + cum[:, (2 * j + 1) * dk:(2 * j + 2) * dk]
                     for j in range(u)])
    order = [(d, j if d == 0 else u - 1 - j) for j in range(u) for d in (0, 1)]
    work = {}
    for d, j in order:
        rows = pl.ds(pl.multiple_of(starts[d] + j * c, c), c)
        cum = cums[d][j]
        if d == 1:
            ref, tot = cum[c // 2:c // 2 + 1, :], cum[0:1, :]
        else:
            ref, tot = cum[c // 2 - 1:c // 2, :], cum[c - 1:c, :]
        qf = q_ref[rows, :].astype(F32) * (GLA_HEAD_K ** -0.5)
        kf = k_ref[rows, :].astype(F32)
        work[d, j] = dict(
            rows=rows, tot=tot, v=v_ref[rows, :],
            q_mid=(qf * jnp.exp(cum - ref)).astype(BF16), k_mid=(kf * jnp.exp(ref - cum)).astype(BF16),
            q_dec=(qf * jnp.exp(cum)).astype(BF16), k_end=(kf * jnp.exp(tot - cum)).astype(BF16))
    for key in order:
        w = work[key]
        s = lax.dot_general(w["q_mid"], w["k_mid"], (((1,), (1,)), ((), ())), preferred_element_type=F32)
        w["scores"] = jnp.where(masks[key[0]], s, 0.0).astype(BF16)
    for key in order:
        w = work[key]
        w["o"] = _dot(w["scores"], w["v"])
        w["ds"] = lax.dot_general(w["k_end"], w["v"], (((0,), (0,)), ((), ())), preferred_element_type=F32)
        dec = jnp.broadcast_to(jnp.exp(w["tot"]), (LANES, dk)).T
        w["dec"] = jnp.concatenate([dec] * (GLA_HEAD_V // LANES), axis=1)
    out = []
    state = [s_ref[0], s_ref[1]]
    for d, j in order:
        w = work[d, j]
        out.append((w["rows"], w["o"] + _dot(w["q_dec"], state[d].astype(BF16))))
        state[d] = state[d] * w["dec"] + w["ds"]
    s_ref[0] = state[0]
    s_ref[1] = state[1]
    return out


def _gla_kernel(qc_ref, kc_ref, vc_ref, rc_ref, lowc_ref, ql_ref, kl_ref, vl_ref, rl_ref, lowl_ref,
                w2_ref, bg_ref, hg_ref, oc_ref, ol_ref, s_ref, acc_ref):
    c = GLA_CHUNK
    row = lax.broadcasted_iota(jnp.int32, (c, c), 0)
    col = lax.broadcasted_iota(jnp.int32, (c, c), 1)
    masks = (col <= row, col >= row)
    tris = tuple(m.astype(BF16) for m in masks)
    hg = hg_ref[...]
    s_ref[...] = jnp.zeros_like(s_ref)

    def phase(q_ref, k_ref, v_ref, r_ref, low_ref, o_ref, length):
        u = min(GLA_UNROLL, length // (2 * c))
        n_groups = length // (u * c)
        half = n_groups // 2

        def group(n):
            starts = (n * (u * c), (n_groups - 1 - n) * (u * c))
            return _gla_group(q_ref, k_ref, v_ref, low_ref, w2_ref, bg_ref, tris, masks, s_ref, starts, u)

        def first_visit(n, carry):
            for rows, o in group(n):
                acc_ref[rows, :] = o
            return carry

        def second_visit(n, carry):
            for rows, o in group(n):
                o = _rms(o + acc_ref[rows, :]) * hg
                r = r_ref[rows, :].astype(F32)
                o_ref[rows, :] = (o * (r * jax.nn.sigmoid(r))).astype(BF16)
            return carry

        lax.fori_loop(0, half, first_visit, 0)
        lax.fori_loop(half, n_groups, second_visit, 0)

    phase(qc_ref, kc_ref, vc_ref, rc_ref, lowc_ref, oc_ref, CTX_LEN)
    phase(ql_ref, kl_ref, vl_ref, rl_ref, lowl_ref, ol_ref, SEQ)


def _gla_scan(qkvr_c, low_c, qkvr_l, low_l, w2, bg, head_gain):
    b = qkvr_l.shape[0]
    nk = GLA_KEY_DIM // GLA_HEAD_K

    def stream_specs(length):
        return [
            pl.BlockSpec((None, length, GLA_HEAD_K), lambda i, h: (i, 0, h)),
            pl.BlockSpec((None, length, GLA_HEAD_K), lambda i, h: (i, 0, nk + h)),
            pl.BlockSpec((None, length, GLA_HEAD_V), lambda i, h: (i, 0, nk + h)),
            pl.BlockSpec((None, length, GLA_HEAD_V), lambda i, h: (i, 0, 2 * nk + h)),
            pl.BlockSpec((None, length, LANES), lambda i, h: (i, 0, 0)),
        ]

    def out_spec(length):
        return pl.BlockSpec((None, length, GLA_HEAD_V), lambda i, h: (i, 0, h))

    return pl.pallas_call(
        _gla_kernel,
        grid=(b, GLA_HEADS),
        in_specs=stream_specs(CTX_LEN) + stream_specs(SEQ) + [
            pl.BlockSpec((2, LANES, GLA_HEAD_K), lambda i, h: (0, 0, h)),
            pl.BlockSpec((2, 1, GLA_HEAD_K), lambda i, h: (0, 0, h)),
            pl.BlockSpec((1, GLA_HEAD_V), lambda i, h: (0, 0)),
        ],
        out_specs=[out_spec(CTX_LEN), out_spec(SEQ)],
        out_shape=[
            jax.ShapeDtypeStruct((b, CTX_LEN, GLA_VALUE_DIM), BF16),
            jax.ShapeDtypeStruct((b, SEQ, GLA_VALUE_DIM), BF16),
        ],
        scratch_shapes=[
            pltpu.VMEM((2, GLA_HEAD_K, GLA_HEAD_V), F32),
            pltpu.VMEM((SEQ, GLA_HEAD_V), F32),
        ],
        compiler_params=_cparams(("arbitrary", "arbitrary")),
        name="gla_scan",
    )(qkvr_c, qkvr_c, qkvr_c, qkvr_c, low_c, qkvr_l, qkvr_l, qkvr_l, qkvr_l, low_l, w2, bg, head_gain)


def _fnet_ch_kernel(x_ref, sh_ref, sc_ref, g_ref, w_ref, y1_ref, y2_ref, *, by_phase):
    if by_phase:
        x = jnp.concatenate([x_ref[:, s, :] for s in range(DFT_RADIX)], axis=0)
    else:
        x = x_ref[...]
    h = _modulate(x, g_ref[...], sh_ref[...], sc_ref[...]).astype(BF16)
    w = w_ref[...]
    gd = FNET_GROUP_DIM
    rows = TOK_TILE // DFT_RADIX
    for g in range(FNET_GROUPS):
        r = _dot(h[:, g * gd:(g + 1) * gd], w).astype(BF16)
        cols = slice(g * gd, (g + 1) * gd)
        if by_phase:
            for s in range(DFT_RADIX):
                y1_ref[s, :, cols] = r[s * rows:(s + 1) * rows, :gd]
                y2_ref[s, :, cols] = r[s * rows:(s + 1) * rows, gd:]
        else:
            y1_ref[:, cols] = r[:, :gd]
            y2_ref[:, cols] = r[:, gd:]


def _fnet_channels(x, mods, gain, w_ch, by_phase):
    bq, s, _ = x.shape
    if by_phase:
        rows = TOK_TILE // DFT_RADIX
        x = x.reshape(bq, s // DFT_RADIX, DFT_RADIX, D_MODEL)
        x_spec = pl.BlockSpec((None, rows, DFT_RADIX, D_MODEL), lambda b, t: (b, t, 0, 0))
        y_spec = pl.BlockSpec((None, DFT_RADIX, rows, D_MODEL), lambda b, t: (b, 0, t, 0))
        out = jax.ShapeDtypeStruct((bq, DFT_RADIX, s // DFT_RADIX, D_MODEL), BF16)
    else:
        x_spec = y_spec = _tile_spec(False)
        out = jax.ShapeDtypeStruct((bq, s, D_MODEL), BF16)
    return pl.pallas_call(
        partial(_fnet_ch_kernel, by_phase=by_phase),
        grid=(bq, s // TOK_TILE),
        in_specs=[
            x_spec, _mod_spec(0), _mod_spec(1), _row_spec(),
            pl.BlockSpec((FNET_GROUP_DIM, 2 * FNET_GROUP_DIM), lambda b, t: (0, 0)),
        ],
        out_specs=[y_spec, y_spec],
        out_shape=[out, out],
        compiler_params=_cparams(("arbitrary", "arbitrary")),
        name="fnet_channels",
    )(x, mods, mods, gain, w_ch)


def _fnet_sub_kernel(w_ref, y1_ref, y2_ref, o_ref):
    n = y1_ref.shape[0]
    y1, y2 = y1_ref[...], y2_ref[...]
    k1 = _dot(w_ref[0], y1 + y2)
    o_ref[n:, :] = (-(k1 + _dot(w_ref[1], y1))).astype(BF16)
    o_ref[:n, :] = (k1 - _dot(w_ref[2], y2)).astype(BF16)


def _fnet_sub_dft(w_sub, y1, y2):
    b, radix, n, _ = y1.shape
    y_spec = pl.BlockSpec((None, None, n, D_MODEL), lambda i, s: (i, s, 0, 0))
    return pl.pallas_call(
        _fnet_sub_kernel,
        grid=(b, radix),
        in_specs=[pl.BlockSpec((None, 3, n, n), lambda i, s: (s, 0, 0, 0)), y_spec, y_spec],
        out_specs=pl.BlockSpec((None, None, 2 * n, D_MODEL), lambda i, s: (i, s, 0, 0)),
        out_shape=jax.ShapeDtypeStruct((b, radix, 2 * n, D_MODEL), BF16),
        compiler_params=_cparams(("arbitrary", "arbitrary")),
        name="fnet_sub_dft",
    )(w_sub, y1, y2)


def _fft8_real(xs):
    r2 = float(np.sqrt(0.5))

    def add(a, b):
        return a[0] + b[0], a[1] + b[1]

    def sub(a, b):
        return a[0] - b[0], a[1] - b[1]

    def mul_neg_i(a):
        return a[1], -a[0]

    u = [add(xs[k], xs[k + 4]) for k in range(4)]
    d = [sub(xs[k], xs[k + 4]) for k in range(4)]
    v = [d[0],
         ((d[1][0] + d[1][1]) * r2, (d[1][1] - d[1][0]) * r2),
         mul_neg_i(d[2]),
         ((d[3][1] - d[3][0]) * r2, -(d[3][0] + d[3][1]) * r2)]

    def fft4_real(y):
        p0, p1 = add(y[0], y[2]), add(y[1], y[3])
        q0, q1 = sub(y[0], y[2]), mul_neg_i(sub(y[1], y[3]))
        return [p0[0] + p1[0], q0[0] + q1[0], p0[0] - p1[0], q0[0] - q1[0]]

    even, odd = fft4_real(u), fft4_real(v)
    return [even[0], odd[0], even[1], odd[1], even[2], odd[2], even[3], odd[3]]


def _fnet_combine_kernel(re_ref, im_ref, o_ref):
    bp = re_ref.shape[1]
    sub_rows = 16

    def body(rg, carry):
        rows = pl.ds(pl.multiple_of(rg * sub_rows, sub_rows), sub_rows)
        for lc in range(D_MODEL // LANES):
            lanes = slice(lc * LANES, (lc + 1) * LANES)
            xs = [(re_ref[s, rows, lanes].astype(F32), im_ref[s, rows, lanes].astype(F32))
                  for s in range(DFT_RADIX)]
            for q, z in enumerate(_fft8_real(xs)):
                o_ref[q, rows, lanes] = z.astype(BF16)
        return carry

    lax.fori_loop(0, bp // sub_rows, body, 0)


def _fnet_combine(parts):
    b, radix, n2, _ = parts.shape
    n = n2 // 2
    bp = COMBINE_ROWS
    part_spec = lambda off: pl.BlockSpec((None, radix, bp, D_MODEL), lambda i, p: (i, 0, off + p, 0))
    z = pl.pallas_call(
        _fnet_combine_kernel,
        grid=(b, n // bp),
        in_specs=[part_spec(0), part_spec(n // bp)],
        out_specs=pl.BlockSpec((None, radix, bp, D_MODEL), lambda i, p: (i, 0, p, 0)),
        out_shape=jax.ShapeDtypeStruct((b, radix, n, D_MODEL), BF16),
        compiler_params=_cparams(("arbitrary", "arbitrary")),
        name="fnet_combine",
    )(parts, parts)
    return z.reshape(b, radix * n, D_MODEL)


def _fnet_tok_kernel(gc_ref, gs_ref, y1_ref, y2_ref, o_ref, acc_ref):
    k = pl.program_id(2)

    @pl.when(k == 0)
    def _():
        acc_ref[...] = jnp.zeros_like(acc_ref)

    acc_ref[...] += _dot(gc_ref[...], y1_ref[...]) + _dot(gs_ref[...], y2_ref[...])

    @pl.when(k == pl.num_programs(2) - 1)
    def _():
        o_ref[...] = acc_ref[...].astype(BF16)


def _fnet_tokens(gc, gs, y1, y2):
    b, length, _ = y1.shape
    bm, bk = min(DFT_BM, length), min(DFT_BK, length)
    y_spec = pl.BlockSpec((None, bk, D_MODEL), lambda i, m, k: (i, k, 0))
    g_spec = pl.BlockSpec((bm, bk), lambda i, m, k: (m, k))
    return pl.pallas_call(
        _fnet_tok_kernel,
        grid=(b, length // bm, length // bk),
        in_specs=[g_spec, g_spec, y_spec, y_spec],
        out_specs=pl.BlockSpec((None, bm, D_MODEL), lambda i, m, k: (i, m, 0)),
        out_shape=jax.ShapeDtypeStruct((b, length, D_MODEL), BF16),
        scratch_shapes=[pltpu.VMEM((bm, D_MODEL), F32)],
        compiler_params=_cparams(("arbitrary", "arbitrary", "arbitrary")),
        name="fnet_tokens",
    )(gc, gs, y1, y2)


def _dft_tables(n, split):
    lp = jnp.arange(n, dtype=jnp.int32)[:, None]
    hi = jnp.arange(n // split, dtype=jnp.int32)[None, :]
    lo = jnp.arange(split, dtype=jnp.int32)[None, :]
    ang_hi = ((lp * hi) % (n // split)).astype(F32) * (2.0 * np.pi * split / n)
    ang_lo = ((lp * lo) % n).astype(F32) * (2.0 * np.pi / n)
    ch, sh_, cl, sl = jnp.cos(ang_hi), jnp.sin(ang_hi), jnp.cos(ang_lo), jnp.sin(ang_lo)
    cos = ch[:, :, None] * cl[:, None, :] - sh_[:, :, None] * sl[:, None, :]
    sin = sh_[:, :, None] * cl[:, None, :] + ch[:, :, None] * sl[:, None, :]
    return cos.reshape(n, n), sin.reshape(n, n)


def _out_ffn_kernel(x_ref, a_ref, gtm_ref, shf_ref, scf_ref, gtf_ref, g1_ref, g2_ref, g3_ref,
                    wout_ref, wg_ref, wu_ref, wd_ref, o_ref, *rest, colmajor, emit_bf16):
    h_ref, acc_ref = rest[-2:]
    k = pl.program_id(2)

    x1_ref = o_ref.reshape(TOK_TILE, D_MODEL) if colmajor else o_ref

    @pl.when(k == 0)
    def _():
        y = _dot(a_ref[...], wout_ref[...])
        x1 = _load_tile(x_ref, colmajor) + gtm_ref[...] * (_rms(y) * g1_ref[...])
        x1_ref[...] = x1
        h_ref[...] = _modulate(x1, g2_ref[...], shf_ref[...], scf_ref[...]).astype(BF16)
        acc_ref[...] = jnp.zeros_like(acc_ref)

    wg, wu, wd = wg_ref[...], wu_ref[...], wd_ref[...]
    if emit_bf16:
        wg, wu, wd = wg.astype(BF16), wu.astype(BF16), wd.astype(BF16)
        rest[0][...] = wg
        rest[1][...] = wu
        rest[2][...] = wd
    h = h_ref[...]
    gate = _dot(h, wg)
    up = _dot(h, wu)
    act = (gate * jax.nn.sigmoid(gate) * up).astype(BF16)
    acc_ref[...] += _dot(act, wd)

    @pl.when(k == pl.num_programs(2) - 1)
    def _():
        out = x1_ref[...] + gtf_ref[...] * (_rms(acc_ref[...]) * g3_ref[...])
        _store_tile(o_ref, out, colmajor)


def _out_ffn(x, a, mods, g1, g2, g3, w_out, mix_layer, ffn_w, colmajor, layer=None):
    bq, s, _ = x.shape
    emit = layer is not None
    tf = FF_TILE_F32 if emit else FF_TILE
    nf = D_FF // tf
    if emit:
        assert bq * s == TOK_TILE, "each weight block must be visited exactly once"
        w_gu, w_down = ffn_w
        weights = (w_gu, w_gu, w_down)
        w_specs = [
            pl.BlockSpec((None, D_MODEL, tf), lambda b, t, k: (layer, 0, k)),
            pl.BlockSpec((None, D_MODEL, tf), lambda b, t, k: (layer, 0, nf + k)),
            pl.BlockSpec((None, tf, D_MODEL), lambda b, t, k: (layer, k, 0)),
        ]
    else:
        weights = ffn_w
        w_specs = [
            pl.BlockSpec((None, D_MODEL, tf), lambda b, t, k: (k, 0, 0)),
            pl.BlockSpec((None, D_MODEL, tf), lambda b, t, k: (k, 0, 0)),
            pl.BlockSpec((tf, D_MODEL), lambda b, t, k: (k, 0)),
        ]
    out_specs = [_tile_spec(colmajor)]
    out_shape = [jax.ShapeDtypeStruct(_as_stream(x, colmajor).shape, F32)]
    if emit:
        per = FF_TILE // tf
        out_specs += [
            pl.BlockSpec((None, D_MODEL, tf), lambda b, t, k: (k // per, 0, k % per)),
            pl.BlockSpec((None, D_MODEL, tf), lambda b, t, k: (k // per, 0, k % per)),
            pl.BlockSpec((tf, D_MODEL), lambda b, t, k: (k, 0)),
        ]
        out_shape += [
            jax.ShapeDtypeStruct((D_FF // FF_TILE, D_MODEL, FF_TILE), BF16),
            jax.ShapeDtypeStruct((D_FF // FF_TILE, D_MODEL, FF_TILE), BF16),
            jax.ShapeDtypeStruct((D_FF, D_MODEL), BF16),
        ]
    res = pl.pallas_call(
        partial(_out_ffn_kernel, colmajor=colmajor, emit_bf16=emit),
        grid=(bq, s // TOK_TILE, nf),
        in_specs=[
            _tile_spec(colmajor), _tile_spec(False),
            _mod_spec(2), _mod_spec(3), _mod_spec(4), _mod_spec(5),
            _row_spec(), _row_spec(), _row_spec(),
            pl.BlockSpec((None, D_MODEL, D_MODEL), lambda b, t, k: (mix_layer, 0, 0),
                         pipeline_mode=pl.Buffered(1)),
        ] + w_specs,
        out_specs=out_specs,
        out_shape=out_shape,
        scratch_shapes=[
            pltpu.VMEM((TOK_TILE, D_MODEL), BF16),
            pltpu.VMEM((TOK_TILE, D_MODEL), F32),
        ],
        compiler_params=_cparams(("arbitrary", "arbitrary", "arbitrary")),
        name="out_ffn",
    )(_as_stream(x, colmajor), a, mods, mods, mods, mods, g1, g2, g3, w_out, *weights)
    if emit:
        return res[0].reshape(x.shape), tuple(res[1:])
    return res[0].reshape(x.shape)


def _cast_gu_kernel(g_ref, u_ref, go_ref, uo_ref):
    go_ref[...] = g_ref[...].astype(BF16)
    uo_ref[...] = u_ref[...].astype(BF16)


def _cast_gate_up(w_gu, layer):
    nf = D_FF // FF_TILE
    out = jax.ShapeDtypeStruct((nf, D_MODEL, FF_TILE), BF16)
    o_spec = pl.BlockSpec((None, D_MODEL, FF_TILE), lambda k: (k, 0, 0))
    return pl.pallas_call(
        _cast_gu_kernel,
        grid=(nf,),
        in_specs=[
            pl.BlockSpec((None, D_MODEL, FF_TILE), lambda k: (layer, 0, k)),
            pl.BlockSpec((None, D_MODEL, FF_TILE), lambda k: (layer, 0, nf + k)),
        ],
        out_specs=[o_spec, o_spec],
        out_shape=[out, out],
        compiler_params=_cparams(("arbitrary",)),
        name="cast_gate_up",
    )(w_gu, w_gu)


def kernel(x, c, ctx, c_ctx, ada_w, ada_b, norm_gains, gla_w_in, gla_wg2_f, gla_bg_f, gla_wg2_b, gla_bg_b,
           gla_head_gain, gla_w_out, fnet_w_out, ffn_w_gu, ffn_w_down):
    batch = x.shape[0]
    cond = jnp.zeros((ADA_ROWS, D_MODEL), F32).at[:batch].set(c).at[batch].set(c_ctx)
    mods = _ada_rows(cond, ada_w, ada_b)
    ctx_s = ctx.reshape(1, batch * CTX_LEN, D_MODEL)

    n_ch = FNET_GROUP_DIM
    cos_c, sin_c = _dft_tables(n_ch, 16)
    w_ch = (jnp.concatenate([cos_c, sin_c], axis=1) * (n_ch ** -0.5)).astype(BF16)
    cos_t, sin_t = _dft_tables(CTX_LEN, 16)
    ctx_tables = ((cos_t * (CTX_LEN ** -0.5)).astype(BF16), (sin_t * -(CTX_LEN ** -0.5)).astype(BF16))
    n_sub = SEQ // DFT_RADIX
    cos_s, sin_s = _dft_tables(n_sub, 16)
    tw_ang = ((jnp.arange(DFT_RADIX, dtype=jnp.int32)[:, None] * jnp.arange(n_sub, dtype=jnp.int32)[None, :])
              % SEQ).astype(F32) * (2.0 * np.pi / SEQ)
    tw_cos, tw_sin = jnp.cos(tw_ang)[:, :, None], jnp.sin(tw_ang)[:, :, None]
    cos_p = cos_s[None] * tw_cos - sin_s[None] * tw_sin
    sin_p = sin_s[None] * tw_cos + cos_s[None] * tw_sin
    w_sub = (jnp.stack([cos_p, sin_p - cos_p, cos_p + sin_p], axis=1) * (SEQ ** -0.5)).astype(BF16)

    gla_w_out_bf = gla_w_out.astype(BF16)
    fnet_w_out_bf = fnet_w_out.astype(BF16)

    for i in range(DEPTH):
        need_ctx = i < DEPTH - 1
        j = i // 2
        mod_lat = mods[i, :batch].reshape(batch, 1, N_ADA * D_MODEL)
        mod_ctx = mods[i, batch:batch + 1].reshape(1, 1, N_ADA * D_MODEL)
        gains = [norm_gains[i, n].reshape(1, D_MODEL) for n in range(4)]
        colmajor = False
        if i % 2 == 0:
            colmajor = j % 2 == 1
            w_low = lax.optimization_barrier(gla_w_in[j, :, GLA_MAIN_DIM:])
            w_low = jnp.pad(w_low, ((0, 0), (0, LANES - 2 * GLA_GATE_RANK))).astype(BF16)
            w2 = jnp.zeros((2, LANES, GLA_KEY_DIM), F32)
            w2 = w2.at[0, :GLA_GATE_RANK].set(gla_wg2_f[j]).at[1, GLA_GATE_RANK:2 * GLA_GATE_RANK].set(gla_wg2_b[j])
            bg = jnp.stack([gla_bg_f[j], gla_bg_b[j]]).reshape(2, 1, GLA_KEY_DIM)
            qkvr_c, low_c, w_in_bf = _gla_in(ctx_s, mod_ctx, gains[0], jnp.swapaxes(gla_w_in, 1, 2), w_low,
                                             False, layer=j)
            qkvr_l, low_l = _gla_in(x, mod_lat, gains[0], w_in_bf, w_low, colmajor)
            a_ctx, a_lat = _gla_scan(
                qkvr_c.reshape(batch, CTX_LEN, GLA_MAIN_DIM), low_c.reshape(batch, CTX_LEN, LANES),
                qkvr_l, low_l, w2.astype(BF16), bg, gla_head_gain[j].reshape(1, GLA_HEAD_V))
            w_mix = gla_w_out_bf
        else:
            y1, y2 = _fnet_channels(x, mod_lat, gains[0], w_ch, True)
            a_lat = _fnet_combine(_fnet_sub_dft(w_sub, y1, y2))
            if need_ctx:
                y1, y2 = _fnet_channels(ctx_s, mod_ctx, gains[0], w_ch, False)
                a_ctx = _fnet_tokens(*ctx_tables, y1.reshape(batch, CTX_LEN, D_MODEL),
                                     y2.reshape(batch, CTX_LEN, D_MODEL))
            w_mix = fnet_w_out_bf
        if need_ctx:
            ctx_s, ffn_bf = _out_ffn(ctx_s, a_ctx.reshape(1, batch * CTX_LEN, D_MODEL), mod_ctx,
                                     gains[1], gains[2], gains[3], w_mix, j, (ffn_w_gu, ffn_w_down), False,
                                     layer=i)
        else:
            ffn_bf = (*_cast_gate_up(ffn_w_gu, i), ffn_w_down[i].astype(BF16))
        x = _out_ffn(x, a_lat, mod_lat, gains[1], gains[2], gains[3], w_mix, j, ffn_bf, colmajor)
    return x
```

```python
from functools import partial

import jax
import jax.numpy as jnp
import numpy as np
from jax import lax
from jax.experimental import pallas as pl
from jax.experimental.pallas import tpu as pltpu

D_MODEL = 2048
SEQ = 4096
CTX_LEN = 256
GRID_W = 64
DEPTH = 4
GLA_HEADS = 4
GLA_HEAD_K = 256
GLA_HEAD_V = 512
GLA_KEY_DIM = GLA_HEADS * GLA_HEAD_K
GLA_VALUE_DIM = GLA_HEADS * GLA_HEAD_V
GLA_GATE_RANK = 16
GLA_GATE_TAU = 16.0
GLA_MAIN_DIM = 2 * GLA_KEY_DIM + 2 * GLA_VALUE_DIM
FNET_GROUPS = 4
FNET_GROUP_DIM = D_MODEL // FNET_GROUPS
D_FF = 5632
N_ADA = 6
EPS = 1e-6

LANES = 128
ADA_ROWS = 8
VMEM_LIMIT = 56 * 1024 * 1024

TOK_TILE = 512
IN_TOK_TILE = 1024
FF_TILE = 512
IN_TILE = 1536
IN_TILE_F32 = 768
FF_TILE_F32 = 256
ADA_TILE = 2048
GLA_CHUNK = 128
GLA_UNROLL = 4
DFT_BM = 1024
DFT_BK = 512
DFT_RADIX = 8
COMBINE_ROWS = 128
DFT_TABLE_SPLIT = 16

BF16 = jnp.bfloat16
F32 = jnp.float32


def _cparams(sem):
    return pltpu.CompilerParams(dimension_semantics=sem, vmem_limit_bytes=VMEM_LIMIT)


def _dot(a, b):
    return jnp.dot(a, b, preferred_element_type=F32)


def _rms(x):
    return x * lax.rsqrt(jnp.mean(x * x, axis=-1, keepdims=True) + EPS)


def _modulate(x, gain, shift, scale):
    return (_rms(x) * gain) * (1.0 + scale) + shift


def _ada_kernel(c_ref, w_ref, b_ref, o_ref):
    c = c_ref[...]
    s = (c * jax.nn.sigmoid(c)).astype(BF16)
    o_ref[...] = _dot(s, w_ref[...].astype(BF16)) + b_ref[...]


def _ada_rows(cond, ada_w, ada_b):
    n_out = N_ADA * D_MODEL
    return pl.pallas_call(
        _ada_kernel,
        grid=(DEPTH, n_out // ADA_TILE),
        in_specs=[
            pl.BlockSpec((ADA_ROWS, D_MODEL), lambda i, n: (0, 0)),
            pl.BlockSpec((None, D_MODEL, ADA_TILE), lambda i, n: (i, 0, n)),
            pl.BlockSpec((None, 1, ADA_TILE), lambda i, n: (i, 0, n)),
        ],
        out_specs=pl.BlockSpec((None, ADA_ROWS, ADA_TILE), lambda i, n: (i, 0, n)),
        out_shape=jax.ShapeDtypeStruct((DEPTH, ADA_ROWS, n_out), F32),
        compiler_params=_cparams(("arbitrary", "arbitrary")),
        name="ada_rows",
    )(cond, ada_w, ada_b.reshape(DEPTH, 1, n_out))


def _tile_spec(colmajor, rows=TOK_TILE):
    if colmajor:
        return pl.BlockSpec((None, GRID_W, rows // GRID_W, D_MODEL), lambda b, t, *_: (b, 0, t, 0))
    return pl.BlockSpec((None, rows, D_MODEL), lambda b, t, *_: (b, t, 0))


def _load_tile(x_ref, colmajor):
    if not colmajor:
        return x_ref[...]
    return jnp.concatenate([x_ref[:, c, :] for c in range(x_ref.shape[1])], axis=0)


def _store_tile(o_ref, val, colmajor):
    if not colmajor:
        o_ref[...] = val
        return
    for c in range(o_ref.shape[1]):
        o_ref[:, c, :] = val[c * GRID_W:(c + 1) * GRID_W, :]


def _mod_spec(j):
    return pl.BlockSpec((None, 1, D_MODEL), lambda b, t, *_: (b, 0, j))


def _row_spec():
    return pl.BlockSpec((1, D_MODEL), lambda b, t, *_: (0, 0))


def _as_stream(x, colmajor):
    if colmajor:
        bq, s, d = x.shape
        return x.reshape(bq, GRID_W, s // GRID_W, d)
    return x


def _gla_in_kernel(x_ref, sh_ref, sc_ref, g_ref, w_ref, wlow_ref, o_ref, low_ref, *rest, colmajor, emit_bf16):
    h_ref = rest[-1]
    n = pl.program_id(2)

    @pl.when(n == 0)
    def _():
        h = _modulate(_load_tile(x_ref, colmajor), g_ref[...], sh_ref[...], sc_ref[...]).astype(BF16)
        h_ref[...] = h
        low_ref[...] = _dot(h, wlow_ref[...])

    w = w_ref[...]
    if emit_bf16:
        w = w.T.astype(BF16)
        rest[0][...] = w
    o_ref[...] = _dot(h_ref[...], w).astype(BF16)


def _gla_in(x, mods, gain, w_in, w_low, colmajor, layer=None):
    bq, s, _ = x.shape
    emit = layer is not None
    tn = IN_TILE_F32 if emit else IN_TILE
    tm = min(IN_TOK_TILE, s)
    if emit:
        assert bq * s == tm, "each weight block must be visited exactly once"
        w_spec = pl.BlockSpec((None, tn, D_MODEL), lambda b, t, n: (layer, n, 0))
    else:
        w_spec = pl.BlockSpec((None, D_MODEL, tn), lambda b, t, n: (n, 0, 0))
    out_specs = [
        pl.BlockSpec((None, tm, tn), lambda b, t, n: (b, t, n)),
        pl.BlockSpec((None, tm, LANES), lambda b, t, n: (b, t, 0)),
    ]
    out_shape = [
        jax.ShapeDtypeStruct((bq, s, GLA_MAIN_DIM), BF16),
        jax.ShapeDtypeStruct((bq, s, LANES), F32),
    ]
    if emit:
        per = IN_TILE // tn
        out_specs.append(pl.BlockSpec((None, D_MODEL, tn), lambda b, t, n: (n // per, 0, n % per)))
        out_shape.append(jax.ShapeDtypeStruct((GLA_MAIN_DIM // IN_TILE, D_MODEL, IN_TILE), BF16))
    return pl.pallas_call(
        partial(_gla_in_kernel, colmajor=colmajor, emit_bf16=emit),
        grid=(bq, s // tm, GLA_MAIN_DIM // tn),
        in_specs=[
            _tile_spec(colmajor, tm), _mod_spec(0), _mod_spec(1), _row_spec(),
            w_spec,
            pl.BlockSpec((D_MODEL, LANES), lambda b, t, n: (0, 0)),
        ],
        out_specs=out_specs,
        out_shape=out_shape,
        scratch_shapes=[pltpu.VMEM((tm, D_MODEL), BF16)],
        compiler_params=_cparams(("arbitrary", "arbitrary", "arbitrary")),
        name="gla_in",
    )(_as_stream(x, colmajor), mods, mods, gain, w_in, w_low)


def _log_gate(z):
    return (jnp.minimum(z, 0.0) - jnp.log(1.0 + jnp.exp(-jnp.abs(z)))) * (1.0 / GLA_GATE_TAU)


def _gla_group(q_ref, k_ref, v_ref, low_ref, w2_ref, bg_ref, tris, masks, s_ref, starts, u):
    c, dk = GLA_CHUNK, GLA_HEAD_K
    span = [pl.ds(pl.multiple_of(starts[d], c), u * c) for d in (0, 1)]
    z = [_dot(low_ref[span[d], :].astype(BF16), w2_ref[d]) + bg_ref[d] for d in (0, 1)]
    g = [_log_gate(zd) for zd in z]
    cums = []
    for d in (0, 1):
        parts = []
        for j in range(u):
            gj = g[d][j * c:(j + 1) * c, :]
            hi = gj.astype(BF16)
            parts += [hi, (gj - hi.astype(F32)).astype(BF16)]
        cum = _dot(tris[d], jnp.concatenate(parts, axis=1))
        cums.append([cum[:, 2 * j * dk:(2 * j + 1) * dk] + cum[:, (2 * j + 1) * dk:(2 * j + 2) * dk]
                     for j in range(u)])
    order = [(d, j if d == 0 else u - 1 - j) for j in range(u) for d in (0, 1)]
    work = {}
    for d, j in order:
        rows = pl.ds(pl.multiple_of(starts[d] + j * c, c), c)
        cum = cums[d][j]
        if d == 1:
            ref, tot = cum[c // 2:c // 2 + 1, :], cum[0:1, :]
        else:
            ref, tot = cum[c // 2 - 1:c // 2, :], cum[c - 1:c, :]
        qf = q_ref[rows, :].astype(F32) * (GLA_HEAD_K ** -0.5)
        kf = k_ref[rows, :].astype(F32)
        work[d, j] = dict(
            rows=rows, tot=tot, v=v_ref[rows, :],
            q_mid=(qf * jnp.exp(cum - ref)).astype(BF16), k_mid=(kf * jnp.exp(ref - cum)).astype(BF16),
            q_dec=(qf * jnp.exp(cum)).astype(BF16), k_end=(kf * jnp.exp(tot - cum)).astype(BF16))
    for key in order:
        w = work[key]
        s = lax.dot_general(w["q_mid"], w["k_mid"], (((1,), (1,)), ((), ())), preferred_element_type=F32)
        w["scores"] = jnp.where(masks[key[0]], s, 0.0).astype(BF16)
    for key in order:
        w = work[key]
        w["o"] = _dot(w["scores"], w["v"])
        w["ds"] = lax.dot_general(w["k_end"], w["v"], (((0,), (0,)), ((), ())), preferred_element_type=F32)
        dec = jnp.broadcast_to(jnp.exp(w["tot"]), (LANES, dk)).T
        w["dec"] = jnp.concatenate([dec] * (GLA_HEAD_V // LANES), axis=1)
    out = []
    state = [s_ref[0], s_ref[1]]
    for d, j in order:
        w = work[d, j]
        out.append((w["rows"], w["o"] + _dot(w["q_dec"], state[d].astype(BF16))))
        state[d] = state[d] * w["dec"] + w["ds"]
    s_ref[0] = state[0]
    s_ref[1] = state[1]
    return out


def _gla_kernel(qc_ref, kc_ref, vc_ref, rc_ref, lowc_ref, ql_ref, kl_ref, vl_ref, rl_ref, lowl_ref,
                w2_ref, bg_ref, hg_ref, oc_ref, ol_ref, s_ref, acc_ref):
    c = GLA_CHUNK
    row = lax.broadcasted_iota(jnp.int32, (c, c), 0)
    col = lax.broadcasted_iota(jnp.int32, (c, c), 1)
    masks = (col <= row, col >= row)
    tris = tuple(m.astype(BF16) for m in masks)
    hg = hg_ref[...]
    s_ref[...] = jnp.zeros_like(s_ref)

    def phase(q_ref, k_ref, v_ref, r_ref, low_ref, o_ref, length):
        u = min(GLA_UNROLL, length // (2 * c))
        n_groups = length // (u * c)
        half = n_groups // 2

        def group(n):
            starts = (n * (u * c), (n_groups - 1 - n) * (u * c))
            return _gla_group(q_ref, k_ref, v_ref, low_ref, w2_ref, bg_ref, tris, masks, s_ref, starts, u)

        def first_visit(n, carry):
            for rows, o in group(n):
                acc_ref[rows, :] = o
            return carry

        def second_visit(n, carry):
            for rows, o in group(n):
                o = _rms(o + acc_ref[rows, :]) * hg
                r = r_ref[rows, :].astype(F32)
                o_ref[rows, :] = (o * (r * jax.nn.sigmoid(r))).astype(BF16)
            return carry

        lax.fori_loop(0, half, first_visit, 0)
        lax.fori_loop(half, n_groups, second_visit, 0)

    phase(qc_ref, kc_ref, vc_ref, rc_ref, lowc_ref, oc_ref, CTX_LEN)
    phase(ql_ref, kl_ref, vl_ref, rl_ref, lowl_ref, ol_ref, SEQ)


def _gla_scan(qkvr_c, low_c, qkvr_l, low_l, w2, bg, head_gain):
    b = qkvr_l.shape[0]
    nk = GLA_KEY_DIM // GLA_HEAD_K

    def stream_specs(length):
        return [
            pl.BlockSpec((None, length, GLA_HEAD_K), lambda i, h: (i, 0, h)),
            pl.BlockSpec((None, length, GLA_HEAD_K), lambda i, h: (i, 0, nk + h)),
            pl.BlockSpec((None, length, GLA_HEAD_V), lambda i, h: (i, 0, nk + h)),
            pl.BlockSpec((None, length, GLA_HEAD_V), lambda i, h: (i, 0, 2 * nk + h)),
            pl.BlockSpec((None, length, LANES), lambda i, h: (i, 0, 0)),
        ]

    def out_spec(length):
        return pl.BlockSpec((None, length, GLA_HEAD_V), lambda i, h: (i, 0, h))

    return pl.pallas_call(
        _gla_kernel,
        grid=(b, GLA_HEADS),
        in_specs=stream_specs(CTX_LEN) + stream_specs(SEQ) + [
            pl.BlockSpec((2, LANES, GLA_HEAD_K), lambda i, h: (0, 0, h)),
            pl.BlockSpec((2, 1, GLA_HEAD_K), lambda i, h: (0, 0, h)),
            pl.BlockSpec((1, GLA_HEAD_V), lambda i, h: (0, 0)),
        ],
        out_specs=[out_spec(CTX_LEN), out_spec(SEQ)],
        out_shape=[
            jax.ShapeDtypeStruct((b, CTX_LEN, GLA_VALUE_DIM), BF16),
            jax.ShapeDtypeStruct((b, SEQ, GLA_VALUE_DIM), BF16),
        ],
        scratch_shapes=[
            pltpu.VMEM((2, GLA_HEAD_K, GLA_HEAD_V), F32),
            pltpu.VMEM((SEQ, GLA_HEAD_V), F32),
        ],
        compiler_params=_cparams(("arbitrary", "arbitrary")),
        name="gla_scan",
    )(qkvr_c, qkvr_c, qkvr_c, qkvr_c, low_c, qkvr_l, qkvr_l, qkvr_l, qkvr_l, low_l, w2, bg, head_gain)


def _fnet_ch_kernel(x_ref, sh_ref, sc_ref, g_ref, w_ref, y1_ref, y2_ref, *, by_phase):
    if by_phase:
        x = jnp.concatenate([x_ref[:, s, :] for s in range(DFT_RADIX)], axis=0)
    else:
        x = x_ref[...]
    h = _modulate(x, g_ref[...], sh_ref[...], sc_ref[...]).astype(BF16)
    w = w_ref[...]
    gd = FNET_GROUP_DIM
    rows = TOK_TILE // DFT_RADIX
    for g in range(FNET_GROUPS):
        r = _dot(h[:, g * gd:(g + 1) * gd], w).astype(BF16)
        cols = slice(g * gd, (g + 1) * gd)
        if by_phase:
            for s in range(DFT_RADIX):
                y1_ref[s, :, cols] = r[s * rows:(s + 1) * rows, :gd]
                y2_ref[s, :, cols] = r[s * rows:(s + 1) * rows, gd:]
        else:
            y1_ref[:, cols] = r[:, :gd]
            y2_ref[:, cols] = r[:, gd:]


def _fnet_channels(x, mods, gain, w_ch, by_phase):
    bq, s, _ = x.shape
    if by_phase:
        rows = TOK_TILE // DFT_RADIX
        x = x.reshape(bq, s // DFT_RADIX, DFT_RADIX, D_MODEL)
        x_spec = pl.BlockSpec((None, rows, DFT_RADIX, D_MODEL), lambda b, t: (b, t, 0, 0))
        y_spec = pl.BlockSpec((None, DFT_RADIX, rows, D_MODEL), lambda b, t: (b, 0, t, 0))
        out = jax.ShapeDtypeStruct((bq, DFT_RADIX, s // DFT_RADIX, D_MODEL), BF16)
    else:
        x_spec = y_spec = _tile_spec(False)
        out = jax.ShapeDtypeStruct((bq, s, D_MODEL), BF16)
    return pl.pallas_call(
        partial(_fnet_ch_kernel, by_phase=by_phase),
        grid=(bq, s // TOK_TILE),
        in_specs=[
            x_spec, _mod_spec(0), _mod_spec(1), _row_spec(),
            pl.BlockSpec((FNET_GROUP_DIM, 2 * FNET_GROUP_DIM), lambda b, t: (0, 0)),
        ],
        out_specs=[y_spec, y_spec],
        out_shape=[out, out],
        compiler_params=_cparams(("arbitrary", "arbitrary")),
        name="fnet_channels",
    )(x, mods, mods, gain, w_ch)


def _fnet_sub_kernel(w_ref, y1_ref, y2_ref, o_ref):
    n = y1_ref.shape[0]
    y1, y2 = y1_ref[...], y2_ref[...]
    k1 = _dot(w_ref[0], y1 + y2)
    o_ref[n:, :] = (-(k1 + _dot(w_ref[1], y1))).astype(BF16)
    o_ref[:n, :] = (k1 - _dot(w_ref[2], y2)).astype(BF16)


def _fnet_sub_dft(w_sub, y1, y2):
    b, radix, n, _ = y1.shape
    y_spec = pl.BlockSpec((None, None, n, D_MODEL), lambda s, i: (i, s, 0, 0))
    return pl.pallas_call(
        _fnet_sub_kernel,
        grid=(radix, b),
        in_specs=[pl.BlockSpec((None, 3, n, n), lambda s, i: (s, 0, 0, 0)), y_spec, y_spec],
        out_specs=pl.BlockSpec((None, None, 2 * n, D_MODEL), lambda s, i: (i, s, 0, 0)),
        out_shape=jax.ShapeDtypeStruct((b, radix, 2 * n, D_MODEL), BF16),
        compiler_params=_cparams(("arbitrary", "arbitrary")),
        name="fnet_sub_dft",
    )(w_sub, y1, y2)


def _fft8_real(xs):
    r2 = float(np.sqrt(0.5))

    def add(a, b):
        return a[0] + b[0], a[1] + b[1]

    def sub(a, b):
        return a[0] - b[0], a[1] - b[1]

    def mul_neg_i(a):
        return a[1], -a[0]

    u = [add(xs[k], xs[k + 4]) for k in range(4)]
    d = [sub(xs[k], xs[k + 4]) for k in range(4)]
    v = [d[0],
         ((d[1][0] + d[1][1]) * r2, (d[1][1] - d[1][0]) * r2),
         mul_neg_i(d[2]),
         ((d[3][1] - d[3][0]) * r2, -(d[3][0] + d[3][1]) * r2)]

    def fft4_real(y):
        p0, p1 = add(y[0], y[2]), add(y[1], y[3])
        q0, q1 = sub(y[0], y[2]), mul_neg_i(sub(y[1], y[3]))
        return [p0[0] + p1[0], q0[0] + q1[0], p0[0] - p1[0], q0[0] - q1[0]]

    even, odd = fft4_real(u), fft4_real(v)
    return [even[0], odd[0], even[1], odd[1], even[2], odd[2], even[3], odd[3]]


def _fnet_combine_kernel(re_ref, im_ref, o_ref):
    bp = re_ref.shape[1]
    sub_rows = 16

    def body(rg, carry):
        rows = pl.ds(pl.multiple_of(rg * sub_rows, sub_rows), sub_rows)
        for lc in range(D_MODEL // LANES):
            lanes = slice(lc * LANES, (lc + 1) * LANES)
            xs = [(re_ref[s, rows, lanes].astype(F32), im_ref[s, rows, lanes].astype(F32))
                  for s in range(DFT_RADIX)]
            for q, z in enumerate(_fft8_real(xs)):
                o_ref[q, rows, lanes] = z.astype(BF16)
        return carry

    lax.fori_loop(0, bp // sub_rows, body, 0)


def _fnet_combine(parts):
    b, radix, n2, _ = parts.shape
    n = n2 // 2
    bp = COMBINE_ROWS
    part_spec = lambda off: pl.BlockSpec((None, radix, bp, D_MODEL), lambda i, p: (i, 0, off + p, 0))
    z = pl.pallas_call(
        _fnet_combine_kernel,
        grid=(b, n // bp),
        in_specs=[part_spec(0), part_spec(n // bp)],
        out_specs=pl.BlockSpec((None, radix, bp, D_MODEL), lambda i, p: (i, 0, p, 0)),
        out_shape=jax.ShapeDtypeStruct((b, radix, n, D_MODEL), BF16),
        compiler_params=_cparams(("arbitrary", "arbitrary")),
        name="fnet_combine",
    )(parts, parts)
    return z.reshape(b, radix * n, D_MODEL)


def _fnet_tok_kernel(gc_ref, gs_ref, y1_ref, y2_ref, o_ref, acc_ref):
    k = pl.program_id(2)

    @pl.when(k == 0)
    def _():
        acc_ref[...] = jnp.zeros_like(acc_ref)

    acc_ref[...] += _dot(gc_ref[...], y1_ref[...]) + _dot(gs_ref[...], y2_ref[...])

    @pl.when(k == pl.num_programs(2) - 1)
    def _():
        o_ref[...] = acc_ref[...].astype(BF16)


def _fnet_tokens(gc, gs, y1, y2):
    b, length, _ = y1.shape
    bm, bk = min(DFT_BM, length), min(DFT_BK, length)
    y_spec = pl.BlockSpec((None, bk, D_MODEL), lambda i, m, k: (i, k, 0))
    g_spec = pl.BlockSpec((bm, bk), lambda i, m, k: (m, k))
    return pl.pallas_call(
        _fnet_tok_kernel,
        grid=(b, length // bm, length // bk),
        in_specs=[g_spec, g_spec, y_spec, y_spec],
        out_specs=pl.BlockSpec((None, bm, D_MODEL), lambda i, m, k: (i, m, 0)),
        out_shape=jax.ShapeDtypeStruct((b, length, D_MODEL), BF16),
        scratch_shapes=[pltpu.VMEM((bm, D_MODEL), F32)],
        compiler_params=_cparams(("arbitrary", "arbitrary", "arbitrary")),
        name="fnet_tokens",
    )(gc, gs, y1, y2)


def _dft_tables(n, split=DFT_TABLE_SPLIT):
    lp = jnp.arange(n, dtype=jnp.int32)[:, None]
    hi = jnp.arange(n // split, dtype=jnp.int32)[None, :]
    lo = jnp.arange(split, dtype=jnp.int32)[None, :]
    ang_hi = ((lp * hi) % (n // split)).astype(F32) * (2.0 * np.pi * split / n)
    ang_lo = ((lp * lo) % n).astype(F32) * (2.0 * np.pi / n)
    ch, sh_, cl, sl = jnp.cos(ang_hi), jnp.sin(ang_hi), jnp.cos(ang_lo), jnp.sin(ang_lo)
    cos = ch[:, :, None] * cl[:, None, :] - sh_[:, :, None] * sl[:, None, :]
    sin = sh_[:, :, None] * cl[:, None, :] + ch[:, :, None] * sl[:, None, :]
    return cos.reshape(n, n), sin.reshape(n, n)


def _out_ffn_kernel(x_ref, a_ref, gtm_ref, shf_ref, scf_ref, gtf_ref, g1_ref, g2_ref, g3_ref,
                    wout_ref, wg_ref, wu_ref, wd_ref, o_ref, *rest, colmajor, emit_bf16):
    h_ref, acc_ref = rest[-2:]
    k = pl.program_id(2)

    x1_ref = o_ref.reshape(TOK_TILE, D_MODEL) if colmajor else o_ref

    @pl.when(k == 0)
    def _():
        y = _dot(a_ref[...], wout_ref[...])
        x1 = _load_tile(x_ref, colmajor) + gtm_ref[...] * (_rms(y) * g1_ref[...])
        x1_ref[...] = x1
        h_ref[...] = _modulate(x1, g2_ref[...], shf_ref[...], scf_ref[...]).astype(BF16)
        acc_ref[...] = jnp.zeros_like(acc_ref)

    wg, wu, wd = wg_ref[...], wu_ref[...], wd_ref[...]
    if emit_bf16:
        wg, wu, wd = wg.astype(BF16), wu.astype(BF16), wd.astype(BF16)
        rest[0][...] = wg
        rest[1][...] = wu
        rest[2][...] = wd
    h = h_ref[...]
    gate = _dot(h, wg)
    up = _dot(h, wu)
    act = (gate * jax.nn.sigmoid(gate) * up).astype(BF16)
    acc_ref[...] += _dot(act, wd)

    @pl.when(k == pl.num_programs(2) - 1)
    def _():
        out = x1_ref[...] + gtf_ref[...] * (_rms(acc_ref[...]) * g3_ref[...])
        _store_tile(o_ref, out, colmajor)


def _out_ffn(x, a, mods, g1, g2, g3, w_out, mix_layer, ffn_w, colmajor, layer=None):
    bq, s, _ = x.shape
    emit = layer is not None
    tf = FF_TILE_F32 if emit else FF_TILE
    nf = D_FF // tf
    if emit:
        assert bq * s == TOK_TILE, "each weight block must be visited exactly once"
        w_gu, w_down = ffn_w
        weights = (w_gu, w_gu, w_down)
        w_specs = [
            pl.BlockSpec((None, D_MODEL, tf), lambda b, t, k: (layer, 0, k)),
            pl.BlockSpec((None, D_MODEL, tf), lambda b, t, k: (layer, 0, nf + k)),
            pl.BlockSpec((None, tf, D_MODEL), lambda b, t, k: (layer, k, 0)),
        ]
    else:
        weights = ffn_w
        w_specs = [
            pl.BlockSpec((None, D_MODEL, tf), lambda b, t, k: (k, 0, 0)),
            pl.BlockSpec((None, D_MODEL, tf), lambda b, t, k: (k, 0, 0)),
            pl.BlockSpec((tf, D_MODEL), lambda b, t, k: (k, 0)),
        ]
    out_specs = [_tile_spec(colmajor)]
    out_shape = [jax.ShapeDtypeStruct(_as_stream(x, colmajor).shape, F32)]
    if emit:
        per = FF_TILE // tf
        out_specs += [
            pl.BlockSpec((None, D_MODEL, tf), lambda b, t, k: (k // per, 0, k % per)),
            pl.BlockSpec((None, D_MODEL, tf), lambda b, t, k: (k // per, 0, k % per)),
            pl.BlockSpec((tf, D_MODEL), lambda b, t, k: (k, 0)),
        ]
        out_shape += [
            jax.ShapeDtypeStruct((D_FF // FF_TILE, D_MODEL, FF_TILE), BF16),
            jax.ShapeDtypeStruct((D_FF // FF_TILE, D_MODEL, FF_TILE), BF16),
            jax.ShapeDtypeStruct((D_FF, D_MODEL), BF16),
        ]
    res = pl.pallas_call(
        partial(_out_ffn_kernel, colmajor=colmajor, emit_bf16=emit),
        grid=(bq, s // TOK_TILE, nf),
        in_specs=[
            _tile_spec(colmajor), _tile_spec(False),
            _mod_spec(2), _mod_spec(3), _mod_spec(4), _mod_spec(5),
            _row_spec(), _row_spec(), _row_spec(),
            pl.BlockSpec((None, D_MODEL, D_MODEL), lambda b, t, k: (mix_layer, 0, 0),
                         pipeline_mode=pl.Buffered(1)),
        ] + w_specs,
        out_specs=out_specs,
        out_shape=out_shape,
        scratch_shapes=[
            pltpu.VMEM((TOK_TILE, D_MODEL), BF16),
            pltpu.VMEM((TOK_TILE, D_MODEL), F32),
        ],
        compiler_params=_cparams(("arbitrary", "arbitrary", "arbitrary")),
        name="out_ffn",
    )(_as_stream(x, colmajor), a, mods, mods, mods, mods, g1, g2, g3, w_out, *weights)
    if emit:
        return res[0].reshape(x.shape), tuple(res[1:])
    return res[0].reshape(x.shape)


def _cast_gu_kernel(g_ref, u_ref, go_ref, uo_ref):
    go_ref[...] = g_ref[...].astype(BF16)
    uo_ref[...] = u_ref[...].astype(BF16)


def _cast_gate_up(w_gu, layer):
    nf = D_FF // FF_TILE
    out = jax.ShapeDtypeStruct((nf, D_MODEL, FF_TILE), BF16)
    o_spec = pl.BlockSpec((None, D_MODEL, FF_TILE), lambda k: (k, 0, 0))
    return pl.pallas_call(
        _cast_gu_kernel,
        grid=(nf,),
        in_specs=[
            pl.BlockSpec((None, D_MODEL, FF_TILE), lambda k: (layer, 0, k)),
            pl.BlockSpec((None, D_MODEL, FF_TILE), lambda k: (layer, 0, nf + k)),
        ],
        out_specs=[o_spec, o_spec],
        out_shape=[out, out],
        compiler_params=_cparams(("arbitrary",)),
        name="cast_gate_up",
    )(w_gu, w_gu)


def kernel(x, c, ctx, c_ctx, ada_w, ada_b, norm_gains, gla_w_in, gla_wg2_f, gla_bg_f, gla_wg2_b, gla_bg_b,
           gla_head_gain, gla_w_out, fnet_w_out, ffn_w_gu, ffn_w_down):
    batch = x.shape[0]
    cond = jnp.zeros((ADA_ROWS, D_MODEL), F32).at[:batch].set(c).at[batch].set(c_ctx)
    mods = _ada_rows(cond, ada_w, ada_b)
    ctx_s = ctx.reshape(1, batch * CTX_LEN, D_MODEL)

    n_ch = FNET_GROUP_DIM
    cos_c, sin_c = _dft_tables(n_ch)
    w_ch = (jnp.concatenate([cos_c, sin_c], axis=1) * (n_ch ** -0.5)).astype(BF16)
    cos_t, sin_t = _dft_tables(CTX_LEN)
    ctx_tables = ((cos_t * (CTX_LEN ** -0.5)).astype(BF16), (sin_t * -(CTX_LEN ** -0.5)).astype(BF16))
    n_sub = SEQ // DFT_RADIX
    cos_s, sin_s = _dft_tables(n_sub)
    tw_ang = ((jnp.arange(DFT_RADIX, dtype=jnp.int32)[:, None] * jnp.arange(n_sub, dtype=jnp.int32)[None, :])
              % SEQ).astype(F32) * (2.0 * np.pi / SEQ)
    tw_cos, tw_sin = jnp.cos(tw_ang)[:, :, None], jnp.sin(tw_ang)[:, :, None]
    cos_p = cos_s[None] * tw_cos - sin_s[None] * tw_sin
    sin_p = sin_s[None] * tw_cos + cos_s[None] * tw_sin
    w_sub = (jnp.stack([cos_p, sin_p - cos_p, cos_p + sin_p], axis=1) * (SEQ ** -0.5)).astype(BF16)

    gla_w_out_bf = gla_w_out.astype(BF16)
    fnet_w_out_bf = fnet_w_out.astype(BF16)

    for i in range(DEPTH):
        need_ctx = i < DEPTH - 1
        j = i // 2
        mod_lat = mods[i, :batch].reshape(batch, 1, N_ADA * D_MODEL)
        mod_ctx = mods[i, batch:batch + 1].reshape(1, 1, N_ADA * D_MODEL)
        gains = [norm_gains[i, n].reshape(1, D_MODEL) for n in range(4)]
        colmajor = False
        if i % 2 == 0:
            colmajor = j % 2 == 1
            w_low = lax.optimization_barrier(gla_w_in[j, :, GLA_MAIN_DIM:])
            w_low = jnp.pad(w_low, ((0, 0), (0, LANES - 2 * GLA_GATE_RANK))).astype(BF16)
            w2 = jnp.zeros((2, LANES, GLA_KEY_DIM), F32)
            w2 = w2.at[0, :GLA_GATE_RANK].set(gla_wg2_f[j]).at[1, GLA_GATE_RANK:2 * GLA_GATE_RANK].set(gla_wg2_b[j])
            bg = jnp.stack([gla_bg_f[j], gla_bg_b[j]]).reshape(2, 1, GLA_KEY_DIM)
            qkvr_c, low_c, w_in_bf = _gla_in(ctx_s, mod_ctx, gains[0], jnp.swapaxes(gla_w_in, 1, 2), w_low,
                                             False, layer=j)
            qkvr_l, low_l = _gla_in(x, mod_lat, gains[0], w_in_bf, w_low, colmajor)
            a_ctx, a_lat = _gla_scan(
                qkvr_c.reshape(batch, CTX_LEN, GLA_MAIN_DIM), low_c.reshape(batch, CTX_LEN, LANES),
                qkvr_l, low_l, w2.astype(BF16), bg, gla_head_gain[j].reshape(1, GLA_HEAD_V))
            w_mix = gla_w_out_bf
        else:
            y1, y2 = _fnet_channels(x, mod_lat, gains[0], w_ch, True)
            a_lat = _fnet_combine(_fnet_sub_dft(w_sub, y1, y2))
            if need_ctx:
                y1, y2 = _fnet_channels(ctx_s, mod_ctx, gains[0], w_ch, False)
                a_ctx = _fnet_tokens(*ctx_tables, y1.reshape(batch, CTX_LEN, D_MODEL),
                                     y2.reshape(batch, CTX_LEN, D_MODEL))
            w_mix = fnet_w_out_bf
        if need_ctx:
            ctx_s, ffn_bf = _out_ffn(ctx_s, a_ctx.reshape(1, batch * CTX_LEN, D_MODEL), mod_ctx,
                                     gains[1], gains[2], gains[3], w_mix, j, (ffn_w_gu, ffn_w_down), False,
                                     layer=i)
        else:
            ffn_bf = (*_cast_gate_up(ffn_w_gu, i), ffn_w_down[i].astype(BF16))
        x = _out_ffn(x, a_lat, mod_lat, gains[1], gains[2], gains[3], w_mix, j, ffn_bf, colmajor)
    return x
```

```python
from functools import partial

import jax
import jax.numpy as jnp
import numpy as np
from jax import lax
from jax.experimental import pallas as pl
from jax.experimental.pallas import tpu as pltpu

D_MODEL = 2048
SEQ = 4096
CTX_LEN = 256
GRID_W = 64
DEPTH = 4
GLA_HEADS = 4
GLA_HEAD_K = 256
GLA_HEAD_V = 512
GLA_KEY_DIM = GLA_HEADS * GLA_HEAD_K
GLA_VALUE_DIM = GLA_HEADS * GLA_HEAD_V
GLA_GATE_RANK = 16
GLA_GATE_TAU = 16.0
GLA_MAIN_DIM = 2 * GLA_KEY_DIM + 2 * GLA_VALUE_DIM
FNET_GROUPS = 4
FNET_GROUP_DIM = D_MODEL // FNET_GROUPS
D_FF = 5632
N_ADA = 6
EPS = 1e-6

LANES = 128
ADA_ROWS = 8
VMEM_LIMIT = 56 * 1024 * 1024

TOK_TILE = 512
IN_TOK_TILE = 1024
FNET_TOK_TILE = 1024
FF_TILE = 512
IN_TILE = 1536
IN_TILE_F32 = 768
FF_TILE_F32 = 256
ADA_TILE = 1536
GLA_CHUNK = 128
GLA_UNROLL = 4
DFT_BM = 1024
DFT_BK = 512
DFT_RADIX = 8
COMBINE_ROWS = 256

BF16 = jnp.bfloat16
F32 = jnp.float32


def _cparams(sem):
    return pltpu.CompilerParams(dimension_semantics=sem, vmem_limit_bytes=VMEM_LIMIT)


def _dot(a, b):
    return jnp.dot(a, b, preferred_element_type=F32)


def _rms(x):
    return x * lax.rsqrt(jnp.mean(x * x, axis=-1, keepdims=True) + EPS)


def _modulate(x, gain, shift, scale):
    return (_rms(x) * gain) * (1.0 + scale) + shift


def _ada_kernel(c_ref, w_ref, b_ref, o_ref):
    c = c_ref[...]
    s = (c * jax.nn.sigmoid(c)).astype(BF16)
    o_ref[...] = _dot(s, w_ref[...].astype(BF16)) + b_ref[...]


def _ada_rows(cond, ada_w, ada_b):
    n_out = N_ADA * D_MODEL
    return pl.pallas_call(
        _ada_kernel,
        grid=(DEPTH, n_out // ADA_TILE),
        in_specs=[
            pl.BlockSpec((ADA_ROWS, D_MODEL), lambda i, n: (0, 0)),
            pl.BlockSpec((None, D_MODEL, ADA_TILE), lambda i, n: (i, 0, n)),
            pl.BlockSpec((None, 1, ADA_TILE), lambda i, n: (i, 0, n)),
        ],
        out_specs=pl.BlockSpec((None, ADA_ROWS, ADA_TILE), lambda i, n: (i, 0, n)),
        out_shape=jax.ShapeDtypeStruct((DEPTH, ADA_ROWS, n_out), F32),
        compiler_params=_cparams(("arbitrary", "arbitrary")),
        name="ada_rows",
    )(cond, ada_w, ada_b.reshape(DEPTH, 1, n_out))


def _tile_spec(colmajor, rows=TOK_TILE):
    if colmajor:
        return pl.BlockSpec((None, GRID_W, rows // GRID_W, D_MODEL), lambda b, t, *_: (b, 0, t, 0))
    return pl.BlockSpec((None, rows, D_MODEL), lambda b, t, *_: (b, t, 0))


def _load_tile(x_ref, colmajor):
    if not colmajor:
        return x_ref[...]
    return jnp.concatenate([x_ref[:, c, :] for c in range(x_ref.shape[1])], axis=0)


def _store_tile(o_ref, val, colmajor):
    if not colmajor:
        o_ref[...] = val
        return
    for c in range(o_ref.shape[1]):
        o_ref[:, c, :] = val[c * GRID_W:(c + 1) * GRID_W, :]


def _mod_spec(j):
    return pl.BlockSpec((None, 1, D_MODEL), lambda b, t, *_: (b, 0, j))


def _row_spec():
    return pl.BlockSpec((1, D_MODEL), lambda b, t, *_: (0, 0))


def _as_stream(x, colmajor):
    if colmajor:
        bq, s, d = x.shape
        return x.reshape(bq, GRID_W, s // GRID_W, d)
    return x


def _gla_in_kernel(x_ref, sh_ref, sc_ref, g_ref, w_ref, wlow_ref, o_ref, low_ref, *rest, colmajor, emit_bf16):
    h_ref = rest[-1]
    n = pl.program_id(2)

    @pl.when(n == 0)
    def _():
        h = _modulate(_load_tile(x_ref, colmajor), g_ref[...], sh_ref[...], sc_ref[...]).astype(BF16)
        h_ref[...] = h
        low_ref[...] = _dot(h, wlow_ref[...])

    w = w_ref[...]
    if emit_bf16:
        w = w.T.astype(BF16)
        rest[0][...] = w
    o_ref[...] = _dot(h_ref[...], w).astype(BF16)


def _gla_in(x, mods, gain, w_in, w_low, colmajor, layer=None):
    bq, s, _ = x.shape
    emit = layer is not None
    tn = IN_TILE_F32 if emit else IN_TILE
    tm = min(IN_TOK_TILE, s)
    if emit:
        assert bq * s == tm, "each weight block must be visited exactly once"
        w_spec = pl.BlockSpec((None, tn, D_MODEL), lambda b, t, n: (layer, n, 0))
    else:
        w_spec = pl.BlockSpec((None, D_MODEL, tn), lambda b, t, n: (n, 0, 0))
    out_specs = [
        pl.BlockSpec((None, tm, tn), lambda b, t, n: (b, t, n)),
        pl.BlockSpec((None, tm, LANES), lambda b, t, n: (b, t, 0)),
    ]
    out_shape = [
        jax.ShapeDtypeStruct((bq, s, GLA_MAIN_DIM), BF16),
        jax.ShapeDtypeStruct((bq, s, LANES), F32),
    ]
    if emit:
        per = IN_TILE // tn
        out_specs.append(pl.BlockSpec((None, D_MODEL, tn), lambda b, t, n: (n // per, 0, n % per)))
        out_shape.append(jax.ShapeDtypeStruct((GLA_MAIN_DIM // IN_TILE, D_MODEL, IN_TILE), BF16))
    return pl.pallas_call(
        partial(_gla_in_kernel, colmajor=colmajor, emit_bf16=emit),
        grid=(bq, s // tm, GLA_MAIN_DIM // tn),
        in_specs=[
            _tile_spec(colmajor, tm), _mod_spec(0), _mod_spec(1), _row_spec(),
            w_spec,
            pl.BlockSpec((D_MODEL, LANES), lambda b, t, n: (0, 0)),
        ],
        out_specs=out_specs,
        out_shape=out_shape,
        scratch_shapes=[pltpu.VMEM((tm, D_MODEL), BF16)],
        compiler_params=_cparams(("arbitrary", "arbitrary", "arbitrary")),
        name="gla_in",
    )(_as_stream(x, colmajor), mods, mods, gain, w_in, w_low)


def _log_gate(z):
    return (jnp.minimum(z, 0.0) - jnp.log(1.0 + jnp.exp(-jnp.abs(z)))) * (1.0 / GLA_GATE_TAU)


def _gla_group(q_ref, k_ref, v_ref, low_ref, w2_ref, bg_ref, tris, masks, s_ref, starts, u):
    c, dk = GLA_CHUNK, GLA_HEAD_K
    span = [pl.ds(pl.multiple_of(starts[d], c), u * c) for d in (0, 1)]
    z = [_dot(low_ref[span[d], :].astype(BF16), w2_ref[d]) + bg_ref[d] for d in (0, 1)]
    g = [_log_gate(zd) for zd in z]
    cums = []
    for d in (0, 1):
        parts = []
        for j in range(u):
            gj = g[d][j * c:(j + 1) * c, :]
            hi = gj.astype(BF16)
            parts += [hi, (gj - hi.astype(F32)).astype(BF16)]
        cum = _dot(tris[d], jnp.concatenate(parts, axis=1))
        cums.append([cum[:, 2 * j * dk:(2 * j + 1) * dk] + cum[:, (2 * j + 1) * dk:(2 * j + 2) * dk]
                     for j in range(u)])
    order = [(d, j if d == 0 else u - 1 - j) for j in range(u) for d in (0, 1)]
    work = {}
    for d, j in order:
        rows = pl.ds(pl.multiple_of(starts[d] + j * c, c), c)
        cum = cums[d][j]
        if d == 1:
            ref, tot = cum[c // 2:c // 2 + 1, :], cum[0:1, :]
        else:
            ref, tot = cum[c // 2 - 1:c // 2, :], cum[c - 1:c, :]
        qf = q_ref[rows, :].astype(F32) * (GLA_HEAD_K ** -0.5)
        kf = k_ref[rows, :].astype(F32)
        work[d, j] = dict(
            rows=rows, tot=tot, v=v_ref[rows, :],
            q_mid=(qf * jnp.exp(cum - ref)).astype(BF16), k_mid=(kf * jnp.exp(ref - cum)).astype(BF16),
            q_dec=(qf * jnp.exp(cum)).astype(BF16), k_end=(kf * jnp.exp(tot - cum)).astype(BF16))
    for key in order:
        w = work[key]
        s = lax.dot_general(w["q_mid"], w["k_mid"], (((1,), (1,)), ((), ())), preferred_element_type=F32)
        w["scores"] = jnp.where(masks[key[0]], s, 0.0).astype(BF16)
    for key in order:
        w = work[key]
        w["o"] = _dot(w["scores"], w["v"])
        w["ds"] = lax.dot_general(w["k_end"], w["v"], (((0,), (0,)), ((), ())), preferred_element_type=F32)
        dec = jnp.broadcast_to(jnp.exp(w["tot"]), (LANES, dk)).T
        w["dec"] = jnp.concatenate([dec] * (GLA_HEAD_V // LANES), axis=1)
    out = []
    state = [s_ref[0], s_ref[1]]
    for d, j in order:
        w = work[d, j]
        out.append((w["rows"], w["o"] + _dot(w["q_dec"], state[d].astype(BF16))))
        state[d] = state[d] * w["dec"] + w["ds"]
    s_ref[0] = state[0]
    s_ref[1] = state[1]
    return out


def _gla_kernel(qc_ref, kc_ref, vc_ref, rc_ref, lowc_ref, ql_ref, kl_ref, vl_ref, rl_ref, lowl_ref,
                w2_ref, bg_ref, hg_ref, oc_ref, ol_ref, s_ref, acc_ref):
    c = GLA_CHUNK
    row = lax.broadcasted_iota(jnp.int32, (c, c), 0)
    col = lax.broadcasted_iota(jnp.int32, (c, c), 1)
    masks = (col <= row, col >= row)
    tris = tuple(m.astype(BF16) for m in masks)
    hg = hg_ref[...]
    s_ref[...] = jnp.zeros_like(s_ref)

    def phase(q_ref, k_ref, v_ref, r_ref, low_ref, o_ref, length):
        u = min(GLA_UNROLL, length // (2 * c))
        n_groups = length // (u * c)
        half = n_groups // 2

        def group(n):
            starts = (n * (u * c), (n_groups - 1 - n) * (u * c))
            return _gla_group(q_ref, k_ref, v_ref, low_ref, w2_ref, bg_ref, tris, masks, s_ref, starts, u)

        def first_visit(n, carry):
            for rows, o in group(n):
                acc_ref[rows, :] = o
            return carry

        def second_visit(n, carry):
            for rows, o in group(n):
                o = _rms(o + acc_ref[rows, :]) * hg
                r = r_ref[rows, :].astype(F32)
                o_ref[rows, :] = (o * (r * jax.nn.sigmoid(r))).astype(BF16)
            return carry

        lax.fori_loop(0, half, first_visit, 0)
        lax.fori_loop(half, n_groups, second_visit, 0)

    phase(qc_ref, kc_ref, vc_ref, rc_ref, lowc_ref, oc_ref, CTX_LEN)
    phase(ql_ref, kl_ref, vl_ref, rl_ref, lowl_ref, ol_ref, SEQ)


def _gla_scan(qkvr_c, low_c, qkvr_l, low_l, w2, bg, head_gain):
    b = qkvr_l.shape[0]
    nk = GLA_KEY_DIM // GLA_HEAD_K

    def stream_specs(length):
        return [
            pl.BlockSpec((None, length, GLA_HEAD_K), lambda i, h: (i, 0, h)),
            pl.BlockSpec((None, length, GLA_HEAD_K), lambda i, h: (i, 0, nk + h)),
            pl.BlockSpec((None, length, GLA_HEAD_V), lambda i, h: (i, 0, nk + h)),
            pl.BlockSpec((None, length, GLA_HEAD_V), lambda i, h: (i, 0, 2 * nk + h)),
            pl.BlockSpec((None, length, LANES), lambda i, h: (i, 0, 0)),
        ]

    def out_spec(length):
        return pl.BlockSpec((None, length, GLA_HEAD_V), lambda i, h: (i, 0, h))

    return pl.pallas_call(
        _gla_kernel,
        grid=(b, GLA_HEADS),
        in_specs=stream_specs(CTX_LEN) + stream_specs(SEQ) + [
            pl.BlockSpec((2, LANES, GLA_HEAD_K), lambda i, h: (0, 0, h)),
            pl.BlockSpec((2, 1, GLA_HEAD_K), lambda i, h: (0, 0, h)),
            pl.BlockSpec((1, GLA_HEAD_V), lambda i, h: (0, 0)),
        ],
        out_specs=[out_spec(CTX_LEN), out_spec(SEQ)],
        out_shape=[
            jax.ShapeDtypeStruct((b, CTX_LEN, GLA_VALUE_DIM), BF16),
            jax.ShapeDtypeStruct((b, SEQ, GLA_VALUE_DIM), BF16),
        ],
        scratch_shapes=[
            pltpu.VMEM((2, GLA_HEAD_K, GLA_HEAD_V), F32),
            pltpu.VMEM((SEQ, GLA_HEAD_V), F32),
        ],
        compiler_params=_cparams(("arbitrary", "arbitrary")),
        name="gla_scan",
    )(qkvr_c, qkvr_c, qkvr_c, qkvr_c, low_c, qkvr_l, qkvr_l, qkvr_l, qkvr_l, low_l, w2, bg, head_gain)


def _fnet_ch_kernel(x_ref, sh_ref, sc_ref, g_ref, w_ref, y1_ref, y2_ref, *, by_phase):
    if by_phase:
        x = jnp.concatenate([x_ref[:, s, :] for s in range(DFT_RADIX)], axis=0)
    else:
        x = x_ref[...]
    h = _modulate(x, g_ref[...], sh_ref[...], sc_ref[...]).astype(BF16)
    w = w_ref[...]
    gd = FNET_GROUP_DIM
    rows = x_ref.shape[0]
    for g in range(FNET_GROUPS):
        r = _dot(h[:, g * gd:(g + 1) * gd], w).astype(BF16)
        cols = slice(g * gd, (g + 1) * gd)
        if by_phase:
            for s in range(DFT_RADIX):
                y1_ref[s, :, cols] = r[s * rows:(s + 1) * rows, :gd]
                y2_ref[s, :, cols] = r[s * rows:(s + 1) * rows, gd:]
        else:
            y1_ref[:, cols] = r[:, :gd]
            y2_ref[:, cols] = r[:, gd:]


def _fnet_channels(x, mods, gain, w_ch, by_phase):
    bq, s, _ = x.shape
    tm = FNET_TOK_TILE if by_phase else TOK_TILE
    if by_phase:
        rows = tm // DFT_RADIX
        x = x.reshape(bq, s // DFT_RADIX, DFT_RADIX, D_MODEL)
        x_spec = pl.BlockSpec((None, rows, DFT_RADIX, D_MODEL), lambda b, t: (b, t, 0, 0))
        y_spec = pl.BlockSpec((None, DFT_RADIX, rows, D_MODEL), lambda b, t: (b, 0, t, 0))
        out = jax.ShapeDtypeStruct((bq, DFT_RADIX, s // DFT_RADIX, D_MODEL), BF16)
    else:
        x_spec = y_spec = _tile_spec(False)
        out = jax.ShapeDtypeStruct((bq, s, D_MODEL), BF16)
    return pl.pallas_call(
        partial(_fnet_ch_kernel, by_phase=by_phase),
        grid=(bq, s // tm),
        in_specs=[
            x_spec, _mod_spec(0), _mod_spec(1), _row_spec(),
            pl.BlockSpec((FNET_GROUP_DIM, 2 * FNET_GROUP_DIM), lambda b, t: (0, 0)),
        ],
        out_specs=[y_spec, y_spec],
        out_shape=[out, out],
        compiler_params=_cparams(("arbitrary", "arbitrary")),
        name="fnet_channels",
    )(x, mods, mods, gain, w_ch)


def _fnet_sub_kernel(w_ref, y1_ref, y2_ref, o_ref):
    n = y1_ref.shape[0]
    y1, y2 = y1_ref[...], y2_ref[...]
    k1 = _dot(w_ref[0], y1 + y2)
    o_ref[n:, :] = (-(k1 + _dot(w_ref[1], y1))).astype(BF16)
    o_ref[:n, :] = (k1 - _dot(w_ref[2], y2)).astype(BF16)


def _fnet_sub_dft(w_sub, y1, y2):
    b, radix, n, _ = y1.shape
    y_spec = pl.BlockSpec((None, None, n, D_MODEL), lambda i, s: (i, s, 0, 0))
    return pl.pallas_call(
        _fnet_sub_kernel,
        grid=(b, radix),
        in_specs=[pl.BlockSpec((None, 3, n, n), lambda i, s: (s, 0, 0, 0)), y_spec, y_spec],
        out_specs=pl.BlockSpec((None, None, 2 * n, D_MODEL), lambda i, s: (i, s, 0, 0)),
        out_shape=jax.ShapeDtypeStruct((b, radix, 2 * n, D_MODEL), BF16),
        compiler_params=_cparams(("arbitrary", "arbitrary")),
        name="fnet_sub_dft",
    )(w_sub, y1, y2)


def _fft8_real(xs):
    r2 = float(np.sqrt(0.5))

    def add(a, b):
        return a[0] + b[0], a[1] + b[1]

    def sub(a, b):
        return a[0] - b[0], a[1] - b[1]

    def mul_neg_i(a):
        return a[1], -a[0]

    u = [add(xs[k], xs[k + 4]) for k in range(4)]
    d = [sub(xs[k], xs[k + 4]) for k in range(4)]
    v = [d[0],
         ((d[1][0] + d[1][1]) * r2, (d[1][1] - d[1][0]) * r2),
         mul_neg_i(d[2]),
         ((d[3][1] - d[3][0]) * r2, -(d[3][0] + d[3][1]) * r2)]

    def fft4_real(y):
        p0, p1 = add(y[0], y[2]), add(y[1], y[3])
        q0, q1 = sub(y[0], y[2]), mul_neg_i(sub(y[1], y[3]))
        return [p0[0] + p1[0], q0[0] + q1[0], p0[0] - p1[0], q0[0] - q1[0]]

    even, odd = fft4_real(u), fft4_real(v)
    return [even[0], odd[0], even[1], odd[1], even[2], odd[2], even[3], odd[3]]


def _fnet_combine_kernel(re_ref, im_ref, o_ref):
    bp = re_ref.shape[1]
    sub_rows = 16

    def body(rg, carry):
        rows = pl.ds(pl.multiple_of(rg * sub_rows, sub_rows), sub_rows)
        for lc in range(D_MODEL // LANES):
            lanes = slice(lc * LANES, (lc + 1) * LANES)
            xs = [(re_ref[s, rows, lanes].astype(F32), im_ref[s, rows, lanes].astype(F32))
                  for s in range(DFT_RADIX)]
            for q, z in enumerate(_fft8_real(xs)):
                o_ref[q, rows, lanes] = z.astype(BF16)
        return carry

    lax.fori_loop(0, bp // sub_rows, body, 0)


def _fnet_combine(parts):
    b, radix, n2, _ = parts.shape
    n = n2 // 2
    bp = COMBINE_ROWS
    part_spec = lambda off: pl.BlockSpec((None, radix, bp, D_MODEL), lambda i, p: (i, 0, off + p, 0))
    z = pl.pallas_call(
        _fnet_combine_kernel,
        grid=(b, n // bp),
        in_specs=[part_spec(0), part_spec(n // bp)],
        out_specs=pl.BlockSpec((None, radix, bp, D_MODEL), lambda i, p: (i, 0, p, 0)),
        out_shape=jax.ShapeDtypeStruct((b, radix, n, D_MODEL), BF16),
        compiler_params=_cparams(("arbitrary", "arbitrary")),
        name="fnet_combine",
    )(parts, parts)
    return z.reshape(b, radix * n, D_MODEL)


def _fnet_tok_kernel(gc_ref, gs_ref, y1_ref, y2_ref, o_ref, acc_ref):
    k = pl.program_id(2)

    @pl.when(k == 0)
    def _():
        acc_ref[...] = jnp.zeros_like(acc_ref)

    acc_ref[...] += _dot(gc_ref[...], y1_ref[...]) + _dot(gs_ref[...], y2_ref[...])

    @pl.when(k == pl.num_programs(2) - 1)
    def _():
        o_ref[...] = acc_ref[...].astype(BF16)


def _fnet_tokens(gc, gs, y1, y2):
    b, length, _ = y1.shape
    bm, bk = min(DFT_BM, length), min(DFT_BK, length)
    y_spec = pl.BlockSpec((None, bk, D_MODEL), lambda i, m, k: (i, k, 0))
    g_spec = pl.BlockSpec((bm, bk), lambda i, m, k: (m, k))
    return pl.pallas_call(
        _fnet_tok_kernel,
        grid=(b, length // bm, length // bk),
        in_specs=[g_spec, g_spec, y_spec, y_spec],
        out_specs=pl.BlockSpec((None, bm, D_MODEL), lambda i, m, k: (i, m, 0)),
        out_shape=jax.ShapeDtypeStruct((b, length, D_MODEL), BF16),
        scratch_shapes=[pltpu.VMEM((bm, D_MODEL), F32)],
        compiler_params=_cparams(("arbitrary", "arbitrary", "arbitrary")),
        name="fnet_tokens",
    )(gc, gs, y1, y2)


def _dft_tables(n, split):
    lp = jnp.arange(n, dtype=jnp.int32)[:, None]
    hi = jnp.arange(n // split, dtype=jnp.int32)[None, :]
    lo = jnp.arange(split, dtype=jnp.int32)[None, :]
    ang_hi = ((lp * hi) % (n // split)).astype(F32) * (2.0 * np.pi * split / n)
    ang_lo = ((lp * lo) % n).astype(F32) * (2.0 * np.pi / n)
    ch, sh_, cl, sl = jnp.cos(ang_hi), jnp.sin(ang_hi), jnp.cos(ang_lo), jnp.sin(ang_lo)
    cos = ch[:, :, None] * cl[:, None, :] - sh_[:, :, None] * sl[:, None, :]
    sin = sh_[:, :, None] * cl[:, None, :] + ch[:, :, None] * sl[:, None, :]
    return cos.reshape(n, n), sin.reshape(n, n)


def _out_ffn_kernel(x_ref, a_ref, gtm_ref, shf_ref, scf_ref, gtf_ref, g1_ref, g2_ref, g3_ref,
                    wout_ref, wg_ref, wu_ref, wd_ref, o_ref, *rest, colmajor, emit_bf16):
    h_ref, acc_ref = rest[-2:]
    k = pl.program_id(2)

    x1_ref = o_ref.reshape(TOK_TILE, D_MODEL) if colmajor else o_ref

    @pl.when(k == 0)
    def _():
        y = _dot(a_ref[...], wout_ref[...])
        x1 = _load_tile(x_ref, colmajor) + gtm_ref[...] * (_rms(y) * g1_ref[...])
        x1_ref[...] = x1
        h_ref[...] = _modulate(x1, g2_ref[...], shf_ref[...], scf_ref[...]).astype(BF16)
        acc_ref[...] = jnp.zeros_like(acc_ref)

    wg, wu, wd = wg_ref[...], wu_ref[...], wd_ref[...]
    if emit_bf16:
        wg, wu, wd = wg.astype(BF16), wu.astype(BF16), wd.astype(BF16)
        rest[0][...] = wg
        rest[1][...] = wu
        rest[2][...] = wd
    h = h_ref[...]
    gate = _dot(h, wg)
    up = _dot(h, wu)
    act = (gate * jax.nn.sigmoid(gate) * up).astype(BF16)
    acc_ref[...] += _dot(act, wd)

    @pl.when(k == pl.num_programs(2) - 1)
    def _():
        out = x1_ref[...] + gtf_ref[...] * (_rms(acc_ref[...]) * g3_ref[...])
        _store_tile(o_ref, out, colmajor)


def _out_ffn(x, a, mods, g1, g2, g3, w_out, mix_layer, ffn_w, colmajor, layer=None):
    bq, s, _ = x.shape
    emit = layer is not None
    tf = FF_TILE_F32 if emit else FF_TILE
    nf = D_FF // tf
    if emit:
        assert bq * s == TOK_TILE, "each weight block must be visited exactly once"
        w_gu, w_down = ffn_w
        weights = (w_gu, w_gu, w_down)
        w_specs = [
            pl.BlockSpec((None, D_MODEL, tf), lambda b, t, k: (layer, 0, k)),
            pl.BlockSpec((None, D_MODEL, tf), lambda b, t, k: (layer, 0, nf + k)),
            pl.BlockSpec((None, tf, D_MODEL), lambda b, t, k: (layer, k, 0)),
        ]
    else:
        weights = ffn_w
        w_specs = [
            pl.BlockSpec((None, D_MODEL, tf), lambda b, t, k: (k, 0, 0)),
            pl.BlockSpec((None, D_MODEL, tf), lambda b, t, k: (k, 0, 0)),
            pl.BlockSpec((tf, D_MODEL), lambda b, t, k: (k, 0)),
        ]
    out_specs = [_tile_spec(colmajor)]
    out_shape = [jax.ShapeDtypeStruct(_as_stream(x, colmajor).shape, F32)]
    if emit:
        per = FF_TILE // tf
        out_specs += [
            pl.BlockSpec((None, D_MODEL, tf), lambda b, t, k: (k // per, 0, k % per)),
            pl.BlockSpec((None, D_MODEL, tf), lambda b, t, k: (k // per, 0, k % per)),
            pl.BlockSpec((tf, D_MODEL), lambda b, t, k: (k, 0)),
        ]
        out_shape += [
            jax.ShapeDtypeStruct((D_FF // FF_TILE, D_MODEL, FF_TILE), BF16),
            jax.ShapeDtypeStruct((D_FF // FF_TILE, D_MODEL, FF_TILE), BF16),
            jax.ShapeDtypeStruct((D_FF, D_MODEL), BF16),
        ]
    res = pl.pallas_call(
        partial(_out_ffn_kernel, colmajor=colmajor, emit_bf16=emit),
        grid=(bq, s // TOK_TILE, nf),
        in_specs=[
            _tile_spec(colmajor), _tile_spec(False),
            _mod_spec(2), _mod_spec(3), _mod_spec(4), _mod_spec(5),
            _row_spec(), _row_spec(), _row_spec(),
            pl.BlockSpec((None, D_MODEL, D_MODEL), lambda b, t, k: (mix_layer, 0, 0),
                         pipeline_mode=pl.Buffered(1)),
        ] + w_specs,
        out_specs=out_specs,
        out_shape=out_shape,
        scratch_shapes=[
            pltpu.VMEM((TOK_TILE, D_MODEL), BF16),
            pltpu.VMEM((TOK_TILE, D_MODEL), F32),
        ],
        compiler_params=_cparams(("arbitrary", "arbitrary", "arbitrary")),
        name="out_ffn",
    )(_as_stream(x, colmajor), a, mods, mods, mods, mods, g1, g2, g3, w_out, *weights)
    if emit:
        return res[0].reshape(x.shape), tuple(res[1:])
    return res[0].reshape(x.shape)


def _cast_gu_kernel(g_ref, u_ref, go_ref, uo_ref):
    go_ref[...] = g_ref[...].astype(BF16)
    uo_ref[...] = u_ref[...].astype(BF16)


def _cast_gate_up(w_gu, layer):
    nf = D_FF // FF_TILE
    out = jax.ShapeDtypeStruct((nf, D_MODEL, FF_TILE), BF16)
    o_spec = pl.BlockSpec((None, D_MODEL, FF_TILE), lambda k: (k, 0, 0))
    return pl.pallas_call(
        _cast_gu_kernel,
        grid=(nf,),
        in_specs=[
            pl.BlockSpec((None, D_MODEL, FF_TILE), lambda k: (layer, 0, k)),
            pl.BlockSpec((None, D_MODEL, FF_TILE), lambda k: (layer, 0, nf + k)),
        ],
        out_specs=[o_spec, o_spec],
        out_shape=[out, out],
        compiler_params=_cparams(("arbitrary",)),
        name="cast_gate_up",
    )(w_gu, w_gu)


def kernel(x, c, ctx, c_ctx, ada_w, ada_b, norm_gains, gla_w_in, gla_wg2_f, gla_bg_f, gla_wg2_b, gla_bg_b,
           gla_head_gain, gla_w_out, fnet_w_out, ffn_w_gu, ffn_w_down):
    batch = x.shape[0]
    cond = jnp.zeros((ADA_ROWS, D_MODEL), F32).at[:batch].set(c).at[batch].set(c_ctx)
    mods = _ada_rows(cond, ada_w, ada_b)
    ctx_s = ctx.reshape(1, batch * CTX_LEN, D_MODEL)

    n_ch = FNET_GROUP_DIM
    cos_c, sin_c = _dft_tables(n_ch, 16)
    w_ch = (jnp.concatenate([cos_c, sin_c], axis=1) * (n_ch ** -0.5)).astype(BF16)
    cos_t, sin_t = _dft_tables(CTX_LEN, 16)
    ctx_tables = ((cos_t * (CTX_LEN ** -0.5)).astype(BF16), (sin_t * -(CTX_LEN ** -0.5)).astype(BF16))
    n_sub = SEQ // DFT_RADIX
    cos_s, sin_s = _dft_tables(n_sub, 16)
    tw_ang = ((jnp.arange(DFT_RADIX, dtype=jnp.int32)[:, None] * jnp.arange(n_sub, dtype=jnp.int32)[None, :])
              % SEQ).astype(F32) * (2.0 * np.pi / SEQ)
    tw_cos, tw_sin = jnp.cos(tw_ang)[:, :, None], jnp.sin(tw_ang)[:, :, None]
    cos_p = cos_s[None] * tw_cos - sin_s[None] * tw_sin
    sin_p = sin_s[None] * tw_cos + cos_s[None] * tw_sin
    w_sub = (jnp.stack([cos_p, sin_p - cos_p, cos_p + sin_p], axis=1) * (SEQ ** -0.5)).astype(BF16)

    gla_w_out_bf = gla_w_out.astype(BF16)
    fnet_w_out_bf = fnet_w_out.astype(BF16)

    for i in range(DEPTH):
        need_ctx = i < DEPTH - 1
        j = i // 2
        mod_lat = mods[i, :batch].reshape(batch, 1, N_ADA * D_MODEL)
        mod_ctx = mods[i, batch:batch + 1].reshape(1, 1, N_ADA * D_MODEL)
        gains = [norm_gains[i, n].reshape(1, D_MODEL) for n in range(4)]
        colmajor = False
        if i % 2 == 0:
            colmajor = j % 2 == 1
            w_low = lax.optimization_barrier(gla_w_in[j, :, GLA_MAIN_DIM:])
            w_low = jnp.pad(w_low, ((0, 0), (0, LANES - 2 * GLA_GATE_RANK))).astype(BF16)
            w2 = jnp.zeros((2, LANES, GLA_KEY_DIM), F32)
            w2 = w2.at[0, :GLA_GATE_RANK].set(gla_wg2_f[j]).at[1, GLA_GATE_RANK:2 * GLA_GATE_RANK].set(gla_wg2_b[j])
            bg = jnp.stack([gla_bg_f[j], gla_bg_b[j]]).reshape(2, 1, GLA_KEY_DIM)
            qkvr_c, low_c, w_in_bf = _gla_in(ctx_s, mod_ctx, gains[0], jnp.swapaxes(gla_w_in, 1, 2), w_low,
                                             False, layer=j)
            qkvr_l, low_l = _gla_in(x, mod_lat, gains[0], w_in_bf, w_low, colmajor)
            a_ctx, a_lat = _gla_scan(
                qkvr_c.reshape(batch, CTX_LEN, GLA_MAIN_DIM), low_c.reshape(batch, CTX_LEN, LANES),
                qkvr_l, low_l, w2.astype(BF16), bg, gla_head_gain[j].reshape(1, GLA_HEAD_V))
            w_mix = gla_w_out_bf
        else:
            y1, y2 = _fnet_channels(x, mod_lat, gains[0], w_ch, True)
            a_lat = _fnet_combine(_fnet_sub_dft(w_sub, y1, y2))
            if need_ctx:
                y1, y2 = _fnet_channels(ctx_s, mod_ctx, gains[0], w_ch, False)
                a_ctx = _fnet_tokens(*ctx_tables, y1.reshape(batch, CTX_LEN, D_MODEL),
                                     y2.reshape(batch, CTX_LEN, D_MODEL))
            w_mix = fnet_w_out_bf
        if need_ctx:
            ctx_s, ffn_bf = _out_ffn(ctx_s, a_ctx.reshape(1, batch * CTX_LEN, D_MODEL), mod_ctx,
                                     gains[1], gains[2], gains[3], w_mix, j, (ffn_w_gu, ffn_w_down), False,
                                     layer=i)
        else:
            ffn_bf = (*_cast_gate_up(ffn_w_gu, i), ffn_w_down[i].astype(BF16))
        x = _out_ffn(x, a_lat, mod_lat, gains[1], gains[2], gains[3], w_mix, j, ffn_bf, colmajor)
    return x
```

```python
from functools import partial

import jax
import jax.numpy as jnp
import numpy as np
from jax import lax
from jax.experimental import pallas as pl
from jax.experimental.pallas import tpu as pltpu

D_MODEL = 2048
SEQ = 4096
CTX_LEN = 256
GRID_W = 64
DEPTH = 4
GLA_HEADS = 4
GLA_HEAD_K = 256
GLA_HEAD_V = 512
GLA_KEY_DIM = GLA_HEADS * GLA_HEAD_K
GLA_VALUE_DIM = GLA_HEADS * GLA_HEAD_V
GLA_GATE_RANK = 16
GLA_GATE_TAU = 16.0
GLA_MAIN_DIM = 2 * GLA_KEY_DIM + 2 * GLA_VALUE_DIM
FNET_GROUPS = 4
FNET_GROUP_DIM = D_MODEL // FNET_GROUPS
D_FF = 5632
N_ADA = 6
EPS = 1e-6

LANES = 128
ADA_ROWS = 8
VMEM_LIMIT = 56 * 1024 * 1024

TOK_TILE = 512
IN_TOK_TILE = 1024
FNET_TOK_TILE = 1024
FF_TILE = 512
IN_TILE = 1536
IN_TILE_F32 = 768
FF_TILE_F32 = 256
ADA_TILE = 1536
GLA_CHUNK = 128
GLA_UNROLL = 4
DFT_BM = 1024
DFT_BK = 512
DFT_RADIX = 8
COMBINE_ROWS = 256
DFT_TABLE_SPLIT = 16

BF16 = jnp.bfloat16
F32 = jnp.float32


def _cparams(sem):
    return pltpu.CompilerParams(dimension_semantics=sem, vmem_limit_bytes=VMEM_LIMIT)


def _dot(a, b):
    return jnp.dot(a, b, preferred_element_type=F32)


def _rms(x):
    return x * lax.rsqrt(jnp.mean(x * x, axis=-1, keepdims=True) + EPS)


def _modulate(x, gain, shift, scale):
    return (_rms(x) * gain) * (1.0 + scale) + shift


def _ada_kernel(c_ref, w_ref, b_ref, o_ref):
    c = c_ref[...]
    s = (c * jax.nn.sigmoid(c)).astype(BF16)
    o_ref[...] = _dot(s, w_ref[...].astype(BF16)) + b_ref[...]


def _ada_rows(cond, ada_w, ada_b):
    n_out = N_ADA * D_MODEL
    return pl.pallas_call(
        _ada_kernel,
        grid=(DEPTH, n_out // ADA_TILE),
        in_specs=[
            pl.BlockSpec((ADA_ROWS, D_MODEL), lambda i, n: (0, 0)),
            pl.BlockSpec((None, D_MODEL, ADA_TILE), lambda i, n: (i, 0, n)),
            pl.BlockSpec((None, 1, ADA_TILE), lambda i, n: (i, 0, n)),
        ],
        out_specs=pl.BlockSpec((None, ADA_ROWS, ADA_TILE), lambda i, n: (i, 0, n)),
        out_shape=jax.ShapeDtypeStruct((DEPTH, ADA_ROWS, n_out), F32),
        compiler_params=_cparams(("arbitrary", "arbitrary")),
        name="ada_rows",
    )(cond, ada_w, ada_b.reshape(DEPTH, 1, n_out))


def _tile_spec(colmajor, rows=TOK_TILE):
    if colmajor:
        return pl.BlockSpec((None, GRID_W, rows // GRID_W, D_MODEL), lambda b, t, *_: (b, 0, t, 0))
    return pl.BlockSpec((None, rows, D_MODEL), lambda b, t, *_: (b, t, 0))


def _load_tile(x_ref, colmajor):
    if not colmajor:
        return x_ref[...]
    return jnp.concatenate([x_ref[:, c, :] for c in range(x_ref.shape[1])], axis=0)


def _store_tile(o_ref, val, colmajor):
    if not colmajor:
        o_ref[...] = val
        return
    for c in range(o_ref.shape[1]):
        o_ref[:, c, :] = val[c * GRID_W:(c + 1) * GRID_W, :]


def _mod_spec(j):
    return pl.BlockSpec((None, 1, D_MODEL), lambda b, t, *_: (b, 0, j))


def _row_spec():
    return pl.BlockSpec((1, D_MODEL), lambda b, t, *_: (0, 0))


def _as_stream(x, colmajor):
    if colmajor:
        bq, s, d = x.shape
        return x.reshape(bq, GRID_W, s // GRID_W, d)
    return x


def _gla_in_kernel(x_ref, sh_ref, sc_ref, g_ref, w_ref, wlow_ref, o_ref, low_ref, *rest, colmajor, emit_bf16):
    h_ref = rest[-1]
    n = pl.program_id(2)

    @pl.when(n == 0)
    def _():
        h = _modulate(_load_tile(x_ref, colmajor), g_ref[...], sh_ref[...], sc_ref[...]).astype(BF16)
        h_ref[...] = h
        low_ref[...] = _dot(h, wlow_ref[...])

    w = w_ref[...]
    if emit_bf16:
        w = w.T.astype(BF16)
        rest[0][...] = w
    o_ref[...] = _dot(h_ref[...], w).astype(BF16)


def _gla_in(x, mods, gain, w_in, w_low, colmajor, layer=None):
    bq, s, _ = x.shape
    emit = layer is not None
    tn = IN_TILE_F32 if emit else IN_TILE
    tm = min(IN_TOK_TILE, s)
    if emit:
        assert bq * s == tm, "each weight block must be visited exactly once"
        w_spec = pl.BlockSpec((None, tn, D_MODEL), lambda b, t, n: (layer, n, 0))
    else:
        w_spec = pl.BlockSpec((None, D_MODEL, tn), lambda b, t, n: (n, 0, 0))
    out_specs = [
        pl.BlockSpec((None, tm, tn), lambda b, t, n: (b, t, n)),
        pl.BlockSpec((None, tm, LANES), lambda b, t, n: (b, t, 0)),
    ]
    out_shape = [
        jax.ShapeDtypeStruct((bq, s, GLA_MAIN_DIM), BF16),
        jax.ShapeDtypeStruct((bq, s, LANES), F32),
    ]
    if emit:
        per = IN_TILE // tn
        out_specs.append(pl.BlockSpec((None, D_MODEL, tn), lambda b, t, n: (n // per, 0, n % per)))
        out_shape.append(jax.ShapeDtypeStruct((GLA_MAIN_DIM // IN_TILE, D_MODEL, IN_TILE), BF16))
    return pl.pallas_call(
        partial(_gla_in_kernel, colmajor=colmajor, emit_bf16=emit),
        grid=(bq, s // tm, GLA_MAIN_DIM // tn),
        in_specs=[
            _tile_spec(colmajor, tm), _mod_spec(0), _mod_spec(1), _row_spec(),
            w_spec,
            pl.BlockSpec((D_MODEL, LANES), lambda b, t, n: (0, 0)),
        ],
        out_specs=out_specs,
        out_shape=out_shape,
        scratch_shapes=[pltpu.VMEM((tm, D_MODEL), BF16)],
        compiler_params=_cparams(("arbitrary", "arbitrary", "arbitrary")),
        name="gla_in",
    )(_as_stream(x, colmajor), mods, mods, gain, w_in, w_low)


def _log_gate(z):
    return (jnp.minimum(z, 0.0) - jnp.log(1.0 + jnp.exp(-jnp.abs(z)))) * (1.0 / GLA_GATE_TAU)


def _gla_group(q_ref, k_ref, v_ref, low_ref, w2_ref, bg_ref, tris, masks, s_ref, starts, u):
    c, dk = GLA_CHUNK, GLA_HEAD_K
    span = [pl.ds(pl.multiple_of(starts[d], c), u * c) for d in (0, 1)]
    z = [_dot(low_ref[span[d], :].astype(BF16), w2_ref[d]) + bg_ref[d] for d in (0, 1)]
    g = [_log_gate(zd) for zd in z]
    cums = []
    for d in (0, 1):
        parts = []
        for j in range(u):
            gj = g[d][j * c:(j + 1) * c, :]
            hi = gj.astype(BF16)
            parts += [hi, (gj - hi.astype(F32)).astype(BF16)]
        cum = _dot(tris[d], jnp.concatenate(parts, axis=1))
        cums.append([cum[:, 2 * j * dk:(2 * j + 1) * dk] + cum[:, (2 * j + 1) * dk:(2 * j + 2) * dk]
                     for j in range(u)])
    order = [(d, j if d == 0 else u - 1 - j) for j in range(u) for d in (0, 1)]
    work = {}
    for d, j in order:
        rows = pl.ds(pl.multiple_of(starts[d] + j * c, c), c)
        cum = cums[d][j]
        if d == 1:
            ref, tot = cum[c // 2:c // 2 + 1, :], cum[0:1, :]
        else:
            ref, tot = cum[c // 2 - 1:c // 2, :], cum[c - 1:c, :]
        qf = q_ref[rows, :].astype(F32) * (GLA_HEAD_K ** -0.5)
        kf = k_ref[rows, :].astype(F32)
        work[d, j] = dict(
            rows=rows, tot=tot, v=v_ref[rows, :],
            q_mid=(qf * jnp.exp(cum - ref)).astype(BF16), k_mid=(kf * jnp.exp(ref - cum)).astype(BF16),
            q_dec=(qf * jnp.exp(cum)).astype(BF16), k_end=(kf * jnp.exp(tot - cum)).astype(BF16))
    for key in order:
        w = work[key]
        s = lax.dot_general(w["q_mid"], w["k_mid"], (((1,), (1,)), ((), ())), preferred_element_type=F32)
        w["scores"] = jnp.where(masks[key[0]], s, 0.0).astype(BF16)
    for key in order:
        w = work[key]
        w["o"] = _dot(w["scores"], w["v"])
        w["ds"] = lax.dot_general(w["k_end"], w["v"], (((0,), (0,)), ((), ())), preferred_element_type=F32)
        dec = jnp.broadcast_to(jnp.exp(w["tot"]), (LANES, dk)).T
        w["dec"] = jnp.concatenate([dec] * (GLA_HEAD_V // LANES), axis=1)
    out = []
    state = [s_ref[0], s_ref[1]]
    for d, j in order:
        w = work[d, j]
        out.append((w["rows"], w["o"] + _dot(w["q_dec"], state[d].astype(BF16))))
        state[d] = state[d] * w["dec"] + w["ds"]
    s_ref[0] = state[0]
    s_ref[1] = state[1]
    return out


def _gla_kernel(qc_ref, kc_ref, vc_ref, rc_ref, lowc_ref, ql_ref, kl_ref, vl_ref, rl_ref, lowl_ref,
                w2_ref, bg_ref, hg_ref, oc_ref, ol_ref, s_ref, acc_ref):
    c = GLA_CHUNK
    row = lax.broadcasted_iota(jnp.int32, (c, c), 0)
    col = lax.broadcasted_iota(jnp.int32, (c, c), 1)
    masks = (col <= row, col >= row)
    tris = tuple(m.astype(BF16) for m in masks)
    hg = hg_ref[...]
    s_ref[...] = jnp.zeros_like(s_ref)

    def phase(q_ref, k_ref, v_ref, r_ref, low_ref, o_ref, length):
        u = min(GLA_UNROLL, length // (2 * c))
        n_groups = length // (u * c)
        half = n_groups // 2

        def group(n):
            starts = (n * (u * c), (n_groups - 1 - n) * (u * c))
            return _gla_group(q_ref, k_ref, v_ref, low_ref, w2_ref, bg_ref, tris, masks, s_ref, starts, u)

        def first_visit(n, carry):
            for rows, o in group(n):
                acc_ref[rows, :] = o
            return carry

        def second_visit(n, carry):
            for rows, o in group(n):
                o = _rms(o + acc_ref[rows, :]) * hg
                r = r_ref[rows, :].astype(F32)
                o_ref[rows, :] = (o * (r * jax.nn.sigmoid(r))).astype(BF16)
            return carry

        lax.fori_loop(0, half, first_visit, 0)
        lax.fori_loop(half, n_groups, second_visit, 0)

    phase(qc_ref, kc_ref, vc_ref, rc_ref, lowc_ref, oc_ref, CTX_LEN)
    phase(ql_ref, kl_ref, vl_ref, rl_ref, lowl_ref, ol_ref, SEQ)


def _gla_scan(qkvr_c, low_c, qkvr_l, low_l, w2, bg, head_gain):
    b = qkvr_l.shape[0]
    nk = GLA_KEY_DIM // GLA_HEAD_K

    def stream_specs(length):
        return [
            pl.BlockSpec((None, length, GLA_HEAD_K), lambda i, h: (i, 0, h)),
            pl.BlockSpec((None, length, GLA_HEAD_K), lambda i, h: (i, 0, nk + h)),
            pl.BlockSpec((None, length, GLA_HEAD_V), lambda i, h: (i, 0, nk + h)),
            pl.BlockSpec((None, length, GLA_HEAD_V), lambda i, h: (i, 0, 2 * nk + h)),
            pl.BlockSpec((None, length, LANES), lambda i, h: (i, 0, 0)),
        ]

    def out_spec(length):
        return pl.BlockSpec((None, length, GLA_HEAD_V), lambda i, h: (i, 0, h))

    return pl.pallas_call(
        _gla_kernel,
        grid=(b, GLA_HEADS),
        in_specs=stream_specs(CTX_LEN) + stream_specs(SEQ) + [
            pl.BlockSpec((2, LANES, GLA_HEAD_K), lambda i, h: (0, 0, h)),
            pl.BlockSpec((2, 1, GLA_HEAD_K), lambda i, h: (0, 0, h)),
            pl.BlockSpec((1, GLA_HEAD_V), lambda i, h: (0, 0)),
        ],
        out_specs=[out_spec(CTX_LEN), out_spec(SEQ)],
        out_shape=[
            jax.ShapeDtypeStruct((b, CTX_LEN, GLA_VALUE_DIM), BF16),
            jax.ShapeDtypeStruct((b, SEQ, GLA_VALUE_DIM), BF16),
        ],
        scratch_shapes=[
            pltpu.VMEM((2, GLA_HEAD_K, GLA_HEAD_V), F32),
            pltpu.VMEM((SEQ, GLA_HEAD_V), F32),
        ],
        compiler_params=_cparams(("arbitrary", "arbitrary")),
        name="gla_scan",
    )(qkvr_c, qkvr_c, qkvr_c, qkvr_c, low_c, qkvr_l, qkvr_l, qkvr_l, qkvr_l, low_l, w2, bg, head_gain)


def _fnet_ch_kernel(x_ref, sh_ref, sc_ref, g_ref, w_ref, y1_ref, y2_ref, *, by_phase):
    if by_phase:
        x = jnp.concatenate([x_ref[:, s, :] for s in range(DFT_RADIX)], axis=0)
    else:
        x = x_ref[...]
    h = _modulate(x, g_ref[...], sh_ref[...], sc_ref[...]).astype(BF16)
    w = w_ref[...]
    gd = FNET_GROUP_DIM
    rows = x_ref.shape[0]
    for g in range(FNET_GROUPS):
        r = _dot(h[:, g * gd:(g + 1) * gd], w).astype(BF16)
        cols = slice(g * gd, (g + 1) * gd)
        if by_phase:
            for s in range(DFT_RADIX):
                y1_ref[s, :, cols] = r[s * rows:(s + 1) * rows, :gd]
                y2_ref[s, :, cols] = r[s * rows:(s + 1) * rows, gd:]
        else:
            y1_ref[:, cols] = r[:, :gd]
            y2_ref[:, cols] = r[:, gd:]


def _fnet_channels(x, mods, gain, w_ch, by_phase):
    bq, s, _ = x.shape
    tm = FNET_TOK_TILE if by_phase else TOK_TILE
    if by_phase:
        rows = tm // DFT_RADIX
        x = x.reshape(bq, s // DFT_RADIX, DFT_RADIX, D_MODEL)
        x_spec = pl.BlockSpec((None, rows, DFT_RADIX, D_MODEL), lambda b, t: (b, t, 0, 0))
        y_spec = pl.BlockSpec((None, DFT_RADIX, rows, D_MODEL), lambda b, t: (b, 0, t, 0))
        out = jax.ShapeDtypeStruct((bq, DFT_RADIX, s // DFT_RADIX, D_MODEL), BF16)
    else:
        x_spec = y_spec = _tile_spec(False)
        out = jax.ShapeDtypeStruct((bq, s, D_MODEL), BF16)
    return pl.pallas_call(
        partial(_fnet_ch_kernel, by_phase=by_phase),
        grid=(bq, s // tm),
        in_specs=[
            x_spec, _mod_spec(0), _mod_spec(1), _row_spec(),
            pl.BlockSpec((FNET_GROUP_DIM, 2 * FNET_GROUP_DIM), lambda b, t: (0, 0)),
        ],
        out_specs=[y_spec, y_spec],
        out_shape=[out, out],
        compiler_params=_cparams(("arbitrary", "arbitrary")),
        name="fnet_channels",
    )(x, mods, mods, gain, w_ch)


def _fnet_sub_kernel(w_ref, y1_ref, y2_ref, o_ref):
    n = y1_ref.shape[0]
    y1, y2 = y1_ref[...], y2_ref[...]
    k1 = _dot(w_ref[0], y1 + y2)
    o_ref[n:, :] = (-(k1 + _dot(w_ref[1], y1))).astype(BF16)
    o_ref[:n, :] = (k1 - _dot(w_ref[2], y2)).astype(BF16)


def _fnet_sub_dft(w_sub, y1, y2):
    b, radix, n, _ = y1.shape
    y_spec = pl.BlockSpec((None, None, n, D_MODEL), lambda i, s: (i, s, 0, 0))
    return pl.pallas_call(
        _fnet_sub_kernel,
        grid=(b, radix),
        in_specs=[pl.BlockSpec((None, 3, n, n), lambda i, s: (s, 0, 0, 0)), y_spec, y_spec],
        out_specs=pl.BlockSpec((None, None, 2 * n, D_MODEL), lambda i, s: (i, s, 0, 0)),
        out_shape=jax.ShapeDtypeStruct((b, radix, 2 * n, D_MODEL), BF16),
        compiler_params=_cparams(("arbitrary", "arbitrary")),
        name="fnet_sub_dft",
    )(w_sub, y1, y2)


def _fft8_real(xs):
    r2 = float(np.sqrt(0.5))

    def add(a, b):
        return a[0] + b[0], a[1] + b[1]

    def sub(a, b):
        return a[0] - b[0], a[1] - b[1]

    def mul_neg_i(a):
        return a[1], -a[0]

    u = [add(xs[k], xs[k + 4]) for k in range(4)]
    d = [sub(xs[k], xs[k + 4]) for k in range(4)]
    v = [d[0],
         ((d[1][0] + d[1][1]) * r2, (d[1][1] - d[1][0]) * r2),
         mul_neg_i(d[2]),
         ((d[3][1] - d[3][0]) * r2, -(d[3][0] + d[3][1]) * r2)]

    def fft4_real(y):
        p0, p1 = add(y[0], y[2]), add(y[1], y[3])
        q0, q1 = sub(y[0], y[2]), mul_neg_i(sub(y[1], y[3]))
        return [p0[0] + p1[0], q0[0] + q1[0], p0[0] - p1[0], q0[0] - q1[0]]

    even, odd = fft4_real(u), fft4_real(v)
    return [even[0], odd[0], even[1], odd[1], even[2], odd[2], even[3], odd[3]]


def _fnet_combine_kernel(re_ref, im_ref, o_ref):
    bp = re_ref.shape[1]
    sub_rows = 16

    def body(rg, carry):
        rows = pl.ds(pl.multiple_of(rg * sub_rows, sub_rows), sub_rows)
        for lc in range(D_MODEL // LANES):
            lanes = slice(lc * LANES, (lc + 1) * LANES)
            xs = [(re_ref[s, rows, lanes].astype(F32), im_ref[s, rows, lanes].astype(F32))
                  for s in range(DFT_RADIX)]
            for q, z in enumerate(_fft8_real(xs)):
                o_ref[q, rows, lanes] = z.astype(BF16)
        return carry

    lax.fori_loop(0, bp // sub_rows, body, 0)


def _fnet_combine(parts):
    b, radix, n2, _ = parts.shape
    n = n2 // 2
    bp = COMBINE_ROWS
    part_spec = lambda off: pl.BlockSpec((None, radix, bp, D_MODEL), lambda i, p: (i, 0, off + p, 0))
    z = pl.pallas_call(
        _fnet_combine_kernel,
        grid=(b, n // bp),
        in_specs=[part_spec(0), part_spec(n // bp)],
        out_specs=pl.BlockSpec((None, radix, bp, D_MODEL), lambda i, p: (i, 0, p, 0)),
        out_shape=jax.ShapeDtypeStruct((b, radix, n, D_MODEL), BF16),
        compiler_params=_cparams(("arbitrary", "arbitrary")),
        name="fnet_combine",
    )(parts, parts)
    return z.reshape(b, radix * n, D_MODEL)


def _fnet_tok_kernel(gc_ref, gs_ref, y1_ref, y2_ref, o_ref, acc_ref):
    k = pl.program_id(2)

    @pl.when(k == 0)
    def _():
        acc_ref[...] = jnp.zeros_like(acc_ref)

    acc_ref[...] += _dot(gc_ref[...], y1_ref[...]) + _dot(gs_ref[...], y2_ref[...])

    @pl.when(k == pl.num_programs(2) - 1)
    def _():
        o_ref[...] = acc_ref[...].astype(BF16)


def _fnet_tokens(gc, gs, y1, y2):
    b, length, _ = y1.shape
    bm, bk = min(DFT_BM, length), min(DFT_BK, length)
    y_spec = pl.BlockSpec((None, bk, D_MODEL), lambda i, m, k: (i, k, 0))
    g_spec = pl.BlockSpec((bm, bk), lambda i, m, k: (m, k))
    return pl.pallas_call(
        _fnet_tok_kernel,
        grid=(b, length // bm, length // bk),
        in_specs=[g_spec, g_spec, y_spec, y_spec],
        out_specs=pl.BlockSpec((None, bm, D_MODEL), lambda i, m, k: (i, m, 0)),
        out_shape=jax.ShapeDtypeStruct((b, length, D_MODEL), BF16),
        scratch_shapes=[pltpu.VMEM((bm, D_MODEL), F32)],
        compiler_params=_cparams(("arbitrary", "arbitrary", "arbitrary")),
        name="fnet_tokens",
    )(gc, gs, y1, y2)


def _dft_tables(n, split=DFT_TABLE_SPLIT):
    lp = jnp.arange(n, dtype=jnp.int32)[:, None]
    hi = jnp.arange(n // split, dtype=jnp.int32)[None, :]
    lo = jnp.arange(split, dtype=jnp.int32)[None, :]
    ang_hi = ((lp * hi) % (n // split)).astype(F32) * (2.0 * np.pi * split / n)
    ang_lo = ((lp * lo) % n).astype(F32) * (2.0 * np.pi / n)
    ch, sh_, cl, sl = jnp.cos(ang_hi), jnp.sin(ang_hi), jnp.cos(ang_lo), jnp.sin(ang_lo)
    cos = ch[:, :, None] * cl[:, None, :] - sh_[:, :, None] * sl[:, None, :]
    sin = sh_[:, :, None] * cl[:, None, :] + ch[:, :, None] * sl[:, None, :]
    return cos.reshape(n, n), sin.reshape(n, n)


def _out_ffn_kernel(x_ref, a_ref, gtm_ref, shf_ref, scf_ref, gtf_ref, g1_ref, g2_ref, g3_ref,
                    wout_ref, wg_ref, wu_ref, wd_ref, o_ref, *rest, colmajor, emit_bf16):
    h_ref, acc_ref = rest[-2:]
    k = pl.program_id(2)

    x1_ref = o_ref.reshape(TOK_TILE, D_MODEL) if colmajor else o_ref

    @pl.when(k == 0)
    def _():
        y = _dot(a_ref[...], wout_ref[...])
        x1 = _load_tile(x_ref, colmajor) + gtm_ref[...] * (_rms(y) * g1_ref[...])
        x1_ref[...] = x1
        h_ref[...] = _modulate(x1, g2_ref[...], shf_ref[...], scf_ref[...]).astype(BF16)
        acc_ref[...] = jnp.zeros_like(acc_ref)

    wg, wu, wd = wg_ref[...], wu_ref[...], wd_ref[...]
    if emit_bf16:
        wg, wu, wd = wg.astype(BF16), wu.astype(BF16), wd.astype(BF16)
        rest[0][...] = wg
        rest[1][...] = wu
        rest[2][...] = wd
    h = h_ref[...]
    gate = _dot(h, wg)
    up = _dot(h, wu)
    act = (gate * jax.nn.sigmoid(gate) * up).astype(BF16)
    acc_ref[...] += _dot(act, wd)

    @pl.when(k == pl.num_programs(2) - 1)
    def _():
        out = x1_ref[...] + gtf_ref[...] * (_rms(acc_ref[...]) * g3_ref[...])
        _store_tile(o_ref, out, colmajor)


def _out_ffn(x, a, mods, g1, g2, g3, w_out, mix_layer, ffn_w, colmajor, layer=None):
    bq, s, _ = x.shape
    emit = layer is not None
    tf = FF_TILE_F32 if emit else FF_TILE
    nf = D_FF // tf
    if emit:
        assert bq * s == TOK_TILE, "each weight block must be visited exactly once"
        w_gu, w_down = ffn_w
        weights = (w_gu, w_gu, w_down)
        w_specs = [
            pl.BlockSpec((None, D_MODEL, tf), lambda b, t, k: (layer, 0, k)),
            pl.BlockSpec((None, D_MODEL, tf), lambda b, t, k: (layer, 0, nf + k)),
            pl.BlockSpec((None, tf, D_MODEL), lambda b, t, k: (layer, k, 0)),
        ]
    else:
        weights = ffn_w
        w_specs = [
            pl.BlockSpec((None, D_MODEL, tf), lambda b, t, k: (k, 0, 0)),
            pl.BlockSpec((None, D_MODEL, tf), lambda b, t, k: (k, 0, 0)),
            pl.BlockSpec((tf, D_MODEL), lambda b, t, k: (k, 0)),
        ]
    out_specs = [_tile_spec(colmajor)]
    out_shape = [jax.ShapeDtypeStruct(_as_stream(x, colmajor).shape, F32)]
    if emit:
        per = FF_TILE // tf
        out_specs += [
            pl.BlockSpec((None, D_MODEL, tf), lambda b, t, k: (k // per, 0, k % per)),
            pl.BlockSpec((None, D_MODEL, tf), lambda b, t, k: (k // per, 0, k % per)),
            pl.BlockSpec((tf, D_MODEL), lambda b, t, k: (k, 0)),
        ]
        out_shape += [
            jax.ShapeDtypeStruct((D_FF // FF_TILE, D_MODEL, FF_TILE), BF16),
            jax.ShapeDtypeStruct((D_FF // FF_TILE, D_MODEL, FF_TILE), BF16),
            jax.ShapeDtypeStruct((D_FF, D_MODEL), BF16),
        ]
    res = pl.pallas_call(
        partial(_out_ffn_kernel, colmajor=colmajor, emit_bf16=emit),
        grid=(bq, s // TOK_TILE, nf),
        in_specs=[
            _tile_spec(colmajor), _tile_spec(False),
            _mod_spec(2), _mod_spec(3), _mod_spec(4), _mod_spec(5),
            _row_spec(), _row_spec(), _row_spec(),
            pl.BlockSpec((None, D_MODEL, D_MODEL), lambda b, t, k: (mix_layer, 0, 0),
                         pipeline_mode=pl.Buffered(1)),
        ] + w_specs,
        out_specs=out_specs,
        out_shape=out_shape,
        scratch_shapes=[
            pltpu.VMEM((TOK_TILE, D_MODEL), BF16),
            pltpu.VMEM((TOK_TILE, D_MODEL), F32),
        ],
        compiler_params=_cparams(("arbitrary", "arbitrary", "arbitrary")),
        name="out_ffn",
    )(_as_stream(x, colmajor), a, mods, mods, mods, mods, g1, g2, g3, w_out, *weights)
    if emit:
        return res[0].reshape(x.shape), tuple(res[1:])
    return res[0].reshape(x.shape)


def _cast_gu_kernel(g_ref, u_ref, go_ref, uo_ref):
    go_ref[...] = g_ref[...].astype(BF16)
    uo_ref[...] = u_ref[...].astype(BF16)


def _cast_gate_up(w_gu, layer):
    nf = D_FF // FF_TILE
    out = jax.ShapeDtypeStruct((nf, D_MODEL, FF_TILE), BF16)
    o_spec = pl.BlockSpec((None, D_MODEL, FF_TILE), lambda k: (k, 0, 0))
    return pl.pallas_call(
        _cast_gu_kernel,
        grid=(nf,),
        in_specs=[
            pl.BlockSpec((None, D_MODEL, FF_TILE), lambda k: (layer, 0, k)),
            pl.BlockSpec((None, D_MODEL, FF_TILE), lambda k: (layer, 0, nf + k)),
        ],
        out_specs=[o_spec, o_spec],
        out_shape=[out, out],
        compiler_params=_cparams(("arbitrary",)),
        name="cast_gate_up",
    )(w_gu, w_gu)


def kernel(x, c, ctx, c_ctx, ada_w, ada_b, norm_gains, gla_w_in, gla_wg2_f, gla_bg_f, gla_wg2_b, gla_bg_b,
           gla_head_gain, gla_w_out, fnet_w_out, ffn_w_gu, ffn_w_down):
    batch = x.shape[0]
    cond = jnp.zeros((ADA_ROWS, D_MODEL), F32).at[:batch].set(c).at[batch].set(c_ctx)
    mods = _ada_rows(cond, ada_w, ada_b)
    ctx_s = ctx.reshape(1, batch * CTX_LEN, D_MODEL)

    n_ch = FNET_GROUP_DIM
    cos_c, sin_c = _dft_tables(n_ch)
    w_ch = (jnp.concatenate([cos_c, sin_c], axis=1) * (n_ch ** -0.5)).astype(BF16)
    cos_t, sin_t = _dft_tables(CTX_LEN)
    ctx_tables = ((cos_t * (CTX_LEN ** -0.5)).astype(BF16), (sin_t * -(CTX_LEN ** -0.5)).astype(BF16))
    n_sub = SEQ // DFT_RADIX
    cos_s, sin_s = _dft_tables(n_sub)
    tw_ang = ((jnp.arange(DFT_RADIX, dtype=jnp.int32)[:, None] * jnp.arange(n_sub, dtype=jnp.int32)[None, :])
              % SEQ).astype(F32) * (2.0 * np.pi / SEQ)
    tw_cos, tw_sin = jnp.cos(tw_ang)[:, :, None], jnp.sin(tw_ang)[:, :, None]
    cos_p = cos_s[None] * tw_cos - sin_s[None] * tw_sin
    sin_p = sin_s[None] * tw_cos + cos_s[None] * tw_sin
    w_sub = (jnp.stack([cos_p, sin_p - cos_p, cos_p + sin_p], axis=1) * (SEQ ** -0.5)).astype(BF16)

    gla_w_out_bf = gla_w_out.astype(BF16)
    fnet_w_out_bf = fnet_w_out.astype(BF16)

    for i in range(DEPTH):
        need_ctx = i < DEPTH - 1
        j = i // 2
        mod_lat = mods[i, :batch].reshape(batch, 1, N_ADA * D_MODEL)
        mod_ctx = mods[i, batch:batch + 1].reshape(1, 1, N_ADA * D_MODEL)
        gains = [norm_gains[i, n].reshape(1, D_MODEL) for n in range(4)]
        colmajor = False
        if i % 2 == 0:
            colmajor = j % 2 == 1
            w_low = lax.optimization_barrier(gla_w_in[j, :, GLA_MAIN_DIM:])
            w_low = jnp.pad(w_low, ((0, 0), (0, LANES - 2 * GLA_GATE_RANK))).astype(BF16)
            w2 = jnp.zeros((2, LANES, GLA_KEY_DIM), F32)
            w2 = w2.at[0, :GLA_GATE_RANK].set(gla_wg2_f[j]).at[1, GLA_GATE_RANK:2 * GLA_GATE_RANK].set(gla_wg2_b[j])
            bg = jnp.stack([gla_bg_f[j], gla_bg_b[j]]).reshape(2, 1, GLA_KEY_DIM)
            qkvr_c, low_c, w_in_bf = _gla_in(ctx_s, mod_ctx, gains[0], jnp.swapaxes(gla_w_in, 1, 2), w_low,
                                             False, layer=j)
            qkvr_l, low_l = _gla_in(x, mod_lat, gains[0], w_in_bf, w_low, colmajor)
            a_ctx, a_lat = _gla_scan(
                qkvr_c.reshape(batch, CTX_LEN, GLA_MAIN_DIM), low_c.reshape(batch, CTX_LEN, LANES),
                qkvr_l, low_l, w2.astype(BF16), bg, gla_head_gain[j].reshape(1, GLA_HEAD_V))
            w_mix = gla_w_out_bf
        else:
            y1, y2 = _fnet_channels(x, mod_lat, gains[0], w_ch, True)
            a_lat = _fnet_combine(_fnet_sub_dft(w_sub, y1, y2))
            if need_ctx:
                y1, y2 = _fnet_channels(ctx_s, mod_ctx, gains[0], w_ch, False)
                a_ctx = _fnet_tokens(*ctx_tables, y1.reshape(batch, CTX_LEN, D_MODEL),
                                     y2.reshape(batch, CTX_LEN, D_MODEL))
            w_mix = fnet_w_out_bf
        if need_ctx:
            ctx_s, ffn_bf = _out_ffn(ctx_s, a_ctx.reshape(1, batch * CTX_LEN, D_MODEL), mod_ctx,
                                     gains[1], gains[2], gains[3], w_mix, j, (ffn_w_gu, ffn_w_down), False,
                                     layer=i)
        else:
            ffn_bf = (*_cast_gate_up(ffn_w_gu, i), ffn_w_down[i].astype(BF16))
        x = _out_ffn(x, a_lat, mod_lat, gains[1], gains[2], gains[3], w_mix, j, ffn_bf, colmajor)
    return x
```

```python
from functools import partial

import jax
import jax.numpy as jnp
import numpy as np
from jax import lax
from jax.experimental import pallas as pl
from jax.experimental.pallas import tpu as pltpu

D_MODEL = 2048
SEQ = 4096
CTX_LEN = 256
GRID_W = 64
DEPTH = 4
GLA_HEADS = 4
GLA_HEAD_K = 256
GLA_HEAD_V = 512
GLA_KEY_DIM = GLA_HEADS * GLA_HEAD_K
GLA_VALUE_DIM = GLA_HEADS * GLA_HEAD_V
GLA_GATE_RANK = 16
GLA_GATE_TAU = 16.0
GLA_MAIN_DIM = 2 * GLA_KEY_DIM + 2 * GLA_VALUE_DIM
FNET_GROUPS = 4
FNET_GROUP_DIM = D_MODEL // FNET_GROUPS
D_FF = 5632
N_ADA = 6
EPS = 1e-6

LANES = 128
ADA_ROWS = 8
VMEM_LIMIT = 56 * 1024 * 1024

TOK_TILE = 512
IN_TOK_TILE = 1024
FNET_TOK_TILE = 1024
FFN_TOK_TILE = 1024
FFN_FF_TILE = 256
FF_TILE = 512
IN_TILE = 1536
IN_TILE_F32 = 768
FF_TILE_F32 = 256
ADA_TILE = 1536
GLA_CHUNK = 128
GLA_UNROLL = 4
DFT_BM = 1024
DFT_BK = 512
DFT_RADIX = 8
COMBINE_ROWS = 256

BF16 = jnp.bfloat16
F32 = jnp.float32


def _cparams(sem):
    return pltpu.CompilerParams(dimension_semantics=sem, vmem_limit_bytes=VMEM_LIMIT)


def _dot(a, b):
    return jnp.dot(a, b, preferred_element_type=F32)


def _rms(x):
    return x * lax.rsqrt(jnp.mean(x * x, axis=-1, keepdims=True) + EPS)


def _modulate(x, gain, shift, scale):
    return (_rms(x) * gain) * (1.0 + scale) + shift


def _ada_kernel(c_ref, w_ref, b_ref, o_ref):
    c = c_ref[...]
    s = (c * jax.nn.sigmoid(c)).astype(BF16)
    o_ref[...] = _dot(s, w_ref[...].astype(BF16)) + b_ref[...]


def _ada_rows(cond, ada_w, ada_b):
    n_out = N_ADA * D_MODEL
    return pl.pallas_call(
        _ada_kernel,
        grid=(DEPTH, n_out // ADA_TILE),
        in_specs=[
            pl.BlockSpec((ADA_ROWS, D_MODEL), lambda i, n: (0, 0)),
            pl.BlockSpec((None, D_MODEL, ADA_TILE), lambda i, n: (i, 0, n)),
            pl.BlockSpec((None, 1, ADA_TILE), lambda i, n: (i, 0, n)),
        ],
        out_specs=pl.BlockSpec((None, ADA_ROWS, ADA_TILE), lambda i, n: (i, 0, n)),
        out_shape=jax.ShapeDtypeStruct((DEPTH, ADA_ROWS, n_out), F32),
        compiler_params=_cparams(("arbitrary", "arbitrary")),
        name="ada_rows",
    )(cond, ada_w, ada_b.reshape(DEPTH, 1, n_out))


def _tile_spec(colmajor, rows=TOK_TILE):
    if colmajor:
        return pl.BlockSpec((None, GRID_W, rows // GRID_W, D_MODEL), lambda b, t, *_: (b, 0, t, 0))
    return pl.BlockSpec((None, rows, D_MODEL), lambda b, t, *_: (b, t, 0))


def _load_tile(x_ref, colmajor):
    if not colmajor:
        return x_ref[...]
    return jnp.concatenate([x_ref[:, c, :] for c in range(x_ref.shape[1])], axis=0)


def _store_tile(o_ref, val, colmajor):
    if not colmajor:
        o_ref[...] = val
        return
    for c in range(o_ref.shape[1]):
        o_ref[:, c, :] = val[c * GRID_W:(c + 1) * GRID_W, :]


def _mod_spec(j):
    return pl.BlockSpec((None, 1, D_MODEL), lambda b, t, *_: (b, 0, j))


def _row_spec():
    return pl.BlockSpec((1, D_MODEL), lambda b, t, *_: (0, 0))


def _as_stream(x, colmajor):
    if colmajor:
        bq, s, d = x.shape
        return x.reshape(bq, GRID_W, s // GRID_W, d)
    return x


def _gla_in_kernel(x_ref, sh_ref, sc_ref, g_ref, w_ref, wlow_ref, o_ref, low_ref, *rest, colmajor, emit_bf16):
    h_ref = rest[-1]
    n = pl.program_id(2)

    @pl.when(n == 0)
    def _():
        h = _modulate(_load_tile(x_ref, colmajor), g_ref[...], sh_ref[...], sc_ref[...]).astype(BF16)
        h_ref[...] = h
        low_ref[...] = _dot(h, wlow_ref[...])

    w = w_ref[...]
    if emit_bf16:
        w = w.T.astype(BF16)
        rest[0][...] = w
    o_ref[...] = _dot(h_ref[...], w).astype(BF16)


def _gla_in(x, mods, gain, w_in, w_low, colmajor, layer=None):
    bq, s, _ = x.shape
    emit = layer is not None
    tn = IN_TILE_F32 if emit else IN_TILE
    tm = min(IN_TOK_TILE, s)
    if emit:
        assert bq * s == tm, "each weight block must be visited exactly once"
        w_spec = pl.BlockSpec((None, tn, D_MODEL), lambda b, t, n: (layer, n, 0))
    else:
        w_spec = pl.BlockSpec((None, D_MODEL, tn), lambda b, t, n: (n, 0, 0))
    out_specs = [
        pl.BlockSpec((None, tm, tn), lambda b, t, n: (b, t, n)),
        pl.BlockSpec((None, tm, LANES), lambda b, t, n: (b, t, 0)),
    ]
    out_shape = [
        jax.ShapeDtypeStruct((bq, s, GLA_MAIN_DIM), BF16),
        jax.ShapeDtypeStruct((bq, s, LANES), F32),
    ]
    if emit:
        per = IN_TILE // tn
        out_specs.append(pl.BlockSpec((None, D_MODEL, tn), lambda b, t, n: (n // per, 0, n % per)))
        out_shape.append(jax.ShapeDtypeStruct((GLA_MAIN_DIM // IN_TILE, D_MODEL, IN_TILE), BF16))
    return pl.pallas_call(
        partial(_gla_in_kernel, colmajor=colmajor, emit_bf16=emit),
        grid=(bq, s // tm, GLA_MAIN_DIM // tn),
        in_specs=[
            _tile_spec(colmajor, tm), _mod_spec(0), _mod_spec(1), _row_spec(),
            w_spec,
            pl.BlockSpec((D_MODEL, LANES), lambda b, t, n: (0, 0)),
        ],
        out_specs=out_specs,
        out_shape=out_shape,
        scratch_shapes=[pltpu.VMEM((tm, D_MODEL), BF16)],
        compiler_params=_cparams(("arbitrary", "arbitrary", "arbitrary")),
        name="gla_in",
    )(_as_stream(x, colmajor), mods, mods, gain, w_in, w_low)


def _log_gate(z):
    return (jnp.minimum(z, 0.0) - jnp.log(1.0 + jnp.exp(-jnp.abs(z)))) * (1.0 / GLA_GATE_TAU)


def _gla_group(q_ref, k_ref, v_ref, low_ref, w2_ref, bg_ref, tris, masks, s_ref, starts, u):
    c, dk = GLA_CHUNK, GLA_HEAD_K
    span = [pl.ds(pl.multiple_of(starts[d], c), u * c) for d in (0, 1)]
    z = [_dot(low_ref[span[d], :].astype(BF16), w2_ref[d]) + bg_ref[d] for d in (0, 1)]
    g = [_log_gate(zd) for zd in z]
    cums = []
    for d in (0, 1):
        parts = []
        for j in range(u):
            gj = g[d][j * c:(j + 1) * c, :]
            hi = gj.astype(BF16)
            parts += [hi, (gj - hi.astype(F32)).astype(BF16)]
        cum = _dot(tris[d], jnp.concatenate(parts, axis=1))
        cums.append([cum[:, 2 * j * dk:(2 * j + 1) * dk] + cum[:, (2 * j + 1) * dk:(2 * j + 2) * dk]
                     for j in range(u)])
    order = [(d, j if d == 0 else u - 1 - j) for j in range(u) for d in (0, 1)]
    work = {}
    for d, j in order:
        rows = pl.ds(pl.multiple_of(starts[d] + j * c, c), c)
        cum = cums[d][j]
        if d == 1:
            ref, tot = cum[c // 2:c // 2 + 1, :], cum[0:1, :]
        else:
            ref, tot = cum[c // 2 - 1:c // 2, :], cum[c - 1:c, :]
        qf = q_ref[rows, :].astype(F32) * (GLA_HEAD_K ** -0.5)
        kf = k_ref[rows, :].astype(F32)
        work[d, j] = dict(
            rows=rows, tot=tot, v=v_ref[rows, :],
            q_mid=(qf * jnp.exp(cum - ref)).astype(BF16), k_mid=(kf * jnp.exp(ref - cum)).astype(BF16),
            q_dec=(qf * jnp.exp(cum)).astype(BF16), k_end=(kf * jnp.exp(tot - cum)).astype(BF16))
    for key in order:
        w = work[key]
        s = lax.dot_general(w["q_mid"], w["k_mid"], (((1,), (1,)), ((), ())), preferred_element_type=F32)
        w["scores"] = jnp.where(masks[key[0]], s, 0.0).astype(BF16)
    for key in order:
        w = work[key]
        w["o"] = _dot(w["scores"], w["v"])
        w["ds"] = lax.dot_general(w["k_end"], w["v"], (((0,), (0,)), ((), ())), preferred_element_type=F32)
        dec = jnp.broadcast_to(jnp.exp(w["tot"]), (LANES, dk)).T
        w["dec"] = jnp.concatenate([dec] * (GLA_HEAD_V // LANES), axis=1)
    out = []
    state = [s_ref[0], s_ref[1]]
    for d, j in order:
        w = work[d, j]
        out.append((w["rows"], w["o"] + _dot(w["q_dec"], state[d].astype(BF16))))
        state[d] = state[d] * w["dec"] + w["ds"]
    s_ref[0] = state[0]
    s_ref[1] = state[1]
    return out


def _gla_kernel(qc_ref, kc_ref, vc_ref, rc_ref, lowc_ref, ql_ref, kl_ref, vl_ref, rl_ref, lowl_ref,
                w2_ref, bg_ref, hg_ref, oc_ref, ol_ref, s_ref, acc_ref):
    c = GLA_CHUNK
    row = lax.broadcasted_iota(jnp.int32, (c, c), 0)
    col = lax.broadcasted_iota(jnp.int32, (c, c), 1)
    masks = (col <= row, col >= row)
    tris = tuple(m.astype(BF16) for m in masks)
    hg = hg_ref[...]
    s_ref[...] = jnp.zeros_like(s_ref)

    def phase(q_ref, k_ref, v_ref, r_ref, low_ref, o_ref, length):
        u = min(GLA_UNROLL, length // (2 * c))
        n_groups = length // (u * c)
        half = n_groups // 2

        def group(n):
            starts = (n * (u * c), (n_groups - 1 - n) * (u * c))
            return _gla_group(q_ref, k_ref, v_ref, low_ref, w2_ref, bg_ref, tris, masks, s_ref, starts, u)

        def first_visit(n, carry):
            for rows, o in group(n):
                acc_ref[rows, :] = o
            return carry

        def second_visit(n, carry):
            for rows, o in group(n):
                o = _rms(o + acc_ref[rows, :]) * hg
                r = r_ref[rows, :].astype(F32)
                o_ref[rows, :] = (o * (r * jax.nn.sigmoid(r))).astype(BF16)
            return carry

        lax.fori_loop(0, half, first_visit, 0)
        lax.fori_loop(half, n_groups, second_visit, 0)

    phase(qc_ref, kc_ref, vc_ref, rc_ref, lowc_ref, oc_ref, CTX_LEN)
    phase(ql_ref, kl_ref, vl_ref, rl_ref, lowl_ref, ol_ref, SEQ)


def _gla_scan(qkvr_c, low_c, qkvr_l, low_l, w2, bg, head_gain):
    b = qkvr_l.shape[0]
    nk = GLA_KEY_DIM // GLA_HEAD_K

    def stream_specs(length):
        return [
            pl.BlockSpec((None, length, GLA_HEAD_K), lambda i, h: (i, 0, h)),
            pl.BlockSpec((None, length, GLA_HEAD_K), lambda i, h: (i, 0, nk + h)),
            pl.BlockSpec((None, length, GLA_HEAD_V), lambda i, h: (i, 0, nk + h)),
            pl.BlockSpec((None, length, GLA_HEAD_V), lambda i, h: (i, 0, 2 * nk + h)),
            pl.BlockSpec((None, length, LANES), lambda i, h: (i, 0, 0)),
        ]

    def out_spec(length):
        return pl.BlockSpec((None, length, GLA_HEAD_V), lambda i, h: (i, 0, h))

    return pl.pallas_call(
        _gla_kernel,
        grid=(b, GLA_HEADS),
        in_specs=stream_specs(CTX_LEN) + stream_specs(SEQ) + [
            pl.BlockSpec((2, LANES, GLA_HEAD_K), lambda i, h: (0, 0, h)),
            pl.BlockSpec((2, 1, GLA_HEAD_K), lambda i, h: (0, 0, h)),
            pl.BlockSpec((1, GLA_HEAD_V), lambda i, h: (0, 0)),
        ],
        out_specs=[out_spec(CTX_LEN), out_spec(SEQ)],
        out_shape=[
            jax.ShapeDtypeStruct((b, CTX_LEN, GLA_VALUE_DIM), BF16),
            jax.ShapeDtypeStruct((b, SEQ, GLA_VALUE_DIM), BF16),
        ],
        scratch_shapes=[
            pltpu.VMEM((2, GLA_HEAD_K, GLA_HEAD_V), F32),
            pltpu.VMEM((SEQ, GLA_HEAD_V), F32),
        ],
        compiler_params=_cparams(("arbitrary", "arbitrary")),
        name="gla_scan",
    )(qkvr_c, qkvr_c, qkvr_c, qkvr_c, low_c, qkvr_l, qkvr_l, qkvr_l, qkvr_l, low_l, w2, bg, head_gain)


def _fnet_ch_kernel(x_ref, sh_ref, sc_ref, g_ref, w_ref, y1_ref, y2_ref, *, by_phase):
    if by_phase:
        x = jnp.concatenate([x_ref[:, s, :] for s in range(DFT_RADIX)], axis=0)
    else:
        x = x_ref[...]
    h = _modulate(x, g_ref[...], sh_ref[...], sc_ref[...]).astype(BF16)
    w = w_ref[...]
    gd = FNET_GROUP_DIM
    rows = x_ref.shape[0]
    for g in range(FNET_GROUPS):
        r = _dot(h[:, g * gd:(g + 1) * gd], w).astype(BF16)
        cols = slice(g * gd, (g + 1) * gd)
        if by_phase:
            for s in range(DFT_RADIX):
                y1_ref[s, :, cols] = r[s * rows:(s + 1) * rows, :gd]
                y2_ref[s, :, cols] = r[s * rows:(s + 1) * rows, gd:]
        else:
            y1_ref[:, cols] = r[:, :gd]
            y2_ref[:, cols] = r[:, gd:]


def _fnet_channels(x, mods, gain, w_ch, by_phase):
    bq, s, _ = x.shape
    tm = FNET_TOK_TILE if by_phase else TOK_TILE
    if by_phase:
        rows = tm // DFT_RADIX
        x = x.reshape(bq, s // DFT_RADIX, DFT_RADIX, D_MODEL)
        x_spec = pl.BlockSpec((None, rows, DFT_RADIX, D_MODEL), lambda b, t: (b, t, 0, 0))
        y_spec = pl.BlockSpec((None, DFT_RADIX, rows, D_MODEL), lambda b, t: (b, 0, t, 0))
        out = jax.ShapeDtypeStruct((bq, DFT_RADIX, s // DFT_RADIX, D_MODEL), BF16)
    else:
        x_spec = y_spec = _tile_spec(False)
        out = jax.ShapeDtypeStruct((bq, s, D_MODEL), BF16)
    return pl.pallas_call(
        partial(_fnet_ch_kernel, by_phase=by_phase),
        grid=(bq, s // tm),
        in_specs=[
            x_spec, _mod_spec(0), _mod_spec(1), _row_spec(),
            pl.BlockSpec((FNET_GROUP_DIM, 2 * FNET_GROUP_DIM), lambda b, t: (0, 0)),
        ],
        out_specs=[y_spec, y_spec],
        out_shape=[out, out],
        compiler_params=_cparams(("arbitrary", "arbitrary")),
        name="fnet_channels",
    )(x, mods, mods, gain, w_ch)


def _fnet_sub_kernel(w_ref, y1_ref, y2_ref, o_ref):
    n = y1_ref.shape[0]
    y1, y2 = y1_ref[...], y2_ref[...]
    k1 = _dot(w_ref[0], y1 + y2)
    o_ref[n:, :] = (-(k1 + _dot(w_ref[1], y1))).astype(BF16)
    o_ref[:n, :] = (k1 - _dot(w_ref[2], y2)).astype(BF16)


def _fnet_sub_dft(w_sub, y1, y2):
    b, radix, n, _ = y1.shape
    y_spec = pl.BlockSpec((None, None, n, D_MODEL), lambda i, s: (i, s, 0, 0))
    return pl.pallas_call(
        _fnet_sub_kernel,
        grid=(b, radix),
        in_specs=[pl.BlockSpec((None, 3, n, n), lambda i, s: (s, 0, 0, 0)), y_spec, y_spec],
        out_specs=pl.BlockSpec((None, None, 2 * n, D_MODEL), lambda i, s: (i, s, 0, 0)),
        out_shape=jax.ShapeDtypeStruct((b, radix, 2 * n, D_MODEL), BF16),
        compiler_params=_cparams(("arbitrary", "arbitrary")),
        name="fnet_sub_dft",
    )(w_sub, y1, y2)


def _fft8_real(xs):
    r2 = float(np.sqrt(0.5))

    def add(a, b):
        return a[0] + b[0], a[1] + b[1]

    def sub(a, b):
        return a[0] - b[0], a[1] - b[1]

    def mul_neg_i(a):
        return a[1], -a[0]

    u = [add(xs[k], xs[k + 4]) for k in range(4)]
    d = [sub(xs[k], xs[k + 4]) for k in range(4)]
    v = [d[0],
         ((d[1][0] + d[1][1]) * r2, (d[1][1] - d[1][0]) * r2),
         mul_neg_i(d[2]),
         ((d[3][1] - d[3][0]) * r2, -(d[3][0] + d[3][1]) * r2)]

    def fft4_real(y):
        p0, p1 = add(y[0], y[2]), add(y[1], y[3])
        q0, q1 = sub(y[0], y[2]), mul_neg_i(sub(y[1], y[3]))
        return [p0[0] + p1[0], q0[0] + q1[0], p0[0] - p1[0], q0[0] - q1[0]]

    even, odd = fft4_real(u), fft4_real(v)
    return [even[0], odd[0], even[1], odd[1], even[2], odd[2], even[3], odd[3]]


def _fnet_combine_kernel(re_ref, im_ref, o_ref):
    bp = re_ref.shape[1]
    sub_rows = 16

    def body(rg, carry):
        rows = pl.ds(pl.multiple_of(rg * sub_rows, sub_rows), sub_rows)
        for lc in range(D_MODEL // LANES):
            lanes = slice(lc * LANES, (lc + 1) * LANES)
            xs = [(re_ref[s, rows, lanes].astype(F32), im_ref[s, rows, lanes].astype(F32))
                  for s in range(DFT_RADIX)]
            for q, z in enumerate(_fft8_real(xs)):
                o_ref[q, rows, lanes] = z.astype(BF16)
        return carry

    lax.fori_loop(0, bp // sub_rows, body, 0)


def _fnet_combine(parts):
    b, radix, n2, _ = parts.shape
    n = n2 // 2
    bp = COMBINE_ROWS
    part_spec = lambda off: pl.BlockSpec((None, radix, bp, D_MODEL), lambda i, p: (i, 0, off + p, 0))
    z = pl.pallas_call(
        _fnet_combine_kernel,
        grid=(b, n // bp),
        in_specs=[part_spec(0), part_spec(n // bp)],
        out_specs=pl.BlockSpec((None, radix, bp, D_MODEL), lambda i, p: (i, 0, p, 0)),
        out_shape=jax.ShapeDtypeStruct((b, radix, n, D_MODEL), BF16),
        compiler_params=_cparams(("arbitrary", "arbitrary")),
        name="fnet_combine",
    )(parts, parts)
    return z.reshape(b, radix * n, D_MODEL)


def _fnet_tok_kernel(gc_ref, gs_ref, y1_ref, y2_ref, o_ref, acc_ref):
    k = pl.program_id(2)

    @pl.when(k == 0)
    def _():
        acc_ref[...] = jnp.zeros_like(acc_ref)

    acc_ref[...] += _dot(gc_ref[...], y1_ref[...]) + _dot(gs_ref[...], y2_ref[...])

    @pl.when(k == pl.num_programs(2) - 1)
    def _():
        o_ref[...] = acc_ref[...].astype(BF16)


def _fnet_tokens(gc, gs, y1, y2):
    b, length, _ = y1.shape
    bm, bk = min(DFT_BM, length), min(DFT_BK, length)
    y_spec = pl.BlockSpec((None, bk, D_MODEL), lambda i, m, k: (i, k, 0))
    g_spec = pl.BlockSpec((bm, bk), lambda i, m, k: (m, k))
    return pl.pallas_call(
        _fnet_tok_kernel,
        grid=(b, length // bm, length // bk),
        in_specs=[g_spec, g_spec, y_spec, y_spec],
        out_specs=pl.BlockSpec((None, bm, D_MODEL), lambda i, m, k: (i, m, 0)),
        out_shape=jax.ShapeDtypeStruct((b, length, D_MODEL), BF16),
        scratch_shapes=[pltpu.VMEM((bm, D_MODEL), F32)],
        compiler_params=_cparams(("arbitrary", "arbitrary", "arbitrary")),
        name="fnet_tokens",
    )(gc, gs, y1, y2)


def _dft_tables(n, split):
    lp = jnp.arange(n, dtype=jnp.int32)[:, None]
    hi = jnp.arange(n // split, dtype=jnp.int32)[None, :]
    lo = jnp.arange(split, dtype=jnp.int32)[None, :]
    ang_hi = ((lp * hi) % (n // split)).astype(F32) * (2.0 * np.pi * split / n)
    ang_lo = ((lp * lo) % n).astype(F32) * (2.0 * np.pi / n)
    ch, sh_, cl, sl = jnp.cos(ang_hi), jnp.sin(ang_hi), jnp.cos(ang_lo), jnp.sin(ang_lo)
    cos = ch[:, :, None] * cl[:, None, :] - sh_[:, :, None] * sl[:, None, :]
    sin = sh_[:, :, None] * cl[:, None, :] + ch[:, :, None] * sl[:, None, :]
    return cos.reshape(n, n), sin.reshape(n, n)


def _out_ffn_kernel(x_ref, a_ref, gtm_ref, shf_ref, scf_ref, gtf_ref, g1_ref, g2_ref, g3_ref,
                    wout_ref, wg_ref, wu_ref, wd_ref, o_ref, *rest, colmajor, emit_bf16):
    h_ref, acc_ref = rest[-2:]
    k = pl.program_id(2)

    x1_ref = o_ref.reshape(TOK_TILE, D_MODEL) if colmajor else o_ref

    @pl.when(k == 0)
    def _():
        y = _dot(a_ref[...], wout_ref[...])
        x1 = _load_tile(x_ref, colmajor) + gtm_ref[...] * (_rms(y) * g1_ref[...])
        x1_ref[...] = x1
        h_ref[...] = _modulate(x1, g2_ref[...], shf_ref[...], scf_ref[...]).astype(BF16)
        acc_ref[...] = jnp.zeros_like(acc_ref)

    wg, wu, wd = wg_ref[...], wu_ref[...], wd_ref[...]
    if emit_bf16:
        wg, wu, wd = wg.astype(BF16), wu.astype(BF16), wd.astype(BF16)
        rest[0][...] = wg
        rest[1][...] = wu
        rest[2][...] = wd
    h = h_ref[...]
    gate = _dot(h, wg)
    up = _dot(h, wu)
    act = (gate * jax.nn.sigmoid(gate) * up).astype(BF16)
    acc_ref[...] += _dot(act, wd)

    @pl.when(k == pl.num_programs(2) - 1)
    def _():
        out = x1_ref[...] + gtf_ref[...] * (_rms(acc_ref[...]) * g3_ref[...])
        _store_tile(o_ref, out, colmajor)


def _out_ffn(x, a, mods, g1, g2, g3, w_out, mix_layer, ffn_w, colmajor, layer=None):
    bq, s, _ = x.shape
    emit = layer is not None
    tf = FF_TILE_F32 if emit else FF_TILE
    nf = D_FF // tf
    if emit:
        assert bq * s == TOK_TILE, "each weight block must be visited exactly once"
        w_gu, w_down = ffn_w
        weights = (w_gu, w_gu, w_down)
        w_specs = [
            pl.BlockSpec((None, D_MODEL, tf), lambda b, t, k: (layer, 0, k)),
            pl.BlockSpec((None, D_MODEL, tf), lambda b, t, k: (layer, 0, nf + k)),
            pl.BlockSpec((None, tf, D_MODEL), lambda b, t, k: (layer, k, 0)),
        ]
    else:
        weights = ffn_w
        w_specs = [
            pl.BlockSpec((None, D_MODEL, tf), lambda b, t, k: (k, 0, 0)),
            pl.BlockSpec((None, D_MODEL, tf), lambda b, t, k: (k, 0, 0)),
            pl.BlockSpec((tf, D_MODEL), lambda b, t, k: (k, 0)),
        ]
    out_specs = [_tile_spec(colmajor)]
    out_shape = [jax.ShapeDtypeStruct(_as_stream(x, colmajor).shape, F32)]
    if emit:
        per = FF_TILE // tf
        out_specs += [
            pl.BlockSpec((None, D_MODEL, tf), lambda b, t, k: (k // per, 0, k % per)),
            pl.BlockSpec((None, D_MODEL, tf), lambda b, t, k: (k // per, 0, k % per)),
            pl.BlockSpec((tf, D_MODEL), lambda b, t, k: (k, 0)),
        ]
        out_shape += [
            jax.ShapeDtypeStruct((D_FF // FF_TILE, D_MODEL, FF_TILE), BF16),
            jax.ShapeDtypeStruct((D_FF // FF_TILE, D_MODEL, FF_TILE), BF16),
            jax.ShapeDtypeStruct((D_FF, D_MODEL), BF16),
        ]
    res = pl.pallas_call(
        partial(_out_ffn_kernel, colmajor=colmajor, emit_bf16=emit),
        grid=(bq, s // TOK_TILE, nf),
        in_specs=[
            _tile_spec(colmajor), _tile_spec(False),
            _mod_spec(2), _mod_spec(3), _mod_spec(4), _mod_spec(5),
            _row_spec(), _row_spec(), _row_spec(),
            pl.BlockSpec((None, D_MODEL, D_MODEL), lambda b, t, k: (mix_layer, 0, 0),
                         pipeline_mode=pl.Buffered(1)),
        ] + w_specs,
        out_specs=out_specs,
        out_shape=out_shape,
        scratch_shapes=[
            pltpu.VMEM((TOK_TILE, D_MODEL), BF16),
            pltpu.VMEM((TOK_TILE, D_MODEL), F32),
        ],
        compiler_params=_cparams(("arbitrary", "arbitrary", "arbitrary")),
        name="out_ffn",
    )(_as_stream(x, colmajor), a, mods, mods, mods, mods, g1, g2, g3, w_out, *weights)
    if emit:
        return res[0].reshape(x.shape), tuple(res[1:])
    return res[0].reshape(x.shape)


def _mix_out_kernel(x_ref, a_ref, gtm_ref, shf_ref, scf_ref, g1_ref, g2_ref, wout_ref, x1_ref, h_ref, *,
                    colmajor):
    y = _dot(a_ref[...], wout_ref[...])
    x1 = _load_tile(x_ref, colmajor) + gtm_ref[...] * (_rms(y) * g1_ref[...])
    x1_ref[...] = x1
    h_ref[...] = _modulate(x1, g2_ref[...], shf_ref[...], scf_ref[...]).astype(BF16)


def _mix_out(x, a, mods, g1, g2, w_out, mix_layer, colmajor):
    bq, s, _ = x.shape
    return pl.pallas_call(
        partial(_mix_out_kernel, colmajor=colmajor),
        grid=(bq, s // TOK_TILE),
        in_specs=[
            _tile_spec(colmajor), _tile_spec(False),
            _mod_spec(2), _mod_spec(3), _mod_spec(4), _row_spec(), _row_spec(),
            pl.BlockSpec((None, D_MODEL, D_MODEL), lambda b, t: (mix_layer, 0, 0),
                         pipeline_mode=pl.Buffered(1)),
        ],
        out_specs=[_tile_spec(False), _tile_spec(False)],
        out_shape=[jax.ShapeDtypeStruct((bq, s, D_MODEL), F32), jax.ShapeDtypeStruct((bq, s, D_MODEL), BF16)],
        compiler_params=_cparams(("arbitrary", "arbitrary")),
        name="mix_out",
    )(_as_stream(x, colmajor), a, mods, mods, mods, g1, g2, w_out)


def _ffn_kernel(x1_ref, h_ref, gtf_ref, g3_ref, wg_ref, wu_ref, wd_ref, o_ref, acc_ref, *, colmajor):
    k = pl.program_id(2)

    @pl.when(k == 0)
    def _():
        acc_ref[...] = jnp.zeros_like(acc_ref)

    wg, wu, wd = wg_ref[...], wu_ref[...], wd_ref[...]
    for lo in range(0, h_ref.shape[0], TOK_TILE):
        rows = slice(lo, lo + TOK_TILE)
        h = h_ref[rows, :]
        gate = _dot(h, wg)
        up = _dot(h, wu)
        act = (gate * jax.nn.sigmoid(gate) * up).astype(BF16)
        acc_ref[rows, :] += _dot(act, wd)

    @pl.when(k == pl.num_programs(2) - 1)
    def _():
        out = x1_ref[...] + gtf_ref[...] * (_rms(acc_ref[...]) * g3_ref[...])
        _store_tile(o_ref, out, colmajor)


def _ffn(x1, h, mods, g3, ffn_w, colmajor):
    bq, s, _ = x1.shape
    tm, tf = FFN_TOK_TILE, FFN_FF_TILE
    nf, per = D_FF // tf, FF_TILE // tf
    out_like = _as_stream(x1, colmajor)
    res = pl.pallas_call(
        partial(_ffn_kernel, colmajor=colmajor),
        grid=(bq, s // tm, nf),
        in_specs=[
            pl.BlockSpec((None, tm, D_MODEL), lambda b, t, k: (b, t, 0), pipeline_mode=pl.Buffered(1)),
            pl.BlockSpec((None, tm, D_MODEL), lambda b, t, k: (b, t, 0), pipeline_mode=pl.Buffered(1)),
            _mod_spec(5), _row_spec(),
            pl.BlockSpec((None, D_MODEL, tf), lambda b, t, k: (k // per, 0, k % per)),
            pl.BlockSpec((None, D_MODEL, tf), lambda b, t, k: (k // per, 0, k % per)),
            pl.BlockSpec((tf, D_MODEL), lambda b, t, k: (k, 0)),
        ],
        out_specs=_tile_spec(colmajor, tm),
        out_shape=jax.ShapeDtypeStruct(out_like.shape, F32),
        scratch_shapes=[pltpu.VMEM((tm, D_MODEL), F32)],
        compiler_params=_cparams(("arbitrary", "arbitrary", "arbitrary")),
        name="ffn",
    )(x1, h, mods, g3, *ffn_w)
    return res.reshape(x1.shape)


def _cast_gu_kernel(g_ref, u_ref, go_ref, uo_ref):
    go_ref[...] = g_ref[...].astype(BF16)
    uo_ref[...] = u_ref[...].astype(BF16)


def _cast_gate_up(w_gu, layer):
    nf = D_FF // FF_TILE
    out = jax.ShapeDtypeStruct((nf, D_MODEL, FF_TILE), BF16)
    o_spec = pl.BlockSpec((None, D_MODEL, FF_TILE), lambda k: (k, 0, 0))
    return pl.pallas_call(
        _cast_gu_kernel,
        grid=(nf,),
        in_specs=[
            pl.BlockSpec((None, D_MODEL, FF_TILE), lambda k: (layer, 0, k)),
            pl.BlockSpec((None, D_MODEL, FF_TILE), lambda k: (layer, 0, nf + k)),
        ],
        out_specs=[o_spec, o_spec],
        out_shape=[out, out],
        compiler_params=_cparams(("arbitrary",)),
        name="cast_gate_up",
    )(w_gu, w_gu)


def kernel(x, c, ctx, c_ctx, ada_w, ada_b, norm_gains, gla_w_in, gla_wg2_f, gla_bg_f, gla_wg2_b, gla_bg_b,
           gla_head_gain, gla_w_out, fnet_w_out, ffn_w_gu, ffn_w_down):
    batch = x.shape[0]
    cond = jnp.zeros((ADA_ROWS, D_MODEL), F32).at[:batch].set(c).at[batch].set(c_ctx)
    mods = _ada_rows(cond, ada_w, ada_b)
    ctx_s = ctx.reshape(1, batch * CTX_LEN, D_MODEL)

    n_ch = FNET_GROUP_DIM
    cos_c, sin_c = _dft_tables(n_ch, 16)
    w_ch = (jnp.concatenate([cos_c, sin_c], axis=1) * (n_ch ** -0.5)).astype(BF16)
    cos_t, sin_t = _dft_tables(CTX_LEN, 16)
    ctx_tables = ((cos_t * (CTX_LEN ** -0.5)).astype(BF16), (sin_t * -(CTX_LEN ** -0.5)).astype(BF16))
    n_sub = SEQ // DFT_RADIX
    cos_s, sin_s = _dft_tables(n_sub, 16)
    tw_ang = ((jnp.arange(DFT_RADIX, dtype=jnp.int32)[:, None] * jnp.arange(n_sub, dtype=jnp.int32)[None, :])
              % SEQ).astype(F32) * (2.0 * np.pi / SEQ)
    tw_cos, tw_sin = jnp.cos(tw_ang)[:, :, None], jnp.sin(tw_ang)[:, :, None]
    cos_p = cos_s[None] * tw_cos - sin_s[None] * tw_sin
    sin_p = sin_s[None] * tw_cos + cos_s[None] * tw_sin
    w_sub = (jnp.stack([cos_p, sin_p - cos_p, cos_p + sin_p], axis=1) * (SEQ ** -0.5)).astype(BF16)

    gla_w_out_bf = gla_w_out.astype(BF16)
    fnet_w_out_bf = fnet_w_out.astype(BF16)

    for i in range(DEPTH):
        need_ctx = i < DEPTH - 1
        j = i // 2
        mod_lat = mods[i, :batch].reshape(batch, 1, N_ADA * D_MODEL)
        mod_ctx = mods[i, batch:batch + 1].reshape(1, 1, N_ADA * D_MODEL)
        gains = [norm_gains[i, n].reshape(1, D_MODEL) for n in range(4)]
        colmajor = False
        if i % 2 == 0:
            colmajor = j % 2 == 1
            w_low = lax.optimization_barrier(gla_w_in[j, :, GLA_MAIN_DIM:])
            w_low = jnp.pad(w_low, ((0, 0), (0, LANES - 2 * GLA_GATE_RANK))).astype(BF16)
            w2 = jnp.zeros((2, LANES, GLA_KEY_DIM), F32)
            w2 = w2.at[0, :GLA_GATE_RANK].set(gla_wg2_f[j]).at[1, GLA_GATE_RANK:2 * GLA_GATE_RANK].set(gla_wg2_b[j])
            bg = jnp.stack([gla_bg_f[j], gla_bg_b[j]]).reshape(2, 1, GLA_KEY_DIM)
            qkvr_c, low_c, w_in_bf = _gla_in(ctx_s, mod_ctx, gains[0], jnp.swapaxes(gla_w_in, 1, 2), w_low,
                                             False, layer=j)
            qkvr_l, low_l = _gla_in(x, mod_lat, gains[0], w_in_bf, w_low, colmajor)
            a_ctx, a_lat = _gla_scan(
                qkvr_c.reshape(batch, CTX_LEN, GLA_MAIN_DIM), low_c.reshape(batch, CTX_LEN, LANES),
                qkvr_l, low_l, w2.astype(BF16), bg, gla_head_gain[j].reshape(1, GLA_HEAD_V))
            w_mix = gla_w_out_bf
        else:
            y1, y2 = _fnet_channels(x, mod_lat, gains[0], w_ch, True)
            a_lat = _fnet_combine(_fnet_sub_dft(w_sub, y1, y2))
            if need_ctx:
                y1, y2 = _fnet_channels(ctx_s, mod_ctx, gains[0], w_ch, False)
                a_ctx = _fnet_tokens(*ctx_tables, y1.reshape(batch, CTX_LEN, D_MODEL),
                                     y2.reshape(batch, CTX_LEN, D_MODEL))
            w_mix = fnet_w_out_bf
        if need_ctx:
            ctx_s, ffn_bf = _out_ffn(ctx_s, a_ctx.reshape(1, batch * CTX_LEN, D_MODEL), mod_ctx,
                                     gains[1], gains[2], gains[3], w_mix, j, (ffn_w_gu, ffn_w_down), False,
                                     layer=i)
        else:
            ffn_bf = (*_cast_gate_up(ffn_w_gu, i), ffn_w_down[i].astype(BF16))
        x1, h = _mix_out(x, a_lat, mod_lat, gains[1], gains[2], w_mix, j, colmajor)
        x = _ffn(x1, h, mod_lat, gains[3], ffn_bf, colmajor)
    return x
```

```python
from functools import partial

import jax
import jax.numpy as jnp
import numpy as np
from jax import lax
from jax.experimental import pallas as pl
from jax.experimental.pallas import tpu as pltpu

D_MODEL = 2048
SEQ = 4096
CTX_LEN = 256
GRID_W = 64
DEPTH = 4
GLA_HEADS = 4
GLA_HEAD_K = 256
GLA_HEAD_V = 512
GLA_KEY_DIM = GLA_HEADS * GLA_HEAD_K
GLA_VALUE_DIM = GLA_HEADS * GLA_HEAD_V
GLA_GATE_RANK = 16
GLA_GATE_TAU = 16.0
GLA_MAIN_DIM = 2 * GLA_KEY_DIM + 2 * GLA_VALUE_DIM
FNET_GROUPS = 4
FNET_GROUP_DIM = D_MODEL // FNET_GROUPS
D_FF = 5632
N_ADA = 6
EPS = 1e-6

LANES = 128
ADA_ROWS = 8
VMEM_LIMIT = 56 * 1024 * 1024

TOK_TILE = 512
IN_TOK_TILE = 1024
FNET_TOK_TILE = 1024
FF_TILE = 512
IN_TILE = 1536
IN_TILE_F32 = 768
FF_TILE_F32 = 256
ADA_TILE = 1536
GLA_CHUNK = 128
GLA_UNROLL = 4
DFT_BM = 1024
DFT_BK = 512
DFT_RADIX = 8
COMBINE_ROWS = 256

BF16 = jnp.bfloat16
F32 = jnp.float32


def _cparams(sem):
    return pltpu.CompilerParams(dimension_semantics=sem, vmem_limit_bytes=VMEM_LIMIT)


def _dot(a, b):
    return jnp.dot(a, b, preferred_element_type=F32)


def _rms(x):
    return x * lax.rsqrt(jnp.mean(x * x, axis=-1, keepdims=True) + EPS)


def _modulate(x, gain, shift, scale):
    return (_rms(x) * gain) * (1.0 + scale) + shift


def _ada_kernel(c_ref, w_ref, b_ref, o_ref):
    c = c_ref[...]
    s = (c * jax.nn.sigmoid(c)).astype(BF16)
    o_ref[...] = _dot(s, w_ref[...].astype(BF16)) + b_ref[...]


def _ada_rows(cond, ada_w, ada_b):
    n_out = N_ADA * D_MODEL
    return pl.pallas_call(
        _ada_kernel,
        grid=(DEPTH, n_out // ADA_TILE),
        in_specs=[
            pl.BlockSpec((ADA_ROWS, D_MODEL), lambda i, n: (0, 0)),
            pl.BlockSpec((None, D_MODEL, ADA_TILE), lambda i, n: (i, 0, n)),
            pl.BlockSpec((None, 1, ADA_TILE), lambda i, n: (i, 0, n)),
        ],
        out_specs=pl.BlockSpec((None, ADA_ROWS, ADA_TILE), lambda i, n: (i, 0, n)),
        out_shape=jax.ShapeDtypeStruct((DEPTH, ADA_ROWS, n_out), F32),
        compiler_params=_cparams(("arbitrary", "arbitrary")),
        name="ada_rows",
    )(cond, ada_w, ada_b.reshape(DEPTH, 1, n_out))


def _tile_spec(colmajor, rows=TOK_TILE):
    if colmajor:
        return pl.BlockSpec((None, GRID_W, rows // GRID_W, D_MODEL), lambda b, t, *_: (b, 0, t, 0))
    return pl.BlockSpec((None, rows, D_MODEL), lambda b, t, *_: (b, t, 0))


def _load_tile(x_ref, colmajor):
    if not colmajor:
        return x_ref[...]
    return jnp.concatenate([x_ref[:, c, :] for c in range(x_ref.shape[1])], axis=0)


def _store_tile(o_ref, val, colmajor):
    if not colmajor:
        o_ref[...] = val
        return
    for c in range(o_ref.shape[1]):
        o_ref[:, c, :] = val[c * GRID_W:(c + 1) * GRID_W, :]


def _mod_spec(j):
    return pl.BlockSpec((None, 1, D_MODEL), lambda b, t, *_: (b, 0, j))


def _row_spec():
    return pl.BlockSpec((1, D_MODEL), lambda b, t, *_: (0, 0))


def _as_stream(x, colmajor):
    if colmajor:
        bq, s, d = x.shape
        return x.reshape(bq, GRID_W, s // GRID_W, d)
    return x


def _gla_in_kernel(x_ref, sh_ref, sc_ref, g_ref, w_ref, wlow_ref, o_ref, low_ref, *rest, colmajor, emit_bf16):
    h_ref = rest[-1]
    n = pl.program_id(2)

    @pl.when(n == 0)
    def _():
        h = _modulate(_load_tile(x_ref, colmajor), g_ref[...], sh_ref[...], sc_ref[...]).astype(BF16)
        h_ref[...] = h
        low_ref[...] = _dot(h, wlow_ref[...])

    w = w_ref[...]
    if emit_bf16:
        w = w.T.astype(BF16)
        rest[0][...] = w
    o_ref[...] = _dot(h_ref[...], w).astype(BF16)


def _gla_in(x, mods, gain, w_in, w_low, colmajor, layer=None):
    bq, s, _ = x.shape
    emit = layer is not None
    tn = IN_TILE_F32 if emit else IN_TILE
    tm = min(IN_TOK_TILE, s)
    if emit:
        assert bq * s == tm, "each weight block must be visited exactly once"
        w_spec = pl.BlockSpec((None, tn, D_MODEL), lambda b, t, n: (layer, n, 0))
    else:
        w_spec = pl.BlockSpec((None, D_MODEL, tn), lambda b, t, n: (n, 0, 0))
    out_specs = [
        pl.BlockSpec((None, tm, tn), lambda b, t, n: (b, t, n)),
        pl.BlockSpec((None, tm, LANES), lambda b, t, n: (b, t, 0)),
    ]
    out_shape = [
        jax.ShapeDtypeStruct((bq, s, GLA_MAIN_DIM), BF16),
        jax.ShapeDtypeStruct((bq, s, LANES), F32),
    ]
    if emit:
        per = IN_TILE // tn
        out_specs.append(pl.BlockSpec((None, D_MODEL, tn), lambda b, t, n: (n // per, 0, n % per)))
        out_shape.append(jax.ShapeDtypeStruct((GLA_MAIN_DIM // IN_TILE, D_MODEL, IN_TILE), BF16))
    return pl.pallas_call(
        partial(_gla_in_kernel, colmajor=colmajor, emit_bf16=emit),
        grid=(bq, s // tm, GLA_MAIN_DIM // tn),
        in_specs=[
            _tile_spec(colmajor, tm), _mod_spec(0), _mod_spec(1), _row_spec(),
            w_spec,
            pl.BlockSpec((D_MODEL, LANES), lambda b, t, n: (0, 0)),
        ],
        out_specs=out_specs,
        out_shape=out_shape,
        scratch_shapes=[pltpu.VMEM((tm, D_MODEL), BF16)],
        compiler_params=_cparams(("arbitrary", "arbitrary", "arbitrary")),
        name="gla_in",
    )(_as_stream(x, colmajor), mods, mods, gain, w_in, w_low)


def _log_gate(z):
    return (jnp.minimum(z, 0.0) - jnp.log(1.0 + jnp.exp(-jnp.abs(z)))) * (1.0 / GLA_GATE_TAU)


def _gla_group(q_ref, k_ref, v_ref, low_ref, w2_ref, bg_ref, tris, masks, s_ref, starts, u):
    c, dk = GLA_CHUNK, GLA_HEAD_K
    span = [pl.ds(pl.multiple_of(starts[d], c), u * c) for d in (0, 1)]
    z = [_dot(low_ref[span[d], :].astype(BF16), w2_ref[d]) + bg_ref[d] for d in (0, 1)]
    g = [_log_gate(zd) for zd in z]
    cums = []
    for d in (0, 1):
        parts = []
        for j in range(u):
            gj = g[d][j * c:(j + 1) * c, :]
            hi = gj.astype(BF16)
            parts += [hi, (gj - hi.astype(F32)).astype(BF16)]
        cum = _dot(tris[d], jnp.concatenate(parts, axis=1))
        cums.append([cum[:, 2 * j * dk:(2 * j + 1) * dk] + cum[:, (2 * j + 1) * dk:(2 * j + 2) * dk]
                     for j in range(u)])
    order = [(d, j if d == 0 else u - 1 - j) for j in range(u) for d in (0, 1)]
    work = {}
    for d, j in order:
        rows = pl.ds(pl.multiple_of(starts[d] + j * c, c), c)
        cum = cums[d][j]
        if d == 1:
            ref, tot = cum[c // 2:c // 2 + 1, :], cum[0:1, :]
        else:
            ref, tot = cum[c // 2 - 1:c // 2, :], cum[c - 1:c, :]
        qf = q_ref[rows, :].astype(F32) * (GLA_HEAD_K ** -0.5)
        kf = k_ref[rows, :].astype(F32)
        work[d, j] = dict(
            rows=rows, tot=tot, v=v_ref[rows, :],
            q_mid=(qf * jnp.exp(cum - ref)).astype(BF16), k_mid=(kf * jnp.exp(ref - cum)).astype(BF16),
            q_dec=(qf * jnp.exp(cum)).astype(BF16), k_end=(kf * jnp.exp(tot - cum)).astype(BF16))
    for key in order:
        w = work[key]
        s = lax.dot_general(w["q_mid"], w["k_mid"], (((1,), (1,)), ((), ())), preferred_element_type=F32)
        w["scores"] = jnp.where(masks[key[0]], s, 0.0).astype(BF16)
    for key in order:
        w = work[key]
        w["o"] = _dot(w["scores"], w["v"])
        w["ds"] = lax.dot_general(w["k_end"], w["v"], (((0,), (0,)), ((), ())), preferred_element_type=F32)
        dec = jnp.broadcast_to(jnp.exp(w["tot"]), (LANES, dk)).T
        w["dec"] = jnp.concatenate([dec] * (GLA_HEAD_V // LANES), axis=1)
    out = []
    state = [s_ref[0], s_ref[1]]
    for d, j in order:
        w = work[d, j]
        out.append((w["rows"], w["o"] + _dot(w["q_dec"], state[d].astype(BF16))))
        state[d] = state[d] * w["dec"] + w["ds"]
    s_ref[0] = state[0]
    s_ref[1] = state[1]
    return out


def _gla_kernel(qc_ref, kc_ref, vc_ref, rc_ref, lowc_ref, ql_ref, kl_ref, vl_ref, rl_ref, lowl_ref,
                w2_ref, bg_ref, hg_ref, oc_ref, ol_ref, s_ref, acc_ref):
    c = GLA_CHUNK
    row = lax.broadcasted_iota(jnp.int32, (c, c), 0)
    col = lax.broadcasted_iota(jnp.int32, (c, c), 1)
    masks = (col <= row, col >= row)
    tris = tuple(m.astype(BF16) for m in masks)
    hg = hg_ref[...]
    s_ref[...] = jnp.zeros_like(s_ref)

    def phase(q_ref, k_ref, v_ref, r_ref, low_ref, o_ref, length):
        u = min(GLA_UNROLL, length // (2 * c))
        n_groups = length // (u * c)
        half = n_groups // 2

        def group(n):
            starts = (n * (u * c), (n_groups - 1 - n) * (u * c))
            return _gla_group(q_ref, k_ref, v_ref, low_ref, w2_ref, bg_ref, tris, masks, s_ref, starts, u)

        def first_visit(n, carry):
            for rows, o in group(n):
                acc_ref[rows, :] = o
            return carry

        def second_visit(n, carry):
            for rows, o in group(n):
                o = _rms(o + acc_ref[rows, :]) * hg
                r = r_ref[rows, :].astype(F32)
                o_ref[rows, :] = (o * (r * jax.nn.sigmoid(r))).astype(BF16)
            return carry

        lax.fori_loop(0, half, first_visit, 0)
        lax.fori_loop(half, n_groups, second_visit, 0)

    phase(qc_ref, kc_ref, vc_ref, rc_ref, lowc_ref, oc_ref, CTX_LEN)
    phase(ql_ref, kl_ref, vl_ref, rl_ref, lowl_ref, ol_ref, SEQ)


def _gla_scan(qkvr_c, low_c, qkvr_l, low_l, w2, bg, head_gain):
    b = qkvr_l.shape[0]
    nk = GLA_KEY_DIM // GLA_HEAD_K

    def stream_specs(length):
        return [
            pl.BlockSpec((None, length, GLA_HEAD_K), lambda i, h: (i, 0, h)),
            pl.BlockSpec((None, length, GLA_HEAD_K), lambda i, h: (i, 0, nk + h)),
            pl.BlockSpec((None, length, GLA_HEAD_V), lambda i, h: (i, 0, nk + h)),
            pl.BlockSpec((None, length, GLA_HEAD_V), lambda i, h: (i, 0, 2 * nk + h)),
            pl.BlockSpec((None, length, LANES), lambda i, h: (i, 0, 0)),
        ]

    def out_spec(length):
        return pl.BlockSpec((None, length, GLA_HEAD_V), lambda i, h: (i, 0, h))

    return pl.pallas_call(
        _gla_kernel,
        grid=(b, GLA_HEADS),
        in_specs=stream_specs(CTX_LEN) + stream_specs(SEQ) + [
            pl.BlockSpec((2, LANES, GLA_HEAD_K), lambda i, h: (0, 0, h)),
            pl.BlockSpec((2, 1, GLA_HEAD_K), lambda i, h: (0, 0, h)),
            pl.BlockSpec((1, GLA_HEAD_V), lambda i, h: (0, 0)),
        ],
        out_specs=[out_spec(CTX_LEN), out_spec(SEQ)],
        out_shape=[
            jax.ShapeDtypeStruct((b, CTX_LEN, GLA_VALUE_DIM), BF16),
            jax.ShapeDtypeStruct((b, SEQ, GLA_VALUE_DIM), BF16),
        ],
        scratch_shapes=[
            pltpu.VMEM((2, GLA_HEAD_K, GLA_HEAD_V), F32),
            pltpu.VMEM((SEQ, GLA_HEAD_V), F32),
        ],
        compiler_params=_cparams(("arbitrary", "arbitrary")),
        name="gla_scan",
    )(qkvr_c, qkvr_c, qkvr_c, qkvr_c, low_c, qkvr_l, qkvr_l, qkvr_l, qkvr_l, low_l, w2, bg, head_gain)


def _fnet_ch_kernel(x_ref, sh_ref, sc_ref, g_ref, w_ref, y1_ref, y2_ref, *, by_phase):
    if by_phase:
        x = jnp.concatenate([x_ref[:, s, :] for s in range(DFT_RADIX)], axis=0)
    else:
        x = x_ref[...]
    h = _modulate(x, g_ref[...], sh_ref[...], sc_ref[...]).astype(BF16)
    w = w_ref[...]
    gd = FNET_GROUP_DIM
    rows = x_ref.shape[0]
    for g in range(FNET_GROUPS):
        r = _dot(h[:, g * gd:(g + 1) * gd], w).astype(BF16)
        cols = slice(g * gd, (g + 1) * gd)
        if by_phase:
            for s in range(DFT_RADIX):
                y1_ref[s, :, cols] = r[s * rows:(s + 1) * rows, :gd]
                y2_ref[s, :, cols] = r[s * rows:(s + 1) * rows, gd:]
        else:
            y1_ref[:, cols] = r[:, :gd]
            y2_ref[:, cols] = r[:, gd:]


def _fnet_channels(x, mods, gain, w_ch, by_phase):
    bq, s, _ = x.shape
    tm = FNET_TOK_TILE if by_phase else TOK_TILE
    if by_phase:
        rows = tm // DFT_RADIX
        x = x.reshape(bq, s // DFT_RADIX, DFT_RADIX, D_MODEL)
        x_spec = pl.BlockSpec((None, rows, DFT_RADIX, D_MODEL), lambda b, t: (b, t, 0, 0))
        y_spec = pl.BlockSpec((None, DFT_RADIX, rows, D_MODEL), lambda b, t: (b, 0, t, 0))
        out = jax.ShapeDtypeStruct((bq, DFT_RADIX, s // DFT_RADIX, D_MODEL), BF16)
    else:
        x_spec = y_spec = _tile_spec(False)
        out = jax.ShapeDtypeStruct((bq, s, D_MODEL), BF16)
    return pl.pallas_call(
        partial(_fnet_ch_kernel, by_phase=by_phase),
        grid=(bq, s // tm),
        in_specs=[
            x_spec, _mod_spec(0), _mod_spec(1), _row_spec(),
            pl.BlockSpec((FNET_GROUP_DIM, 2 * FNET_GROUP_DIM), lambda b, t: (0, 0)),
        ],
        out_specs=[y_spec, y_spec],
        out_shape=[out, out],
        compiler_params=_cparams(("arbitrary", "arbitrary")),
        name="fnet_channels",
    )(x, mods, mods, gain, w_ch)


def _fnet_sub_kernel(w_ref, y1_ref, y2_ref, o_ref):
    n = y1_ref.shape[0]
    y1, y2 = y1_ref[...], y2_ref[...]
    k1 = _dot(w_ref[0], y1 + y2)
    o_ref[n:, :] = (-(k1 + _dot(w_ref[1], y1))).astype(BF16)
    o_ref[:n, :] = (k1 - _dot(w_ref[2], y2)).astype(BF16)


def _fnet_sub_dft(w_sub, y1, y2):
    b, radix, n, _ = y1.shape
    y_spec = pl.BlockSpec((None, None, n, D_MODEL), lambda i, s: (i, s, 0, 0))
    return pl.pallas_call(
        _fnet_sub_kernel,
        grid=(b, radix),
        in_specs=[pl.BlockSpec((None, 3, n, n), lambda i, s: (s, 0, 0, 0)), y_spec, y_spec],
        out_specs=pl.BlockSpec((None, None, 2 * n, D_MODEL), lambda i, s: (i, s, 0, 0)),
        out_shape=jax.ShapeDtypeStruct((b, radix, 2 * n, D_MODEL), BF16),
        compiler_params=_cparams(("arbitrary", "arbitrary")),
        name="fnet_sub_dft",
    )(w_sub, y1, y2)


def _fft8_real(xs):
    r2 = float(np.sqrt(0.5))

    def add(a, b):
        return a[0] + b[0], a[1] + b[1]

    def sub(a, b):
        return a[0] - b[0], a[1] - b[1]

    def mul_neg_i(a):
        return a[1], -a[0]

    u = [add(xs[k], xs[k + 4]) for k in range(4)]
    d = [sub(xs[k], xs[k + 4]) for k in range(4)]
    v = [d[0],
         ((d[1][0] + d[1][1]) * r2, (d[1][1] - d[1][0]) * r2),
         mul_neg_i(d[2]),
         ((d[3][1] - d[3][0]) * r2, -(d[3][0] + d[3][1]) * r2)]

    def fft4_real(y):
        p0, p1 = add(y[0], y[2]), add(y[1], y[3])
        q0, q1 = sub(y[0], y[2]), mul_neg_i(sub(y[1], y[3]))
        return [p0[0] + p1[0], q0[0] + q1[0], p0[0] - p1[0], q0[0] - q1[0]]

    even, odd = fft4_real(u), fft4_real(v)
    return [even[0], odd[0], even[1], odd[1], even[2], odd[2], even[3], odd[3]]


def _fnet_combine_kernel(re_ref, im_ref, o_ref):
    bp = re_ref.shape[1]
    sub_rows = 16

    def body(rg, carry):
        rows = pl.ds(pl.multiple_of(rg * sub_rows, sub_rows), sub_rows)
        for lc in range(D_MODEL // LANES):
            lanes = slice(lc * LANES, (lc + 1) * LANES)
            xs = [(re_ref[s, rows, lanes].astype(F32), im_ref[s, rows, lanes].astype(F32))
                  for s in range(DFT_RADIX)]
            for q, z in enumerate(_fft8_real(xs)):
                o_ref[q, rows, lanes] = z.astype(BF16)
        return carry

    lax.fori_loop(0, bp // sub_rows, body, 0)


def _fnet_combine(parts):
    b, radix, n2, _ = parts.shape
    n = n2 // 2
    bp = COMBINE_ROWS
    part_spec = lambda off: pl.BlockSpec((None, radix, bp, D_MODEL), lambda i, p: (i, 0, off + p, 0))
    z = pl.pallas_call(
        _fnet_combine_kernel,
        grid=(b, n // bp),
        in_specs=[part_spec(0), part_spec(n // bp)],
        out_specs=pl.BlockSpec((None, radix, bp, D_MODEL), lambda i, p: (i, 0, p, 0)),
        out_shape=jax.ShapeDtypeStruct((b, radix, n, D_MODEL), BF16),
        compiler_params=_cparams(("arbitrary", "arbitrary")),
        name="fnet_combine",
    )(parts, parts)
    return z.reshape(b, radix * n, D_MODEL)


def _fnet_tok_kernel(gc_ref, gs_ref, y1_ref, y2_ref, o_ref, acc_ref):
    k = pl.program_id(2)

    @pl.when(k == 0)
    def _():
        acc_ref[...] = jnp.zeros_like(acc_ref)

    acc_ref[...] += _dot(gc_ref[...], y1_ref[...]) + _dot(gs_ref[...], y2_ref[...])

    @pl.when(k == pl.num_programs(2) - 1)
    def _():
        o_ref[...] = acc_ref[...].astype(BF16)


def _fnet_tokens(gc, gs, y1, y2):
    b, length, _ = y1.shape
    bm, bk = min(DFT_BM, length), min(DFT_BK, length)
    y_spec = pl.BlockSpec((None, bk, D_MODEL), lambda i, m, k: (i, k, 0))
    g_spec = pl.BlockSpec((bm, bk), lambda i, m, k: (m, k))
    return pl.pallas_call(
        _fnet_tok_kernel,
        grid=(b, length // bm, length // bk),
        in_specs=[g_spec, g_spec, y_spec, y_spec],
        out_specs=pl.BlockSpec((None, bm, D_MODEL), lambda i, m, k: (i, m, 0)),
        out_shape=jax.ShapeDtypeStruct((b, length, D_MODEL), BF16),
        scratch_shapes=[pltpu.VMEM((bm, D_MODEL), F32)],
        compiler_params=_cparams(("arbitrary", "arbitrary", "arbitrary")),
        name="fnet_tokens",
    )(gc, gs, y1, y2)


def _dft_tables(n, split):
    lp = jnp.arange(n, dtype=jnp.int32)[:, None]
    hi = jnp.arange(n // split, dtype=jnp.int32)[None, :]
    lo = jnp.arange(split, dtype=jnp.int32)[None, :]
    ang_hi = ((lp * hi) % (n // split)).astype(F32) * (2.0 * np.pi * split / n)
    ang_lo = ((lp * lo) % n).astype(F32) * (2.0 * np.pi / n)
    ch, sh_, cl, sl = jnp.cos(ang_hi), jnp.sin(ang_hi), jnp.cos(ang_lo), jnp.sin(ang_lo)
    cos = ch[:, :, None] * cl[:, None, :] - sh_[:, :, None] * sl[:, None, :]
    sin = sh_[:, :, None] * cl[:, None, :] + ch[:, :, None] * sl[:, None, :]
    return cos.reshape(n, n), sin.reshape(n, n)


def _out_ffn_kernel(x_ref, a_ref, gtm_ref, shf_ref, scf_ref, gtf_ref, g1_ref, g2_ref, g3_ref,
                    wout_ref, wg_ref, wu_ref, wd_ref, o_ref, *rest, colmajor, emit_bf16):
    h_ref, acc_ref = rest[-2:]
    k = pl.program_id(2)

    x1_ref = o_ref.reshape(TOK_TILE, D_MODEL) if colmajor else o_ref

    @pl.when(k == 0)
    def _():
        y = _dot(a_ref[...], wout_ref[...])
        x1 = x_ref[...] + gtm_ref[...] * (_rms(y) * g1_ref[...])
        x1_ref[...] = x1
        h_ref[...] = _modulate(x1, g2_ref[...], shf_ref[...], scf_ref[...]).astype(BF16)
        acc_ref[...] = jnp.zeros_like(acc_ref)

    wg, wu, wd = wg_ref[...], wu_ref[...], wd_ref[...]
    if emit_bf16:
        wg, wu, wd = wg.astype(BF16), wu.astype(BF16), wd.astype(BF16)
        rest[0][...] = wg
        rest[1][...] = wu
        rest[2][...] = wd
    h = h_ref[...]
    gate = _dot(h, wg)
    up = _dot(h, wu)
    act = (gate * jax.nn.sigmoid(gate) * up).astype(BF16)
    acc_ref[...] += _dot(act, wd)

    @pl.when(k == pl.num_programs(2) - 1)
    def _():
        out = x1_ref[...] + gtf_ref[...] * (_rms(acc_ref[...]) * g3_ref[...])
        _store_tile(o_ref, out, colmajor)


def _out_ffn(x, a, mods, g1, g2, g3, w_out, mix_layer, ffn_w, colmajor, layer=None):
    bq, s, _ = x.shape
    emit = layer is not None
    tf = FF_TILE_F32 if emit else FF_TILE
    nf = D_FF // tf
    if emit:
        assert bq * s == TOK_TILE, "each weight block must be visited exactly once"
        w_gu, w_down = ffn_w
        weights = (w_gu, w_gu, w_down)
        w_specs = [
            pl.BlockSpec((None, D_MODEL, tf), lambda b, t, k: (layer, 0, k)),
            pl.BlockSpec((None, D_MODEL, tf), lambda b, t, k: (layer, 0, nf + k)),
            pl.BlockSpec((None, tf, D_MODEL), lambda b, t, k: (layer, k, 0)),
        ]
    else:
        weights = ffn_w
        w_specs = [
            pl.BlockSpec((None, D_MODEL, tf), lambda b, t, k: (k, 0, 0)),
            pl.BlockSpec((None, D_MODEL, tf), lambda b, t, k: (k, 0, 0)),
            pl.BlockSpec((tf, D_MODEL), lambda b, t, k: (k, 0)),
        ]
    out_specs = [_tile_spec(colmajor)]
    out_shape = [jax.ShapeDtypeStruct(_as_stream(x, colmajor).shape, F32)]
    if emit:
        per = FF_TILE // tf
        out_specs += [
            pl.BlockSpec((None, D_MODEL, tf), lambda b, t, k: (k // per, 0, k % per)),
            pl.BlockSpec((None, D_MODEL, tf), lambda b, t, k: (k // per, 0, k % per)),
            pl.BlockSpec((tf, D_MODEL), lambda b, t, k: (k, 0)),
        ]
        out_shape += [
            jax.ShapeDtypeStruct((D_FF // FF_TILE, D_MODEL, FF_TILE), BF16),
            jax.ShapeDtypeStruct((D_FF // FF_TILE, D_MODEL, FF_TILE), BF16),
            jax.ShapeDtypeStruct((D_FF, D_MODEL), BF16),
        ]
    res = pl.pallas_call(
        partial(_out_ffn_kernel, colmajor=colmajor, emit_bf16=emit),
        grid=(bq, s // TOK_TILE, nf),
        in_specs=[
            _tile_spec(False), _tile_spec(False),
            _mod_spec(2), _mod_spec(3), _mod_spec(4), _mod_spec(5),
            _row_spec(), _row_spec(), _row_spec(),
            pl.BlockSpec((None, D_MODEL, D_MODEL), lambda b, t, k: (mix_layer, 0, 0),
                         pipeline_mode=pl.Buffered(1)),
        ] + w_specs,
        out_specs=out_specs,
        out_shape=out_shape,
        scratch_shapes=[
            pltpu.VMEM((TOK_TILE, D_MODEL), BF16),
            pltpu.VMEM((TOK_TILE, D_MODEL), F32),
        ],
        compiler_params=_cparams(("arbitrary", "arbitrary", "arbitrary")),
        name="out_ffn",
    )(x, a, mods, mods, mods, mods, g1, g2, g3, w_out, *weights)
    if emit:
        return res[0].reshape(x.shape), tuple(res[1:])
    return res[0].reshape(x.shape)


def _cast_gu_kernel(g_ref, u_ref, go_ref, uo_ref):
    go_ref[...] = g_ref[...].astype(BF16)
    uo_ref[...] = u_ref[...].astype(BF16)


def _cast_gate_up(w_gu, layer):
    nf = D_FF // FF_TILE
    out = jax.ShapeDtypeStruct((nf, D_MODEL, FF_TILE), BF16)
    o_spec = pl.BlockSpec((None, D_MODEL, FF_TILE), lambda k: (k, 0, 0))
    return pl.pallas_call(
        _cast_gu_kernel,
        grid=(nf,),
        in_specs=[
            pl.BlockSpec((None, D_MODEL, FF_TILE), lambda k: (layer, 0, k)),
            pl.BlockSpec((None, D_MODEL, FF_TILE), lambda k: (layer, 0, nf + k)),
        ],
        out_specs=[o_spec, o_spec],
        out_shape=[out, out],
        compiler_params=_cparams(("arbitrary",)),
        name="cast_gate_up",
    )(w_gu, w_gu)


def _wants_colmajor(i):
    return i < DEPTH and i % 2 == 0 and (i // 2) % 2 == 1


def kernel(x, c, ctx, c_ctx, ada_w, ada_b, norm_gains, gla_w_in, gla_wg2_f, gla_bg_f, gla_wg2_b, gla_bg_b,
           gla_head_gain, gla_w_out, fnet_w_out, ffn_w_gu, ffn_w_down):
    batch = x.shape[0]
    cond = jnp.zeros((ADA_ROWS, D_MODEL), F32).at[:batch].set(c).at[batch].set(c_ctx)
    mods = _ada_rows(cond, ada_w, ada_b)
    ctx_s = ctx.reshape(1, batch * CTX_LEN, D_MODEL)

    n_ch = FNET_GROUP_DIM
    cos_c, sin_c = _dft_tables(n_ch, 16)
    w_ch = (jnp.concatenate([cos_c, sin_c], axis=1) * (n_ch ** -0.5)).astype(BF16)
    cos_t, sin_t = _dft_tables(CTX_LEN, 16)
    ctx_tables = ((cos_t * (CTX_LEN ** -0.5)).astype(BF16), (sin_t * -(CTX_LEN ** -0.5)).astype(BF16))
    n_sub = SEQ // DFT_RADIX
    cos_s, sin_s = _dft_tables(n_sub, 16)
    tw_ang = ((jnp.arange(DFT_RADIX, dtype=jnp.int32)[:, None] * jnp.arange(n_sub, dtype=jnp.int32)[None, :])
              % SEQ).astype(F32) * (2.0 * np.pi / SEQ)
    tw_cos, tw_sin = jnp.cos(tw_ang)[:, :, None], jnp.sin(tw_ang)[:, :, None]
    cos_p = cos_s[None] * tw_cos - sin_s[None] * tw_sin
    sin_p = sin_s[None] * tw_cos + cos_s[None] * tw_sin
    w_sub = (jnp.stack([cos_p, sin_p - cos_p, cos_p + sin_p], axis=1) * (SEQ ** -0.5)).astype(BF16)

    gla_w_out_bf = gla_w_out.astype(BF16)
    fnet_w_out_bf = fnet_w_out.astype(BF16)

    for i in range(DEPTH):
        need_ctx = i < DEPTH - 1
        j = i // 2
        mod_lat = mods[i, :batch].reshape(batch, 1, N_ADA * D_MODEL)
        mod_ctx = mods[i, batch:batch + 1].reshape(1, 1, N_ADA * D_MODEL)
        gains = [norm_gains[i, n].reshape(1, D_MODEL) for n in range(4)]
        colmajor = _wants_colmajor(i) != _wants_colmajor(i + 1)
        if i % 2 == 0:
            w_low = lax.optimization_barrier(gla_w_in[j, :, GLA_MAIN_DIM:])
            w_low = jnp.pad(w_low, ((0, 0), (0, LANES - 2 * GLA_GATE_RANK))).astype(BF16)
            w2 = jnp.zeros((2, LANES, GLA_KEY_DIM), F32)
            w2 = w2.at[0, :GLA_GATE_RANK].set(gla_wg2_f[j]).at[1, GLA_GATE_RANK:2 * GLA_GATE_RANK].set(gla_wg2_b[j])
            bg = jnp.stack([gla_bg_f[j], gla_bg_b[j]]).reshape(2, 1, GLA_KEY_DIM)
            qkvr_c, low_c, w_in_bf = _gla_in(ctx_s, mod_ctx, gains[0], jnp.swapaxes(gla_w_in, 1, 2), w_low,
                                             False, layer=j)
            qkvr_l, low_l = _gla_in(x, mod_lat, gains[0], w_in_bf, w_low, False)
            a_ctx, a_lat = _gla_scan(
                qkvr_c.reshape(batch, CTX_LEN, GLA_MAIN_DIM), low_c.reshape(batch, CTX_LEN, LANES),
                qkvr_l, low_l, w2.astype(BF16), bg, gla_head_gain[j].reshape(1, GLA_HEAD_V))
            w_mix = gla_w_out_bf
        else:
            y1, y2 = _fnet_channels(x, mod_lat, gains[0], w_ch, True)
            a_lat = _fnet_combine(_fnet_sub_dft(w_sub, y1, y2))
            if need_ctx:
                y1, y2 = _fnet_channels(ctx_s, mod_ctx, gains[0], w_ch, False)
                a_ctx = _fnet_tokens(*ctx_tables, y1.reshape(batch, CTX_LEN, D_MODEL),
                                     y2.reshape(batch, CTX_LEN, D_MODEL))
            w_mix = fnet_w_out_bf
        if need_ctx:
            ctx_s, ffn_bf = _out_ffn(ctx_s, a_ctx.reshape(1, batch * CTX_LEN, D_MODEL), mod_ctx,
                                     gains[1], gains[2], gains[3], w_mix, j, (ffn_w_gu, ffn_w_down), False,
                                     layer=i)
        else:
            ffn_bf = (*_cast_gate_up(ffn_w_gu, i), ffn_w_down[i].astype(BF16))
        x = _out_ffn(x, a_lat, mod_lat, gains[1], gains[2], gains[3], w_mix, j, ffn_bf, colmajor)
    return x
```

```python
from functools import partial

import jax
import jax.numpy as jnp
import numpy as np
from jax import lax
from jax.experimental import pallas as pl
from jax.experimental.pallas import tpu as pltpu

D_MODEL = 2048
SEQ = 4096
CTX_LEN = 256
GRID_W = 64
DEPTH = 4
GLA_HEADS = 4
GLA_HEAD_K = 256
GLA_HEAD_V = 512
GLA_KEY_DIM = GLA_HEADS * GLA_HEAD_K
GLA_VALUE_DIM = GLA_HEADS * GLA_HEAD_V
GLA_GATE_RANK = 16
GLA_GATE_TAU = 16.0
GLA_MAIN_DIM = 2 * GLA_KEY_DIM + 2 * GLA_VALUE_DIM
FNET_GROUPS = 4
FNET_GROUP_DIM = D_MODEL // FNET_GROUPS
D_FF = 5632
N_ADA = 6
EPS = 1e-6

LANES = 128
ADA_ROWS = 8
VMEM_LIMIT = 56 * 1024 * 1024

TOK_TILE = 512
IN_TOK_TILE = 1024
FNET_TOK_TILE = 1024
FF_TILE = 512
IN_TILE = 2048
IN_TILE_F32 = 1024
FF_TILE_F32 = 256
ADA_TILE = 1536
GLA_CHUNK = 128
GLA_UNROLL = 4
DFT_BM = 1024
DFT_BK = 512
DFT_RADIX = 8
COMBINE_ROWS = 256

BF16 = jnp.bfloat16
F32 = jnp.float32


def _cparams(sem):
    return pltpu.CompilerParams(dimension_semantics=sem, vmem_limit_bytes=VMEM_LIMIT)


def _dot(a, b):
    return jnp.dot(a, b, preferred_element_type=F32)


def _rms(x):
    return x * lax.rsqrt(jnp.mean(x * x, axis=-1, keepdims=True) + EPS)


def _modulate(x, gain, shift, scale):
    return (_rms(x) * gain) * (1.0 + scale) + shift


def _ada_kernel(c_ref, w_ref, b_ref, o_ref):
    c = c_ref[...]
    s = (c * jax.nn.sigmoid(c)).astype(BF16)
    o_ref[...] = _dot(s, w_ref[...].astype(BF16)) + b_ref[...]


def _ada_rows(cond, ada_w, ada_b):
    n_out = N_ADA * D_MODEL
    return pl.pallas_call(
        _ada_kernel,
        grid=(DEPTH, n_out // ADA_TILE),
        in_specs=[
            pl.BlockSpec((ADA_ROWS, D_MODEL), lambda i, n: (0, 0)),
            pl.BlockSpec((None, D_MODEL, ADA_TILE), lambda i, n: (i, 0, n)),
            pl.BlockSpec((None, 1, ADA_TILE), lambda i, n: (i, 0, n)),
        ],
        out_specs=pl.BlockSpec((None, ADA_ROWS, ADA_TILE), lambda i, n: (i, 0, n)),
        out_shape=jax.ShapeDtypeStruct((DEPTH, ADA_ROWS, n_out), F32),
        compiler_params=_cparams(("arbitrary", "arbitrary")),
        name="ada_rows",
    )(cond, ada_w, ada_b.reshape(DEPTH, 1, n_out))


def _tile_spec(colmajor, rows=TOK_TILE):
    if colmajor:
        return pl.BlockSpec((None, GRID_W, rows // GRID_W, D_MODEL), lambda b, t, *_: (b, 0, t, 0))
    return pl.BlockSpec((None, rows, D_MODEL), lambda b, t, *_: (b, t, 0))


def _load_tile(x_ref, colmajor):
    if not colmajor:
        return x_ref[...]
    return jnp.concatenate([x_ref[:, c, :] for c in range(x_ref.shape[1])], axis=0)


def _store_tile(o_ref, val, colmajor):
    if not colmajor:
        o_ref[...] = val
        return
    for c in range(o_ref.shape[1]):
        o_ref[:, c, :] = val[c * GRID_W:(c + 1) * GRID_W, :]


def _mod_spec(j):
    return pl.BlockSpec((None, 1, D_MODEL), lambda b, t, *_: (b, 0, j))


def _row_spec():
    return pl.BlockSpec((1, D_MODEL), lambda b, t, *_: (0, 0))


def _as_stream(x, colmajor):
    if colmajor:
        bq, s, d = x.shape
        return x.reshape(bq, GRID_W, s // GRID_W, d)
    return x


def _gla_in_kernel(x_ref, sh_ref, sc_ref, g_ref, w_ref, wlow_ref, o_ref, low_ref, *rest, colmajor, emit_bf16):
    h_ref = rest[-1]
    n = pl.program_id(2)

    @pl.when(n == 0)
    def _():
        h = _modulate(_load_tile(x_ref, colmajor), g_ref[...], sh_ref[...], sc_ref[...]).astype(BF16)
        h_ref[...] = h
        low_ref[...] = _dot(h, wlow_ref[...])

    w = w_ref[...]
    if emit_bf16:
        w = w.T.astype(BF16)
        rest[0][...] = w
    o_ref[...] = _dot(h_ref[...], w).astype(BF16)


def _gla_in(x, mods, gain, w_in, w_low, colmajor, layer=None):
    bq, s, _ = x.shape
    emit = layer is not None
    tn = IN_TILE_F32 if emit else IN_TILE
    tm = min(IN_TOK_TILE, s)
    if emit:
        assert bq * s == tm, "each weight block must be visited exactly once"
        w_spec = pl.BlockSpec((None, tn, D_MODEL), lambda b, t, n: (layer, n, 0))
    else:
        w_spec = pl.BlockSpec((None, D_MODEL, tn), lambda b, t, n: (n, 0, 0))
    out_specs = [
        pl.BlockSpec((None, tm, tn), lambda b, t, n: (b, t, n)),
        pl.BlockSpec((None, tm, LANES), lambda b, t, n: (b, t, 0)),
    ]
    out_shape = [
        jax.ShapeDtypeStruct((bq, s, GLA_MAIN_DIM), BF16),
        jax.ShapeDtypeStruct((bq, s, LANES), F32),
    ]
    if emit:
        per = IN_TILE // tn
        out_specs.append(pl.BlockSpec((None, D_MODEL, tn), lambda b, t, n: (n // per, 0, n % per)))
        out_shape.append(jax.ShapeDtypeStruct((GLA_MAIN_DIM // IN_TILE, D_MODEL, IN_TILE), BF16))
    return pl.pallas_call(
        partial(_gla_in_kernel, colmajor=colmajor, emit_bf16=emit),
        grid=(bq, s // tm, GLA_MAIN_DIM // tn),
        in_specs=[
            _tile_spec(colmajor, tm), _mod_spec(0), _mod_spec(1), _row_spec(),
            w_spec,
            pl.BlockSpec((D_MODEL, LANES), lambda b, t, n: (0, 0)),
        ],
        out_specs=out_specs,
        out_shape=out_shape,
        scratch_shapes=[pltpu.VMEM((tm, D_MODEL), BF16)],
        compiler_params=_cparams(("arbitrary", "arbitrary", "arbitrary")),
        name="gla_in",
    )(_as_stream(x, colmajor), mods, mods, gain, w_in, w_low)


def _log_gate(z):
    return (jnp.minimum(z, 0.0) - jnp.log(1.0 + jnp.exp(-jnp.abs(z)))) * (1.0 / GLA_GATE_TAU)


def _gla_group(q_ref, k_ref, v_ref, low_ref, w2_ref, bg_ref, tris, masks, s_ref, starts, u):
    c, dk = GLA_CHUNK, GLA_HEAD_K
    span = [pl.ds(pl.multiple_of(starts[d], c), u * c) for d in (0, 1)]
    z = [_dot(low_ref[span[d], :].astype(BF16), w2_ref[d]) + bg_ref[d] for d in (0, 1)]
    g = [_log_gate(zd) for zd in z]
    cums = []
    for d in (0, 1):
        parts = []
        for j in range(u):
            gj = g[d][j * c:(j + 1) * c, :]
            hi = gj.astype(BF16)
            parts += [hi, (gj - hi.astype(F32)).astype(BF16)]
        cum = _dot(tris[d], jnp.concatenate(parts, axis=1))
        cums.append([cum[:, 2 * j * dk:(2 * j + 1) * dk] + cum[:, (2 * j + 1) * dk:(2 * j + 2) * dk]
                     for j in range(u)])
    order = [(d, j if d == 0 else u - 1 - j) for j in range(u) for d in (0, 1)]
    work = {}
    for d, j in order:
        rows = pl.ds(pl.multiple_of(starts[d] + j * c, c), c)
        cum = cums[d][j]
        if d == 1:
            ref, tot = cum[c // 2:c // 2 + 1, :], cum[0:1, :]
        else:
            ref, tot = cum[c // 2 - 1:c // 2, :], cum[c - 1:c, :]
        qf = q_ref[rows, :].astype(F32) * (GLA_HEAD_K ** -0.5)
        kf = k_ref[rows, :].astype(F32)
        work[d, j] = dict(
            rows=rows, tot=tot, v=v_ref[rows, :],
            q_mid=(qf * jnp.exp(cum - ref)).astype(BF16), k_mid=(kf * jnp.exp(ref - cum)).astype(BF16),
            q_dec=(qf * jnp.exp(cum)).astype(BF16), k_end=(kf * jnp.exp(tot - cum)).astype(BF16))
    for key in order:
        w = work[key]
        s = lax.dot_general(w["q_mid"], w["k_mid"], (((1,), (1,)), ((), ())), preferred_element_type=F32)
        w["scores"] = jnp.where(masks[key[0]], s, 0.0).astype(BF16)
    for key in order:
        w = work[key]
        w["o"] = _dot(w["scores"], w["v"])
        w["ds"] = lax.dot_general(w["k_end"], w["v"], (((0,), (0,)), ((), ())), preferred_element_type=F32)
        dec = jnp.broadcast_to(jnp.exp(w["tot"]), (LANES, dk)).T
        w["dec"] = jnp.concatenate([dec] * (GLA_HEAD_V // LANES), axis=1)
    out = []
    state = [s_ref[0], s_ref[1]]
    for d, j in order:
        w = work[d, j]
        out.append((w["rows"], w["o"] + _dot(w["q_dec"], state[d].astype(BF16))))
        state[d] = state[d] * w["dec"] + w["ds"]
    s_ref[0] = state[0]
    s_ref[1] = state[1]
    return out


def _gla_kernel(qc_ref, kc_ref, vc_ref, rc_ref, lowc_ref, ql_ref, kl_ref, vl_ref, rl_ref, lowl_ref,
                w2_ref, bg_ref, hg_ref, oc_ref, ol_ref, s_ref, acc_ref):
    c = GLA_CHUNK
    row = lax.broadcasted_iota(jnp.int32, (c, c), 0)
    col = lax.broadcasted_iota(jnp.int32, (c, c), 1)
    masks = (col <= row, col >= row)
    tris = tuple(m.astype(BF16) for m in masks)
    hg = hg_ref[...]
    s_ref[...] = jnp.zeros_like(s_ref)

    def phase(q_ref, k_ref, v_ref, r_ref, low_ref, o_ref, length):
        u = min(GLA_UNROLL, length // (2 * c))
        n_groups = length // (u * c)
        half = n_groups // 2

        def group(n):
            starts = (n * (u * c), (n_groups - 1 - n) * (u * c))
            return _gla_group(q_ref, k_ref, v_ref, low_ref, w2_ref, bg_ref, tris, masks, s_ref, starts, u)

        def first_visit(n, carry):
            for rows, o in group(n):
                acc_ref[rows, :] = o
            return carry

        def second_visit(n, carry):
            for rows, o in group(n):
                o = _rms(o + acc_ref[rows, :]) * hg
                r = r_ref[rows, :].astype(F32)
                o_ref[rows, :] = (o * (r * jax.nn.sigmoid(r))).astype(BF16)
            return carry

        lax.fori_loop(0, half, first_visit, 0)
        lax.fori_loop(half, n_groups, second_visit, 0)

    phase(qc_ref, kc_ref, vc_ref, rc_ref, lowc_ref, oc_ref, CTX_LEN)
    phase(ql_ref, kl_ref, vl_ref, rl_ref, lowl_ref, ol_ref, SEQ)


def _gla_scan(qkvr_c, low_c, qkvr_l, low_l, w2, bg, head_gain):
    b = qkvr_l.shape[0]
    nk = GLA_KEY_DIM // GLA_HEAD_K

    def stream_specs(length):
        return [
            pl.BlockSpec((None, length, GLA_HEAD_K), lambda i, h: (i, 0, h)),
            pl.BlockSpec((None, length, GLA_HEAD_K), lambda i, h: (i, 0, nk + h)),
            pl.BlockSpec((None, length, GLA_HEAD_V), lambda i, h: (i, 0, nk + h)),
            pl.BlockSpec((None, length, GLA_HEAD_V), lambda i, h: (i, 0, 2 * nk + h)),
            pl.BlockSpec((None, length, LANES), lambda i, h: (i, 0, 0)),
        ]

    def out_spec(length):
        return pl.BlockSpec((None, length, GLA_HEAD_V), lambda i, h: (i, 0, h))

    return pl.pallas_call(
        _gla_kernel,
        grid=(b, GLA_HEADS),
        in_specs=stream_specs(CTX_LEN) + stream_specs(SEQ) + [
            pl.BlockSpec((2, LANES, GLA_HEAD_K), lambda i, h: (0, 0, h)),
            pl.BlockSpec((2, 1, GLA_HEAD_K), lambda i, h: (0, 0, h)),
            pl.BlockSpec((1, GLA_HEAD_V), lambda i, h: (0, 0)),
        ],
        out_specs=[out_spec(CTX_LEN), out_spec(SEQ)],
        out_shape=[
            jax.ShapeDtypeStruct((b, CTX_LEN, GLA_VALUE_DIM), BF16),
            jax.ShapeDtypeStruct((b, SEQ, GLA_VALUE_DIM), BF16),
        ],
        scratch_shapes=[
            pltpu.VMEM((2, GLA_HEAD_K, GLA_HEAD_V), F32),
            pltpu.VMEM((SEQ, GLA_HEAD_V), F32),
        ],
        compiler_params=_cparams(("arbitrary", "arbitrary")),
        name="gla_scan",
    )(qkvr_c, qkvr_c, qkvr_c, qkvr_c, low_c, qkvr_l, qkvr_l, qkvr_l, qkvr_l, low_l, w2, bg, head_gain)


def _fnet_ch_kernel(x_ref, sh_ref, sc_ref, g_ref, w_ref, y1_ref, y2_ref, *, by_phase):
    if by_phase:
        x = jnp.concatenate([x_ref[:, s, :] for s in range(DFT_RADIX)], axis=0)
    else:
        x = x_ref[...]
    h = _modulate(x, g_ref[...], sh_ref[...], sc_ref[...]).astype(BF16)
    w = w_ref[...]
    gd = FNET_GROUP_DIM
    rows = x_ref.shape[0]
    for g in range(FNET_GROUPS):
        r = _dot(h[:, g * gd:(g + 1) * gd], w).astype(BF16)
        cols = slice(g * gd, (g + 1) * gd)
        if by_phase:
            for s in range(DFT_RADIX):
                y1_ref[s, :, cols] = r[s * rows:(s + 1) * rows, :gd]
                y2_ref[s, :, cols] = r[s * rows:(s + 1) * rows, gd:]
        else:
            y1_ref[:, cols] = r[:, :gd]
            y2_ref[:, cols] = r[:, gd:]


def _fnet_channels(x, mods, gain, w_ch, by_phase):
    bq, s, _ = x.shape
    tm = FNET_TOK_TILE if by_phase else TOK_TILE
    if by_phase:
        rows = tm // DFT_RADIX
        x = x.reshape(bq, s // DFT_RADIX, DFT_RADIX, D_MODEL)
        x_spec = pl.BlockSpec((None, rows, DFT_RADIX, D_MODEL), lambda b, t: (b, t, 0, 0))
        y_spec = pl.BlockSpec((None, DFT_RADIX, rows, D_MODEL), lambda b, t: (b, 0, t, 0))
        out = jax.ShapeDtypeStruct((bq, DFT_RADIX, s // DFT_RADIX, D_MODEL), BF16)
    else:
        x_spec = y_spec = _tile_spec(False)
        out = jax.ShapeDtypeStruct((bq, s, D_MODEL), BF16)
    return pl.pallas_call(
        partial(_fnet_ch_kernel, by_phase=by_phase),
        grid=(bq, s // tm),
        in_specs=[
            x_spec, _mod_spec(0), _mod_spec(1), _row_spec(),
            pl.BlockSpec((FNET_GROUP_DIM, 2 * FNET_GROUP_DIM), lambda b, t: (0, 0)),
        ],
        out_specs=[y_spec, y_spec],
        out_shape=[out, out],
        compiler_params=_cparams(("arbitrary", "arbitrary")),
        name="fnet_channels",
    )(x, mods, mods, gain, w_ch)


def _fnet_sub_kernel(w_ref, y1_ref, y2_ref, o_ref):
    n = y1_ref.shape[0]
    y1, y2 = y1_ref[...], y2_ref[...]
    k1 = _dot(w_ref[0], y1 + y2)
    o_ref[n:, :] = (-(k1 + _dot(w_ref[1], y1))).astype(BF16)
    o_ref[:n, :] = (k1 - _dot(w_ref[2], y2)).astype(BF16)


def _fnet_sub_dft(w_sub, y1, y2):
    b, radix, n, _ = y1.shape
    y_spec = pl.BlockSpec((None, None, n, D_MODEL), lambda i, s: (i, s, 0, 0))
    return pl.pallas_call(
        _fnet_sub_kernel,
        grid=(b, radix),
        in_specs=[pl.BlockSpec((None, 3, n, n), lambda i, s: (s, 0, 0, 0)), y_spec, y_spec],
        out_specs=pl.BlockSpec((None, None, 2 * n, D_MODEL), lambda i, s: (i, s, 0, 0)),
        out_shape=jax.ShapeDtypeStruct((b, radix, 2 * n, D_MODEL), BF16),
        compiler_params=_cparams(("arbitrary", "arbitrary")),
        name="fnet_sub_dft",
    )(w_sub, y1, y2)


def _fft8_real(xs):
    r2 = float(np.sqrt(0.5))

    def add(a, b):
        return a[0] + b[0], a[1] + b[1]

    def sub(a, b):
        return a[0] - b[0], a[1] - b[1]

    def mul_neg_i(a):
        return a[1], -a[0]

    u = [add(xs[k], xs[k + 4]) for k in range(4)]
    d = [sub(xs[k], xs[k + 4]) for k in range(4)]
    v = [d[0],
         ((d[1][0] + d[1][1]) * r2, (d[1][1] - d[1][0]) * r2),
         mul_neg_i(d[2]),
         ((d[3][1] - d[3][0]) * r2, -(d[3][0] + d[3][1]) * r2)]

    def fft4_real(y):
        p0, p1 = add(y[0], y[2]), add(y[1], y[3])
        q0, q1 = sub(y[0], y[2]), mul_neg_i(sub(y[1], y[3]))
        return [p0[0] + p1[0], q0[0] + q1[0], p0[0] - p1[0], q0[0] - q1[0]]

    even, odd = fft4_real(u), fft4_real(v)
    return [even[0], odd[0], even[1], odd[1], even[2], odd[2], even[3], odd[3]]


def _fnet_combine_kernel(re_ref, im_ref, o_ref):
    bp = re_ref.shape[1]
    sub_rows = 16

    def body(rg, carry):
        rows = pl.ds(pl.multiple_of(rg * sub_rows, sub_rows), sub_rows)
        for lc in range(D_MODEL // LANES):
            lanes = slice(lc * LANES, (lc + 1) * LANES)
            xs = [(re_ref[s, rows, lanes].astype(F32), im_ref[s, rows, lanes].astype(F32))
                  for s in range(DFT_RADIX)]
            for q, z in enumerate(_fft8_real(xs)):
                o_ref[q, rows, lanes] = z.astype(BF16)
        return carry

    lax.fori_loop(0, bp // sub_rows, body, 0)


def _fnet_combine(parts):
    b, radix, n2, _ = parts.shape
    n = n2 // 2
    bp = COMBINE_ROWS
    part_spec = lambda off: pl.BlockSpec((None, radix, bp, D_MODEL), lambda i, p: (i, 0, off + p, 0))
    z = pl.pallas_call(
        _fnet_combine_kernel,
        grid=(b, n // bp),
        in_specs=[part_spec(0), part_spec(n // bp)],
        out_specs=pl.BlockSpec((None, radix, bp, D_MODEL), lambda i, p: (i, 0, p, 0)),
        out_shape=jax.ShapeDtypeStruct((b, radix, n, D_MODEL), BF16),
        compiler_params=_cparams(("arbitrary", "arbitrary")),
        name="fnet_combine",
    )(parts, parts)
    return z.reshape(b, radix * n, D_MODEL)


def _fnet_tok_kernel(gc_ref, gs_ref, y1_ref, y2_ref, o_ref, acc_ref):
    k = pl.program_id(2)

    @pl.when(k == 0)
    def _():
        acc_ref[...] = jnp.zeros_like(acc_ref)

    acc_ref[...] += _dot(gc_ref[...], y1_ref[...]) + _dot(gs_ref[...], y2_ref[...])

    @pl.when(k == pl.num_programs(2) - 1)
    def _():
        o_ref[...] = acc_ref[...].astype(BF16)


def _fnet_tokens(gc, gs, y1, y2):
    b, length, _ = y1.shape
    bm, bk = min(DFT_BM, length), min(DFT_BK, length)
    y_spec = pl.BlockSpec((None, bk, D_MODEL), lambda i, m, k: (i, k, 0))
    g_spec = pl.BlockSpec((bm, bk), lambda i, m, k: (m, k))
    return pl.pallas_call(
        _fnet_tok_kernel,
        grid=(b, length // bm, length // bk),
        in_specs=[g_spec, g_spec, y_spec, y_spec],
        out_specs=pl.BlockSpec((None, bm, D_MODEL), lambda i, m, k: (i, m, 0)),
        out_shape=jax.ShapeDtypeStruct((b, length, D_MODEL), BF16),
        scratch_shapes=[pltpu.VMEM((bm, D_MODEL), F32)],
        compiler_params=_cparams(("arbitrary", "arbitrary", "arbitrary")),
        name="fnet_tokens",
    )(gc, gs, y1, y2)


def _dft_tables(n, split):
    lp = jnp.arange(n, dtype=jnp.int32)[:, None]
    hi = jnp.arange(n // split, dtype=jnp.int32)[None, :]
    lo = jnp.arange(split, dtype=jnp.int32)[None, :]
    ang_hi = ((lp * hi) % (n // split)).astype(F32) * (2.0 * np.pi * split / n)
    ang_lo = ((lp * lo) % n).astype(F32) * (2.0 * np.pi / n)
    ch, sh_, cl, sl = jnp.cos(ang_hi), jnp.sin(ang_hi), jnp.cos(ang_lo), jnp.sin(ang_lo)
    cos = ch[:, :, None] * cl[:, None, :] - sh_[:, :, None] * sl[:, None, :]
    sin = sh_[:, :, None] * cl[:, None, :] + ch[:, :, None] * sl[:, None, :]
    return cos.reshape(n, n), sin.reshape(n, n)


def _out_ffn_kernel(x_ref, a_ref, gtm_ref, shf_ref, scf_ref, gtf_ref, g1_ref, g2_ref, g3_ref,
                    wout_ref, wg_ref, wu_ref, wd_ref, o_ref, *rest, colmajor, emit_bf16):
    h_ref, acc_ref = rest[-2:]
    k = pl.program_id(2)

    x1_ref = o_ref.reshape(TOK_TILE, D_MODEL) if colmajor else o_ref

    @pl.when(k == 0)
    def _():
        y = _dot(a_ref[...], wout_ref[...])
        x1 = x_ref[...] + gtm_ref[...] * (_rms(y) * g1_ref[...])
        x1_ref[...] = x1
        h_ref[...] = _modulate(x1, g2_ref[...], shf_ref[...], scf_ref[...]).astype(BF16)
        acc_ref[...] = jnp.zeros_like(acc_ref)

    wg, wu, wd = wg_ref[...], wu_ref[...], wd_ref[...]
    if emit_bf16:
        wg, wu, wd = wg.astype(BF16), wu.astype(BF16), wd.astype(BF16)
        rest[0][...] = wg
        rest[1][...] = wu
        rest[2][...] = wd
    h = h_ref[...]
    gate = _dot(h, wg)
    up = _dot(h, wu)
    act = (gate * jax.nn.sigmoid(gate) * up).astype(BF16)
    acc_ref[...] += _dot(act, wd)

    @pl.when(k == pl.num_programs(2) - 1)
    def _():
        out = x1_ref[...] + gtf_ref[...] * (_rms(acc_ref[...]) * g3_ref[...])
        _store_tile(o_ref, out, colmajor)


def _out_ffn(x, a, mods, g1, g2, g3, w_out, mix_layer, ffn_w, colmajor, layer=None):
    bq, s, _ = x.shape
    emit = layer is not None
    tf = FF_TILE_F32 if emit else FF_TILE
    nf = D_FF // tf
    if emit:
        assert bq * s == TOK_TILE, "each weight block must be visited exactly once"
        w_gu, w_down = ffn_w
        weights = (w_gu, w_gu, w_down)
        w_specs = [
            pl.BlockSpec((None, D_MODEL, tf), lambda b, t, k: (layer, 0, k)),
            pl.BlockSpec((None, D_MODEL, tf), lambda b, t, k: (layer, 0, nf + k)),
            pl.BlockSpec((None, tf, D_MODEL), lambda b, t, k: (layer, k, 0)),
        ]
    else:
        weights = ffn_w
        w_specs = [
            pl.BlockSpec((None, D_MODEL, tf), lambda b, t, k: (k, 0, 0)),
            pl.BlockSpec((None, D_MODEL, tf), lambda b, t, k: (k, 0, 0)),
            pl.BlockSpec((tf, D_MODEL), lambda b, t, k: (k, 0)),
        ]
    out_specs = [_tile_spec(colmajor)]
    out_shape = [jax.ShapeDtypeStruct(_as_stream(x, colmajor).shape, F32)]
    if emit:
        per = FF_TILE // tf
        out_specs += [
            pl.BlockSpec((None, D_MODEL, tf), lambda b, t, k: (k // per, 0, k % per)),
            pl.BlockSpec((None, D_MODEL, tf), lambda b, t, k: (k // per, 0, k % per)),
            pl.BlockSpec((tf, D_MODEL), lambda b, t, k: (k, 0)),
        ]
        out_shape += [
            jax.ShapeDtypeStruct((D_FF // FF_TILE, D_MODEL, FF_TILE), BF16),
            jax.ShapeDtypeStruct((D_FF // FF_TILE, D_MODEL, FF_TILE), BF16),
            jax.ShapeDtypeStruct((D_FF, D_MODEL), BF16),
        ]
    res = pl.pallas_call(
        partial(_out_ffn_kernel, colmajor=colmajor, emit_bf16=emit),
        grid=(bq, s // TOK_TILE, nf),
        in_specs=[
            _tile_spec(False), _tile_spec(False),
            _mod_spec(2), _mod_spec(3), _mod_spec(4), _mod_spec(5),
            _row_spec(), _row_spec(), _row_spec(),
            pl.BlockSpec((None, D_MODEL, D_MODEL), lambda b, t, k: (mix_layer, 0, 0),
                         pipeline_mode=pl.Buffered(1)),
        ] + w_specs,
        out_specs=out_specs,
        out_shape=out_shape,
        scratch_shapes=[
            pltpu.VMEM((TOK_TILE, D_MODEL), BF16),
            pltpu.VMEM((TOK_TILE, D_MODEL), F32),
        ],
        compiler_params=_cparams(("arbitrary", "arbitrary", "arbitrary")),
        name="out_ffn",
    )(x, a, mods, mods, mods, mods, g1, g2, g3, w_out, *weights)
    if emit:
        return res[0].reshape(x.shape), tuple(res[1:])
    return res[0].reshape(x.shape)


def _cast_gu_kernel(g_ref, u_ref, go_ref, uo_ref):
    go_ref[...] = g_ref[...].astype(BF16)
    uo_ref[...] = u_ref[...].astype(BF16)


def _cast_gate_up(w_gu, layer):
    nf = D_FF // FF_TILE
    out = jax.ShapeDtypeStruct((nf, D_MODEL, FF_TILE), BF16)
    o_spec = pl.BlockSpec((None, D_MODEL, FF_TILE), lambda k: (k, 0, 0))
    return pl.pallas_call(
        _cast_gu_kernel,
        grid=(nf,),
        in_specs=[
            pl.BlockSpec((None, D_MODEL, FF_TILE), lambda k: (layer, 0, k)),
            pl.BlockSpec((None, D_MODEL, FF_TILE), lambda k: (layer, 0, nf + k)),
        ],
        out_specs=[o_spec, o_spec],
        out_shape=[out, out],
        compiler_params=_cparams(("arbitrary",)),
        name="cast_gate_up",
    )(w_gu, w_gu)


def _wants_colmajor(i):
    return i < DEPTH and i % 2 == 0 and (i // 2) % 2 == 1


def kernel(x, c, ctx, c_ctx, ada_w, ada_b, norm_gains, gla_w_in, gla_wg2_f, gla_bg_f, gla_wg2_b, gla_bg_b,
           gla_head_gain, gla_w_out, fnet_w_out, ffn_w_gu, ffn_w_down):
    batch = x.shape[0]
    cond = jnp.zeros((ADA_ROWS, D_MODEL), F32).at[:batch].set(c).at[batch].set(c_ctx)
    mods = _ada_rows(cond, ada_w, ada_b)
    ctx_s = ctx.reshape(1, batch * CTX_LEN, D_MODEL)

    n_ch = FNET_GROUP_DIM
    cos_c, sin_c = _dft_tables(n_ch, 16)
    w_ch = (jnp.concatenate([cos_c, sin_c], axis=1) * (n_ch ** -0.5)).astype(BF16)
    cos_t, sin_t = _dft_tables(CTX_LEN, 16)
    ctx_tables = ((cos_t * (CTX_LEN ** -0.5)).astype(BF16), (sin_t * -(CTX_LEN ** -0.5)).astype(BF16))
    n_sub = SEQ // DFT_RADIX
    cos_s, sin_s = _dft_tables(n_sub, 16)
    tw_ang = ((jnp.arange(DFT_RADIX, dtype=jnp.int32)[:, None] * jnp.arange(n_sub, dtype=jnp.int32)[None, :])
              % SEQ).astype(F32) * (2.0 * np.pi / SEQ)
    tw_cos, tw_sin = jnp.cos(tw_ang)[:, :, None], jnp.sin(tw_ang)[:, :, None]
    cos_p = cos_s[None] * tw_cos - sin_s[None] * tw_sin
    sin_p = sin_s[None] * tw_cos + cos_s[None] * tw_sin
    w_sub = (jnp.stack([cos_p, sin_p - cos_p, cos_p + sin_p], axis=1) * (SEQ ** -0.5)).astype(BF16)

    gla_w_out_bf = gla_w_out.astype(BF16)
    fnet_w_out_bf = fnet_w_out.astype(BF16)

    for i in range(DEPTH):
        need_ctx = i < DEPTH - 1
        j = i // 2
        mod_lat = mods[i, :batch].reshape(batch, 1, N_ADA * D_MODEL)
        mod_ctx = mods[i, batch:batch + 1].reshape(1, 1, N_ADA * D_MODEL)
        gains = [norm_gains[i, n].reshape(1, D_MODEL) for n in range(4)]
        colmajor = _wants_colmajor(i) != _wants_colmajor(i + 1)
        if i % 2 == 0:
            w_low = lax.optimization_barrier(gla_w_in[j, :, GLA_MAIN_DIM:])
            w_low = jnp.pad(w_low, ((0, 0), (0, LANES - 2 * GLA_GATE_RANK))).astype(BF16)
            w2 = jnp.zeros((2, LANES, GLA_KEY_DIM), F32)
            w2 = w2.at[0, :GLA_GATE_RANK].set(gla_wg2_f[j]).at[1, GLA_GATE_RANK:2 * GLA_GATE_RANK].set(gla_wg2_b[j])
            bg = jnp.stack([gla_bg_f[j], gla_bg_b[j]]).reshape(2, 1, GLA_KEY_DIM)
            qkvr_c, low_c, w_in_bf = _gla_in(ctx_s, mod_ctx, gains[0], jnp.swapaxes(gla_w_in, 1, 2), w_low,
                                             False, layer=j)
            qkvr_l, low_l = _gla_in(x, mod_lat, gains[0], w_in_bf, w_low, False)
            a_ctx, a_lat = _gla_scan(
                qkvr_c.reshape(batch, CTX_LEN, GLA_MAIN_DIM), low_c.reshape(batch, CTX_LEN, LANES),
                qkvr_l, low_l, w2.astype(BF16), bg, gla_head_gain[j].reshape(1, GLA_HEAD_V))
            w_mix = gla_w_out_bf
        else:
            y1, y2 = _fnet_channels(x, mod_lat, gains[0], w_ch, True)
            a_lat = _fnet_combine(_fnet_sub_dft(w_sub, y1, y2))
            if need_ctx:
                y1, y2 = _fnet_channels(ctx_s, mod_ctx, gains[0], w_ch, False)
                a_ctx = _fnet_tokens(*ctx_tables, y1.reshape(batch, CTX_LEN, D_MODEL),
                                     y2.reshape(batch, CTX_LEN, D_MODEL))
            w_mix = fnet_w_out_bf
        if need_ctx:
            ctx_s, ffn_bf = _out_ffn(ctx_s, a_ctx.reshape(1, batch * CTX_LEN, D_MODEL), mod_ctx,
                                     gains[1], gains[2], gains[3], w_mix, j, (ffn_w_gu, ffn_w_down), False,
                                     layer=i)
        else:
            ffn_bf = (*_cast_gate_up(ffn_w_gu, i), ffn_w_down[i].astype(BF16))
        x = _out_ffn(x, a_lat, mod_lat, gains[1], gains[2], gains[3], w_mix, j, ffn_bf, colmajor)
    return x
```
